```python
import jax, jax.numpy as jnp
from jax import lax
import numpy as np

D_MODEL = 2048
BATCH = 4
SEQ = 4096
DEPTH = 1

PLE_DIM = 256
ATT_HEADS = 16
ATT_HD = 64
ATT_W = ATT_HEADS * ATT_HD
ROT_DIM = ATT_HD // 4
ROPE_THETA = 500000.0
DILATION_CFG = ((128, 1), (512, 4), (2048, 16))
HG_HEADS = 8
HG_DK = 128
HG_DV = 128
HG_W = HG_HEADS * HG_DV
HG_F = HG_HEADS * HG_DK
HG_CHUNK = 64
MIX_W = ATT_W + HG_W
SPLITS = (ATT_W, ATT_W, ATT_W, ATT_W,
          HG_F, HG_F, HG_F, HG_W, HG_W)
IN_W = sum(SPLITS)
SPLIT_IDX = [int(i) for i in np.cumsum(SPLITS)[:-1]]
EPS = 1e-6
NEG = -1e30

kernel_name = 'hybrid_dilated_attn_hgrn2_parallel_block'


def rmsnorm(x, g):
    xf = x.astype(jnp.float32)
    y = xf * lax.rsqrt(jnp.mean(xf * xf, axis=-1, keepdims=True) + EPS)
    return (y * g.astype(jnp.float32)).astype(x.dtype)


def rope_partial(x, pos):
    inv = jnp.power(ROPE_THETA, -jnp.arange(0, ROT_DIM, 2, dtype=jnp.float32) / ROT_DIM)
    ang = pos.astype(jnp.float32)[..., None] * inv
    c = jnp.cos(ang)[:, :, None, :]
    s = jnp.sin(ang)[:, :, None, :]
    x1 = x[..., :ROT_DIM // 2]
    x2 = x[..., ROT_DIM // 2:ROT_DIM]
    return jnp.concatenate([x1 * c - x2 * s, x2 * c + x1 * s, x[..., ROT_DIM:]], axis=-1)


def dilated_band_attention(q, k, v, window, dilation):
    B, S, H, E = q.shape
    half = (window // 2) // dilation
    blk = half
    L = S // dilation
    nb = -(-L // blk)
    Lp = nb * blk

    def strided(t):
        t = t.reshape(B, L, dilation, H, E).transpose(0, 2, 3, 1, 4)
        return jnp.pad(t, ((0, 0), (0, 0), (0, 0), (0, Lp - L), (0, 0)))

    def neighbours(t):
        tp = jnp.pad(t, ((0, 0), (0, 0), (0, 0), (blk, blk), (0, 0)))
        parts = [tp[:, :, :, j * blk:j * blk + Lp].reshape(B, dilation, H, nb, blk, E) for j in range(3)]
        return jnp.concatenate(parts, axis=4)

    qb = strided(q).reshape(B, dilation, H, nb, blk, E)
    kb = neighbours(strided(k))
    vb = neighbours(strided(v))
    qa = jnp.arange(nb)[:, None] * blk + jnp.arange(blk)[None, :]
    ka = jnp.arange(nb)[:, None] * blk - blk + jnp.arange(3 * blk)[None, :]
    rel = ka[:, None, :] - qa[:, :, None]
    valid = (jnp.abs(rel) <= half) & (ka[:, None, :] >= 0) & (ka[:, None, :] < L)
    s = jnp.einsum('bdhnqe,bdhnke->bdhnqk', qb, kb)
    s = jnp.where(valid, s, NEG)
    m = jnp.max(s, axis=-1, keepdims=True)
    pe = jnp.exp(s - m)
    den = jnp.sum(pe, axis=-1)
    o = jnp.einsum('bdhnqk,bdhnke->bdhnqe', pe, vb) / den[..., None]
    lse = m[..., 0] + jnp.log(den)
    o = o.reshape(B, dilation, H, Lp, E)[:, :, :, :L].transpose(0, 3, 1, 2, 4).reshape(B, S, H, E)
    lse = lse.reshape(B, dilation, H, Lp)[:, :, :, :L].transpose(0, 3, 1, 2).reshape(B, S, H)
    return o, lse


def hgrn2_chunk_scan(q, k, v, g):
    B, S, H, K = q.shape
    V = v.shape[-1]
    C = HG_CHUNK
    N = S // C

    def chunks(t):
        return t.reshape(B, N, C, H, t.shape[-1]).transpose(1, 0, 3, 2, 4)

    tri = jnp.tril(jnp.ones((C, C), dtype=bool))

    def step(state, inp):
        qc, kc, vc, gc = inp
        b = jnp.cumsum(gc, axis=2)
        diff = b[:, :, :, None, :] - b[:, :, None, :, :]
        decay = jnp.exp(jnp.where(tri[:, :, None], diff, -jnp.inf))
        att = jnp.einsum('bhtk,bhsk,bhtsk->bhts', qc, kc, decay)
        o = (jnp.einsum('bhts,bhsv->bhtv', att, vc)
             + jnp.einsum('bhtk,bhkv->bhtv', qc * jnp.exp(b), state))
        b_last = b[:, :, -1:, :]
        state = (jnp.exp(b_last[:, :, 0, :])[..., None] * state
                 + jnp.einsum('bhsk,bhsv->bhkv', kc * jnp.exp(b_last - b), vc))
        return state, o

    s0 = jnp.zeros((B, H, K, V), dtype=q.dtype)
    _, o = lax.scan(step, s0, (chunks(q), chunks(k), chunks(v), chunks(g)))
    return o.transpose(1, 0, 3, 2, 4).reshape(B, S, H, V)


def hgrn2_direction(qh, vh, f_logit, lb):
    f = lb + (1.0 - lb) * jax.nn.sigmoid(f_logit)
    return hgrn2_chunk_scan(qh, 1.0 - f, vh, jnp.log(f))


def hybrid_layer(h, p_l, pos, w_in, w_out, g_pre, g_post, g_hg, lb_f, lb_b, w_pg, w_pp, g_ple):
    B, S, _ = h.shape
    f32 = jnp.float32
    u = rmsnorm(h, g_pre)
    z = u @ w_in
    aq, ak, av, ag, hq, hf_f, hf_b, hv, hg = jnp.split(z, SPLIT_IDX, axis=-1)

    def heads(t, n):
        return t.reshape(B, S, n, -1).astype(f32)

    q = rope_partial(heads(aq, ATT_HEADS), pos) * (ATT_HD ** -0.5)
    k = rope_partial(heads(ak, ATT_HEADS), pos)
    v = heads(av, ATT_HEADS)
    res = [dilated_band_attention(q, k, v, w, d) for (w, d) in DILATION_CFG]
    outs = jnp.stack([r[0] for r in res])
    lses = jnp.stack([r[1] for r in res])
    alpha = jax.nn.softmax(lses, axis=0)
    o_att = jnp.einsum('gbsh,gbshe->bshe', alpha, outs).reshape(B, S, ATT_W)
    y_att = o_att * jax.nn.silu(ag.astype(f32))

    qh = jax.nn.silu(heads(hq, HG_HEADS))
    vh = heads(hv, HG_HEADS)
    o_fw = hgrn2_direction(qh, vh, heads(hf_f, HG_HEADS), lb_f)
    o_bw = jnp.flip(hgrn2_direction(jnp.flip(qh, 1), jnp.flip(vh, 1),
                                    jnp.flip(heads(hf_b, HG_HEADS), 1), lb_b), 1)
    o_hg = rmsnorm(o_fw + o_bw, g_hg).reshape(B, S, HG_W)
    y_hg = o_hg * jax.nn.silu(hg.astype(f32))

    y = jnp.concatenate([y_att, y_hg], axis=-1).astype(h.dtype) @ w_out
    h = h + rmsnorm(y, g_post)

    gate = jax.nn.sigmoid((h @ w_pg).astype(f32))
    e = (p_l @ w_pp).astype(f32)
    h = h + rmsnorm((gate * e).astype(h.dtype), g_ple)
    return h


def setup_inputs(seed: int = 0) -> dict:
    key = jax.random.key(seed)
    ks = jax.random.split(key, 16)
    nrm = jax.random.normal
    x = nrm(ks[0], (BATCH, SEQ, D_MODEL), jnp.float32)
    p = nrm(ks[1], (DEPTH, BATCH, SEQ, PLE_DIM), jnp.float32)
    positions = (jnp.arange(SEQ, dtype=jnp.int32)[None, :]
                 + jax.random.randint(ks[2], (BATCH, 1), 0, 1024, dtype=jnp.int32))
    w_in = nrm(ks[3], (DEPTH, D_MODEL, IN_W), jnp.float32) * D_MODEL ** -0.5
    w_out = nrm(ks[4], (DEPTH, MIX_W, D_MODEL), jnp.float32) * MIX_W ** -0.5
    g_pre = 1.0 + 0.05 * nrm(ks[5], (DEPTH, D_MODEL), jnp.float32)
    g_post = 1.0 + 0.05 * nrm(ks[6], (DEPTH, D_MODEL), jnp.float32)
    g_hg = 1.0 + 0.05 * nrm(ks[7], (DEPTH, HG_DV), jnp.float32)
    lb_fwd = 0.1 * nrm(ks[8], (DEPTH + 1, HG_F), jnp.float32)
    lb_bwd = 0.1 * nrm(ks[9], (DEPTH + 1, HG_F), jnp.float32)
    w_pg = nrm(ks[10], (DEPTH, D_MODEL, D_MODEL), jnp.float32) * D_MODEL ** -0.5
    w_pp = nrm(ks[11], (DEPTH, PLE_DIM, D_MODEL), jnp.float32) * PLE_DIM ** -0.5
    g_ple = 1.0 + 0.05 * nrm(ks[12], (DEPTH, D_MODEL), jnp.float32)
    return {'x': x, 'p': p, 'positions': positions, 'w_in': w_in, 'w_out': w_out,
            'g_pre': g_pre, 'g_post': g_post, 'g_hg': g_hg, 'lb_fwd': lb_fwd, 'lb_bwd': lb_bwd,
            'w_pg': w_pg, 'w_pp': w_pp, 'g_ple': g_ple}


def reference(x, p, positions, w_in, w_out, g_pre, g_post, g_hg, lb_fwd, lb_bwd, w_pg, w_pp, g_ple):
    lb_f_all = jnp.cumsum(jax.nn.softmax(lb_fwd.astype(jnp.float32), axis=0), axis=0)
    lb_b_all = jnp.cumsum(jax.nn.softmax(lb_bwd.astype(jnp.float32), axis=0), axis=0)
    h = x
    for i in range(DEPTH):
        h = hybrid_layer(h, p[i], positions, w_in[i], w_out[i], g_pre[i], g_post[i], g_hg[i],
                         lb_f_all[i].reshape(HG_HEADS, HG_DK), lb_b_all[i].reshape(HG_HEADS, HG_DK),
                         w_pg[i], w_pp[i], g_ple[i])
    return h
```

```python
import functools

import jax
import jax.numpy as jnp
from jax import lax
from jax.experimental import pallas as pl
from jax.experimental.pallas import tpu as pltpu

F32 = jnp.float32
BF16 = jnp.bfloat16

D_MODEL = 2048
PLE_DIM = 256
ATT_HEADS = 16
ATT_HD = 64
ATT_W = ATT_HEADS * ATT_HD
ROT_DIM = ATT_HD // 4
ROPE_THETA = 500000.0
DILATION_CFG = ((128, 1), (512, 4), (2048, 16))
DILATIONS = tuple(d for _, d in DILATION_CFG)
HALF = (DILATION_CFG[0][0] // 2) // DILATION_CFG[0][1]
assert all((w // 2) // d == HALF for w, d in DILATION_CFG)
HG_HEADS = 8
HG_DK = 128
HG_W = HG_HEADS * HG_DK
HG_CHUNK = 64
IN_W = 4 * ATT_W + 5 * HG_W
EPS = 1e-6
NEG = -1e30

LANES = 128
VMEM_LIMIT = 56 * 1024 * 1024

COL_AQ, COL_AK, COL_AV, COL_AG = (i * ATT_W // LANES for i in range(4))
COL_HQ, COL_HFF, COL_HFB, COL_HV, COL_HG = (4 * ATT_W // LANES + i * HG_W // LANES for i in range(5))

ROPE_TR = 1024
INPROJ_TM = 1024
INPROJ_TN = 512
ATT_TILE = 1024
ATT_QB = HALF
ATT_KB = 3 * HALF
HG_PRE_ROWS = 512
OUT_TM = 256
HG_FAST_MIN_LOGDECAY = -120.0


def _bdot(a, b):
    return jnp.dot(a.astype(BF16), b.astype(BF16), preferred_element_type=F32)


def _bdot_nt(a, b):
    return lax.dot_general(a.astype(BF16), b.astype(BF16), (((1,), (1,)), ((), ())),
                           preferred_element_type=F32)


def _bdot_tn(a, b):
    return lax.dot_general(a.astype(BF16), b.astype(BF16), (((0,), (0,)), ((), ())),
                           preferred_element_type=F32)


def _silu(x):
    return x * jax.nn.sigmoid(x)


def _rope_table_kernel(pos_ref, inv_ref, cos_ref, sa_ref, sb_ref):
    ang = pos_ref[...].astype(F32) * inv_ref[...]
    lane = lax.broadcasted_iota(jnp.int32, ang.shape, 1) & (ATT_HD - 1)
    c = jnp.cos(ang)
    s = jnp.sin(ang)
    half = ROT_DIM // 2
    cos_ref[...] = jnp.where(lane < ROT_DIM, c, 1.0)
    sa_ref[...] = jnp.where(lane < half, -s, 0.0)
    sb_ref[...] = jnp.where((lane >= half) & (lane < ROT_DIM), s, 0.0)


def _rope_tables(positions):
    t = positions.size
    pos = positions.reshape(t, 1)
    inv = jnp.power(ROPE_THETA, -jnp.arange(0, ROT_DIM, 2, dtype=F32) / ROT_DIM)
    inv_lane = jnp.tile(inv, LANES // inv.shape[0]).reshape(1, LANES)
    tr = min(ROPE_TR, t)
    out = jax.ShapeDtypeStruct((t, LANES), F32)
    return pl.pallas_call(
        _rope_table_kernel,
        grid=(t // tr,),
        in_specs=[pl.BlockSpec((tr, 1), lambda i: (i, 0)),
                  pl.BlockSpec((1, LANES), lambda i: (0, 0))],
        out_specs=[pl.BlockSpec((tr, LANES), lambda i: (i, 0))] * 3,
        out_shape=[out] * 3,
        name="rope_tables",
    )(pos, inv_lane)


def _inproj_kernel(x_ref, g_ref, w_ref, cos_ref, sa_ref, sb_ref, z_ref, u_ref, *, tm, tn):
    j = pl.program_id(1)
    rb = 128

    @pl.when(j == 0)
    def _():
        def body(i, carry):
            rows = pl.ds(pl.multiple_of(i * rb, rb), rb)
            xb = x_ref[rows, :]
            ms = jnp.mean(xb * xb, axis=-1, keepdims=True)
            u_ref[rows, :] = ((xb * lax.rsqrt(ms + EPS)) * g_ref[...]).astype(BF16)
            return carry
        lax.fori_loop(0, tm // rb, body, 0)

    z_ref[...] = jnp.dot(u_ref[...], w_ref[...], preferred_element_type=F32)

    n_q = ATT_W // tn
    n_rope = 2 * ATT_W // tn

    @pl.when(j < n_rope)
    def _():
        scale = jnp.where(j < n_q, ATT_HD ** -0.5, 1.0).astype(F32)

        def body(i, carry):
            rows = pl.ds(pl.multiple_of(i * rb, rb), rb)
            c = cos_ref[rows, :]
            sa = sa_ref[rows, :]
            sb = sb_ref[rows, :]
            for cb in range(tn // LANES):
                cols = slice(cb * LANES, (cb + 1) * LANES)
                zc = z_ref[rows, cols]
                hi = pltpu.roll(zc, LANES - ROT_DIM // 2, 1)
                lo = pltpu.roll(zc, ROT_DIM // 2, 1)
                z_ref[rows, cols] = (zc * c + hi * sa + lo * sb) * scale
            return carry
        lax.fori_loop(0, tm // rb, body, 0)


def _inproj(x2, g_pre, w_in_bf16, cos_t, sa_t, sb_t):
    t = x2.shape[0]
    tm = min(INPROJ_TM, t)
    tn = INPROJ_TN
    row = lambda i, j: (i, 0)
    return pl.pallas_call(
        functools.partial(_inproj_kernel, tm=tm, tn=tn),
        grid=(t // tm, IN_W // tn),
        in_specs=[pl.BlockSpec((tm, D_MODEL), row),
                  pl.BlockSpec((1, D_MODEL), lambda i, j: (0, 0)),
                  pl.BlockSpec((D_MODEL, tn), lambda i, j: (0, j)),
                  pl.BlockSpec((tm, LANES), row),
                  pl.BlockSpec((tm, LANES), row),
                  pl.BlockSpec((tm, LANES), row)],
        out_specs=pl.BlockSpec((tm, tn), lambda i, j: (i, j)),
        out_shape=jax.ShapeDtypeStruct((t, IN_W), F32),
        scratch_shapes=[pltpu.VMEM((tm, D_MODEL), BF16)],
        compiler_params=pltpu.CompilerParams(
            dimension_semantics=("parallel", "arbitrary"), vmem_limit_bytes=VMEM_LIMIT),
        name="inproj",
    )(x2, g_pre.reshape(1, D_MODEL), w_in_bf16, cos_t, sa_t, sb_t)


def _att_kernel(q_ref, k_ref, v_ref, g_ref, y_ref, m_s, l_s, n_s, *, seq):
    n_tiles = seq // ATT_TILE
    lane = lax.broadcasted_iota(jnp.int32, (ATT_QB, LANES), 1)
    head_a = lane < ATT_HD
    qa_off = lax.broadcasted_iota(jnp.int32, (ATT_QB, ATT_KB), 0)
    ka_off = lax.broadcasted_iota(jnp.int32, (ATT_QB, ATT_KB), 1)

    def rows_of(start, size, d):
        return pl.ds(start, size) if d == 1 else pl.ds(start, size, stride=d)

    def band_block(gi, d, tile0, r, a0):
        n_cls = seq // d
        ka0 = jnp.clip(a0 - HALF, 0, n_cls - ATT_KB)
        q2 = q_ref[rows_of(r + d * a0, ATT_QB, d), :]
        kk = k_ref[rows_of(r + d * ka0, ATT_KB, d), :].astype(BF16)
        vv = v_ref[rows_of(r + d * ka0, ATT_KB, d), :].astype(BF16)
        valid = jnp.abs((ka0 + ka_off) - (a0 + qa_off)) <= HALF

        def one_head(mask):
            s = _bdot_nt(jnp.where(mask, q2, 0.0), kk)
            s = jnp.where(valid, s, NEG)
            m = jnp.max(s, axis=-1, keepdims=True)
            p = jnp.exp(s - m)
            l = jnp.sum(p, axis=-1, keepdims=True)
            return m, l, _bdot(p, vv)

        ma, la, na = one_head(head_a)
        mb, lb, nb = one_head(jnp.logical_not(head_a))
        dst = rows_of(r + d * a0 - tile0, ATT_QB, d)
        m_s[gi, dst, :] = jnp.where(head_a, ma, mb)
        l_s[gi, dst, :] = jnp.where(head_a, la, lb)
        n_s[gi, dst, :] = jnp.where(head_a, na, nb)

    def tile_body(tj, carry):
        tile0 = tj * ATT_TILE
        for gi, d in enumerate(DILATIONS):
            per_cls = ATT_TILE // d // ATT_QB

            def blk_body(bi, c, gi=gi, d=d, per_cls=per_cls):
                r = bi // per_cls
                a0 = tile0 // d + (bi % per_cls) * ATT_QB
                band_block(gi, d, tile0, r, a0)
                return c
            lax.fori_loop(0, ATT_TILE // ATT_QB, blk_body, 0)

        cb = 256

        def comb_body(ci, c):
            rows = pl.ds(pl.multiple_of(ci * cb, cb), cb)
            ms = [m_s[gi, rows, :] for gi in range(len(DILATIONS))]
            mx = functools.reduce(jnp.maximum, ms)
            num = jnp.zeros((cb, LANES), F32)
            den = jnp.zeros((cb, LANES), F32)
            for gi in range(len(DILATIONS)):
                w = jnp.exp(ms[gi] - mx)
                num = num + w * n_s[gi, rows, :]
                den = den + w * l_s[gi, rows, :]
            orow = pl.ds(pl.multiple_of(tile0 + ci * cb, cb), cb)
            y_ref[orow, :] = ((num / den) * _silu(g_ref[orow, :])).astype(y_ref.dtype)
            return c
        lax.fori_loop(0, ATT_TILE // cb, comb_body, 0)
        return carry

    lax.fori_loop(0, n_tiles, tile_body, 0)


def _attention(z, batch, seq):
    t = batch * seq
    n_pairs = ATT_W // LANES

    def col(c0):
        return pl.BlockSpec((seq, LANES), lambda b, h, c0=c0: (b, c0 + h))

    scratch = pltpu.VMEM((len(DILATIONS), ATT_TILE, LANES), F32)
    return pl.pallas_call(
        functools.partial(_att_kernel, seq=seq),
        grid=(batch, n_pairs),
        in_specs=[col(COL_AQ), col(COL_AK), col(COL_AV), col(COL_AG)],
        out_specs=pl.BlockSpec((seq, LANES), lambda b, h: (b, h)),
        out_shape=jax.ShapeDtypeStruct((t, ATT_W), BF16),
        scratch_shapes=[scratch, scratch, scratch],
        compiler_params=pltpu.CompilerParams(
            dimension_semantics=("parallel", "parallel"), vmem_limit_bytes=VMEM_LIMIT),
        name="band_attention",
    )(z, z, z, z)


def _hgrn_kernel(hq_ref, ff_ref, fb_ref, hv_ref, hg_ref, lbf_ref, lbb_ref, ghg_ref, y_ref,
                 q_s, gf_s, gb_s, kf_s, kb_s, o_s, stf_s, stb_s, b_s, *, seq, layer):
    C = HG_CHUNK
    n_chunks = seq // C

    def lower_bound(ref):
        a = ref[...]
        e = jnp.exp(a - jnp.max(a, axis=0, keepdims=True))
        return jnp.sum(e[0:layer + 1, :], axis=0, keepdims=True) / jnp.sum(e, axis=0, keepdims=True)

    lbf = lower_bound(lbf_ref)
    lbb = lower_bound(lbb_ref)

    pr = HG_PRE_ROWS

    def pre_body(i, lo):
        rows = pl.ds(pl.multiple_of(i * pr, pr), pr)
        q_s[rows, :] = _silu(hq_ref[rows, :])
        o_s[rows, :] = jnp.zeros((pr, HG_DK), F32)
        for f_ref, lb, g_s, k_s in ((ff_ref, lbf, gf_s, kf_s), (fb_ref, lbb, gb_s, kb_s)):
            f = lb + (1.0 - lb) * jax.nn.sigmoid(f_ref[rows, :])
            g = jnp.log(f)
            g_s[rows, :] = g
            k_s[rows, :] = 1.0 - f
            lo = jnp.minimum(lo, jnp.sum(g.reshape(pr // C, C, HG_DK), axis=1))
        return lo

    lo = lax.fori_loop(0, seq // pr, pre_body, jnp.zeros((pr // C, HG_DK), F32))
    fast = jnp.min(lo) >= HG_FAST_MIN_LOGDECAY

    stf_s[...] = jnp.zeros_like(stf_s)
    stb_s[...] = jnp.zeros_like(stb_s)

    ti = lax.broadcasted_iota(jnp.int32, (C, C), 0)
    si = lax.broadcasted_iota(jnp.int32, (C, C), 1)
    tri_f = si <= ti
    tri_b = si >= ti

    def chunk(r0, tri, last, g_s, k_s, st_ref, exact):
        rows = pl.ds(r0, C)
        g = g_s[rows, :]
        q = q_s[rows, :]
        kk = k_s[rows, :]
        v = hv_ref[rows, :]
        b = jnp.dot(tri.astype(F32), g, precision=lax.Precision.HIGHEST,
                    preferred_element_type=F32)
        btot = b[last:last + 1, :]
        if exact:
            b_s[...] = b

            def col_body(s, att):
                bs = b_s[pl.ds(s, 1), :]
                ks = k_s[pl.ds(r0 + s, 1), :]
                w = q * ks * jnp.exp(jnp.minimum(b - bs, 0.0))
                col = jnp.sum(w, axis=-1, keepdims=True)
                return att + jnp.where(si == s, col, 0.0)
            att = lax.fori_loop(0, C, col_body, jnp.zeros((C, C), F32))
        else:
            mid = 0.5 * btot
            att = _bdot_nt(q * jnp.exp(b - mid), kk * jnp.exp(mid - b))
        att = jnp.where(tri, att, 0.0)
        st = st_ref[...]
        o = _bdot(att, v) + _bdot_nt(q * jnp.exp(b), st)
        st_ref[...] = st * jnp.exp(btot) + _bdot_tn(v, kk * jnp.exp(btot - b))
        o_s[rows, :] = o_s[rows, :] + o

    def scan(exact):
        def body(n, carry):
            chunk(pl.multiple_of(n * C, C), tri_f, C - 1, gf_s, kf_s, stf_s, exact)
            chunk(pl.multiple_of((n_chunks - 1 - n) * C, C), tri_b, 0, gb_s, kb_s, stb_s, exact)
            return carry
        lax.fori_loop(0, n_chunks, body, 0)

    lax.cond(fast, lambda: scan(False), lambda: scan(True))

    def post_body(i, carry):
        rows = pl.ds(pl.multiple_of(i * pr, pr), pr)
        o = o_s[rows, :]
        ms = jnp.mean(o * o, axis=-1, keepdims=True)
        y = (o * lax.rsqrt(ms + EPS)) * ghg_ref[...]
        y_ref[rows, :] = (y * _silu(hg_ref[rows, :])).astype(y_ref.dtype)
        return carry
    lax.fori_loop(0, seq // pr, post_body, 0)


def _hgrn(z, lb_fwd, lb_bwd, g_hg, batch, seq, layer):
    t = batch * seq

    def col(c0):
        return pl.BlockSpec((seq, LANES), lambda b, h, c0=c0: (b, c0 + h))

    n_lb = lb_fwd.shape[0]
    lb_spec = pl.BlockSpec((n_lb, HG_DK), lambda b, h: (0, h))
    seq_f32 = pltpu.VMEM((seq, HG_DK), F32)
    state = pltpu.VMEM((HG_DK, HG_DK), F32)
    return pl.pallas_call(
        functools.partial(_hgrn_kernel, seq=seq, layer=layer),
        grid=(batch, HG_HEADS),
        in_specs=[col(COL_HQ), col(COL_HFF), col(COL_HFB), col(COL_HV), col(COL_HG),
                  lb_spec, lb_spec, pl.BlockSpec((1, HG_DK), lambda b, h: (0, 0))],
        out_specs=pl.BlockSpec((seq, HG_DK), lambda b, h: (b, h)),
        out_shape=jax.ShapeDtypeStruct((t, HG_W), BF16),
        scratch_shapes=[seq_f32] * 6 + [state, state, pltpu.VMEM((HG_CHUNK, HG_DK), F32)],
        compiler_params=pltpu.CompilerParams(
            dimension_semantics=("parallel", "parallel"), vmem_limit_bytes=VMEM_LIMIT),
        name="hgrn2",
    )(z, z, z, z, z, lb_fwd, lb_bwd, g_hg.reshape(1, HG_DK))


def _rms(y, g):
    ms = jnp.mean(y * y, axis=-1, keepdims=True)
    return (y * lax.rsqrt(ms + EPS)) * g


def _outproj_kernel(x_ref, ya_ref, yh_ref, p_ref, wo_ref, wpg_ref, wpp_ref, gpost_ref, gple_ref, o_ref):
    y = (jnp.dot(ya_ref[...], wo_ref[0:ATT_W, :], preferred_element_type=F32)
         + jnp.dot(yh_ref[...], wo_ref[ATT_W:, :], preferred_element_type=F32))
    h1 = x_ref[...] + _rms(y, gpost_ref[...])
    gate = jax.nn.sigmoid(jnp.dot(h1.astype(BF16), wpg_ref[...], preferred_element_type=F32))
    e = jnp.dot(p_ref[...].astype(BF16), wpp_ref[...], preferred_element_type=F32)
    o_ref[...] = h1 + _rms(gate * e, gple_ref[...])


def _outproj(x2, y_att, y_hg, p2, w_out, w_pg, w_pp, g_post, g_ple):
    t = x2.shape[0]
    tm = min(OUT_TM, t)
    row = lambda i: (i, 0)
    const = lambda i: (0, 0)
    resident = dict(pipeline_mode=pl.Buffered(1))
    return pl.pallas_call(
        _outproj_kernel,
        grid=(t // tm,),
        in_specs=[pl.BlockSpec((tm, D_MODEL), row),
                  pl.BlockSpec((tm, ATT_W), row),
                  pl.BlockSpec((tm, HG_W), row),
                  pl.BlockSpec((tm, PLE_DIM), row),
                  pl.BlockSpec((ATT_W + HG_W, D_MODEL), const, **resident),
                  pl.BlockSpec((D_MODEL, D_MODEL), const, **resident),
                  pl.BlockSpec((PLE_DIM, D_MODEL), const, **resident),
                  pl.BlockSpec((1, D_MODEL), const),
                  pl.BlockSpec((1, D_MODEL), const)],
        out_specs=pl.BlockSpec((tm, D_MODEL), row),
        out_shape=jax.ShapeDtypeStruct((t, D_MODEL), F32),
        compiler_params=pltpu.CompilerParams(
            dimension_semantics=("parallel",), vmem_limit_bytes=VMEM_LIMIT),
        name="outproj_ple",
    )(x2, y_att, y_hg, p2, w_out, w_pg, w_pp, g_post.reshape(1, D_MODEL), g_ple.reshape(1, D_MODEL))


def kernel(x, p, positions, w_in, w_out, g_pre, g_post, g_hg, lb_fwd, lb_bwd, w_pg, w_pp, g_ple):
    depth = w_in.shape[0]
    batch, seq, _ = x.shape
    assert seq % ATT_TILE == 0 and seq // max(DILATIONS) >= ATT_KB
    t = batch * seq
    cos_t, sa_t, sb_t = _rope_tables(positions)
    h = x.reshape(t, D_MODEL)
    for i in range(depth):
        z = _inproj(h, g_pre[i], w_in[i].astype(BF16), cos_t, sa_t, sb_t)
        y_att = _attention(z, batch, seq)
        y_hg = _hgrn(z, lb_fwd, lb_bwd, g_hg[i], batch, seq, i)
        h = _outproj(h, y_att, y_hg, p[i].reshape(t, PLE_DIM), w_out[i].astype(BF16),
                     w_pg[i].astype(BF16), w_pp[i].astype(BF16), g_post[i], g_ple[i])
    return h.reshape(batch, seq, D_MODEL)
```

```python
import functools

import jax
import jax.numpy as jnp
from jax import lax
from jax.experimental import pallas as pl
from jax.experimental.pallas import tpu as pltpu

F32 = jnp.float32
BF16 = jnp.bfloat16

D_MODEL = 2048
PLE_DIM = 256
ATT_HEADS = 16
ATT_HD = 64
ATT_W = ATT_HEADS * ATT_HD
ROT_DIM = ATT_HD // 4
ROPE_THETA = 500000.0
DILATION_CFG = ((128, 1), (512, 4), (2048, 16))
DILATIONS = tuple(d for _, d in DILATION_CFG)
HALF = (DILATION_CFG[0][0] // 2) // DILATION_CFG[0][1]
assert all((w // 2) // d == HALF for w, d in DILATION_CFG)
HG_HEADS = 8
HG_DK = 128
HG_W = HG_HEADS * HG_DK
HG_CHUNK = 64
IN_W = 4 * ATT_W + 5 * HG_W
EPS = 1e-6
NEG = -1e30

LANES = 128
VMEM_LIMIT = 56 * 1024 * 1024

COL_AQ, COL_AK, COL_AV, COL_AG = (i * ATT_W // LANES for i in range(4))
COL_HQ, COL_HFF, COL_HFB, COL_HV, COL_HG = (4 * ATT_W // LANES + i * HG_W // LANES for i in range(5))

ROPE_TR = 1024
INPROJ_TM = 1024
INPROJ_TN = 512
ATT_TILE = 1024
ATT_QB = HALF
ATT_KB = 3 * HALF
ATT_UNROLL = 4
HG_UNROLL = 2
HG_PRE_ROWS = 512
OUT_TM = 256
HG_FAST_MIN_LOGDECAY = -120.0


def _bdot(a, b):
    return jnp.dot(a.astype(BF16), b.astype(BF16), preferred_element_type=F32)


def _bdot_nt(a, b):
    return lax.dot_general(a.astype(BF16), b.astype(BF16), (((1,), (1,)), ((), ())),
                           preferred_element_type=F32)


def _bdot_tn(a, b):
    return lax.dot_general(a.astype(BF16), b.astype(BF16), (((0,), (0,)), ((), ())),
                           preferred_element_type=F32)


def _silu(x):
    return x * jax.nn.sigmoid(x)


def _rope_table_kernel(pos_ref, inv_ref, cos_ref, sa_ref, sb_ref):
    ang = pos_ref[...].astype(F32) * inv_ref[...]
    lane = lax.broadcasted_iota(jnp.int32, ang.shape, 1) & (ATT_HD - 1)
    c = jnp.cos(ang)
    s = jnp.sin(ang)
    half = ROT_DIM // 2
    cos_ref[...] = jnp.where(lane < ROT_DIM, c, 1.0)
    sa_ref[...] = jnp.where(lane < half, -s, 0.0)
    sb_ref[...] = jnp.where((lane >= half) & (lane < ROT_DIM), s, 0.0)


def _rope_tables(positions):
    t = positions.size
    pos = positions.reshape(t, 1)
    inv = jnp.power(ROPE_THETA, -jnp.arange(0, ROT_DIM, 2, dtype=F32) / ROT_DIM)
    inv_lane = jnp.tile(inv, LANES // inv.shape[0]).reshape(1, LANES)
    tr = min(ROPE_TR, t)
    out = jax.ShapeDtypeStruct((t, LANES), F32)
    return pl.pallas_call(
        _rope_table_kernel,
        grid=(t // tr,),
        in_specs=[pl.BlockSpec((tr, 1), lambda i: (i, 0)),
                  pl.BlockSpec((1, LANES), lambda i: (0, 0))],
        out_specs=[pl.BlockSpec((tr, LANES), lambda i: (i, 0))] * 3,
        out_shape=[out] * 3,
        name="rope_tables",
    )(pos, inv_lane)


def _inproj_kernel(x_ref, g_ref, w_ref, cos_ref, sa_ref, sb_ref, z_ref, u_ref, *, tm, tn):
    j = pl.program_id(1)
    rb = 128

    @pl.when(j == 0)
    def _():
        def body(i, carry):
            rows = pl.ds(pl.multiple_of(i * rb, rb), rb)
            xb = x_ref[rows, :]
            ms = jnp.mean(xb * xb, axis=-1, keepdims=True)
            u_ref[rows, :] = ((xb * lax.rsqrt(ms + EPS)) * g_ref[...]).astype(BF16)
            return carry
        lax.fori_loop(0, tm // rb, body, 0)

    z_ref[...] = jnp.dot(u_ref[...], w_ref[...], preferred_element_type=F32)

    n_q = ATT_W // tn
    n_rope = 2 * ATT_W // tn

    @pl.when(j < n_rope)
    def _():
        scale = jnp.where(j < n_q, ATT_HD ** -0.5, 1.0).astype(F32)

        def body(i, carry):
            rows = pl.ds(pl.multiple_of(i * rb, rb), rb)
            c = cos_ref[rows, :]
            sa = sa_ref[rows, :]
            sb = sb_ref[rows, :]
            for cb in range(tn // LANES):
                cols = slice(cb * LANES, (cb + 1) * LANES)
                zc = z_ref[rows, cols]
                hi = pltpu.roll(zc, LANES - ROT_DIM // 2, 1)
                lo = pltpu.roll(zc, ROT_DIM // 2, 1)
                z_ref[rows, cols] = (zc * c + hi * sa + lo * sb) * scale
            return carry
        lax.fori_loop(0, tm // rb, body, 0)


def _inproj(x2, g_pre, w_in_bf16, cos_t, sa_t, sb_t):
    t = x2.shape[0]
    tm = min(INPROJ_TM, t)
    tn = INPROJ_TN
    row = lambda i, j: (i, 0)
    return pl.pallas_call(
        functools.partial(_inproj_kernel, tm=tm, tn=tn),
        grid=(t // tm, IN_W // tn),
        in_specs=[pl.BlockSpec((tm, D_MODEL), row),
                  pl.BlockSpec((1, D_MODEL), lambda i, j: (0, 0)),
                  pl.BlockSpec((D_MODEL, tn), lambda i, j: (0, j)),
                  pl.BlockSpec((tm, LANES), row),
                  pl.BlockSpec((tm, LANES), row),
                  pl.BlockSpec((tm, LANES), row)],
        out_specs=pl.BlockSpec((tm, tn), lambda i, j: (i, j)),
        out_shape=jax.ShapeDtypeStruct((t, IN_W), F32),
        scratch_shapes=[pltpu.VMEM((tm, D_MODEL), BF16)],
        compiler_params=pltpu.CompilerParams(
            dimension_semantics=("parallel", "arbitrary"), vmem_limit_bytes=VMEM_LIMIT),
        name="inproj",
    )(x2, g_pre.reshape(1, D_MODEL), w_in_bf16, cos_t, sa_t, sb_t)


def _att_kernel(q_ref, k_ref, v_ref, g_ref, y_ref, m_s, l_s, n_s, *, seq):
    n_tiles = seq // ATT_TILE
    lane = lax.broadcasted_iota(jnp.int32, (ATT_QB, LANES), 1)
    head_a = lane < ATT_HD
    qa_off = lax.broadcasted_iota(jnp.int32, (2 * ATT_QB, ATT_KB), 0) & (ATT_QB - 1)
    ka_off = lax.broadcasted_iota(jnp.int32, (2 * ATT_QB, ATT_KB), 1)
    ones_kv = jnp.ones((ATT_KB, LANES), BF16)

    def rows_of(start, size, d):
        return pl.ds(start, size) if d == 1 else pl.ds(start, size, stride=d)

    def band_block(gi, d, tile0, r, a0):
        n_cls = seq // d
        ka0 = jnp.clip(a0 - HALF, 0, n_cls - ATT_KB)
        q2 = q_ref[rows_of(r + d * a0, ATT_QB, d), :]
        kk = k_ref[rows_of(r + d * ka0, ATT_KB, d), :].astype(BF16)
        vv = v_ref[rows_of(r + d * ka0, ATT_KB, d), :].astype(BF16)
        valid = jnp.abs((ka0 + ka_off) - (a0 + qa_off)) <= HALF
        qs = jnp.concatenate([jnp.where(head_a, q2, 0.0), jnp.where(head_a, 0.0, q2)], axis=0)
        s = jnp.where(valid, _bdot_nt(qs, kk), NEG)
        m = jnp.max(s, axis=-1, keepdims=True)
        p = jnp.exp(s - m).astype(BF16)
        n = jnp.dot(p, vv, preferred_element_type=F32)
        l = jnp.dot(p, ones_kv, preferred_element_type=F32)
        dst = rows_of(r + d * a0 - tile0, ATT_QB, d)
        m_s[gi, dst, :] = jnp.where(head_a, m[:ATT_QB], m[ATT_QB:])
        l_s[gi, dst, :] = jnp.where(head_a, l[:ATT_QB], l[ATT_QB:])
        n_s[gi, dst, :] = jnp.where(head_a, n[:ATT_QB], n[ATT_QB:])

    def tile_body(tj, carry):
        tile0 = tj * ATT_TILE
        for gi, d in enumerate(DILATIONS):
            per_cls = ATT_TILE // d // ATT_QB

            def blk_body(bi, c, gi=gi, d=d, per_cls=per_cls):
                r = bi // per_cls
                a0 = tile0 // d + (bi % per_cls) * ATT_QB
                band_block(gi, d, tile0, r, a0)
                return c
            lax.fori_loop(0, ATT_TILE // ATT_QB, blk_body, 0, unroll=ATT_UNROLL)

        cb = 256

        def comb_body(ci, c):
            rows = pl.ds(pl.multiple_of(ci * cb, cb), cb)
            ms = [m_s[gi, rows, :] for gi in range(len(DILATIONS))]
            mx = functools.reduce(jnp.maximum, ms)
            num = jnp.zeros((cb, LANES), F32)
            den = jnp.zeros((cb, LANES), F32)
            for gi in range(len(DILATIONS)):
                w = jnp.exp(ms[gi] - mx)
                num = num + w * n_s[gi, rows, :]
                den = den + w * l_s[gi, rows, :]
            orow = pl.ds(pl.multiple_of(tile0 + ci * cb, cb), cb)
            y_ref[orow, :] = ((num / den) * _silu(g_ref[orow, :])).astype(y_ref.dtype)
            return c
        lax.fori_loop(0, ATT_TILE // cb, comb_body, 0)
        return carry

    lax.fori_loop(0, n_tiles, tile_body, 0)


def _attention(z, batch, seq):
    t = batch * seq
    n_pairs = ATT_W // LANES

    def col(c0):
        return pl.BlockSpec((seq, LANES), lambda b, h, c0=c0: (b, c0 + h))

    scratch = pltpu.VMEM((len(DILATIONS), ATT_TILE, LANES), F32)
    return pl.pallas_call(
        functools.partial(_att_kernel, seq=seq),
        grid=(batch, n_pairs),
        in_specs=[col(COL_AQ), col(COL_AK), col(COL_AV), col(COL_AG)],
        out_specs=pl.BlockSpec((seq, LANES), lambda b, h: (b, h)),
        out_shape=jax.ShapeDtypeStruct((t, ATT_W), BF16),
        scratch_shapes=[scratch, scratch, scratch],
        compiler_params=pltpu.CompilerParams(
            dimension_semantics=("parallel", "parallel"), vmem_limit_bytes=VMEM_LIMIT),
        name="band_attention",
    )(z, z, z, z)


def _hgrn_kernel(hq_ref, ff_ref, fb_ref, hv_ref, hg_ref, lbf_ref, lbb_ref, ghg_ref, y_ref,
                 q_s, gf_s, gb_s, kf_s, kb_s, o_s, stf_s, stb_s, b_s, *, seq, layer):
    C = HG_CHUNK
    n_chunks = seq // C

    def lower_bound(ref):
        a = ref[...]
        e = jnp.exp(a - jnp.max(a, axis=0, keepdims=True))
        return jnp.sum(e[0:layer + 1, :], axis=0, keepdims=True) / jnp.sum(e, axis=0, keepdims=True)

    lbf = lower_bound(lbf_ref)
    lbb = lower_bound(lbb_ref)

    pr = HG_PRE_ROWS

    def pre_body(i, lo):
        rows = pl.ds(pl.multiple_of(i * pr, pr), pr)
        q_s[rows, :] = _silu(hq_ref[rows, :])
        o_s[rows, :] = jnp.zeros((pr, HG_DK), F32)
        for f_ref, lb, g_s, k_s in ((ff_ref, lbf, gf_s, kf_s), (fb_ref, lbb, gb_s, kb_s)):
            f = lb + (1.0 - lb) * jax.nn.sigmoid(f_ref[rows, :])
            g = jnp.log(f)
            g_s[rows, :] = g
            k_s[rows, :] = 1.0 - f
            lo = jnp.minimum(lo, jnp.sum(g.reshape(pr // C, C, HG_DK), axis=1))
        return lo

    lo = lax.fori_loop(0, seq // pr, pre_body, jnp.zeros((pr // C, HG_DK), F32))
    fast = jnp.min(lo) >= HG_FAST_MIN_LOGDECAY

    stf_s[...] = jnp.zeros_like(stf_s)
    stb_s[...] = jnp.zeros_like(stb_s)

    ti = lax.broadcasted_iota(jnp.int32, (C, C), 0)
    si = lax.broadcasted_iota(jnp.int32, (C, C), 1)
    tri_f = si <= ti
    tri_b = si >= ti

    def chunk(r0, tri, last, g_s, k_s, st_ref, exact):
        rows = pl.ds(r0, C)
        g = g_s[rows, :]
        q = q_s[rows, :]
        kk = k_s[rows, :]
        v = hv_ref[rows, :]
        b = jnp.dot(tri.astype(F32), g, precision=lax.Precision.HIGHEST,
                    preferred_element_type=F32)
        btot = b[last:last + 1, :]
        if exact:
            b_s[...] = b

            def col_body(s, att):
                bs = b_s[pl.ds(s, 1), :]
                ks = k_s[pl.ds(r0 + s, 1), :]
                w = q * ks * jnp.exp(jnp.minimum(b - bs, 0.0))
                col = jnp.sum(w, axis=-1, keepdims=True)
                return att + jnp.where(si == s, col, 0.0)
            att = lax.fori_loop(0, C, col_body, jnp.zeros((C, C), F32))
        else:
            mid = 0.5 * btot
            att = _bdot_nt(q * jnp.exp(b - mid), kk * jnp.exp(mid - b))
        att = jnp.where(tri, att, 0.0)
        st = st_ref[...]
        o = _bdot(att, v) + _bdot_nt(q * jnp.exp(b), st)
        st_ref[...] = st * jnp.exp(btot) + _bdot_tn(v, kk * jnp.exp(btot - b))
        o_s[rows, :] = o_s[rows, :] + o

    def scan(exact):
        def body(n, carry):
            chunk(pl.multiple_of(n * C, C), tri_f, C - 1, gf_s, kf_s, stf_s, exact)
            chunk(pl.multiple_of((n_chunks - 1 - n) * C, C), tri_b, 0, gb_s, kb_s, stb_s, exact)
            return carry
        lax.fori_loop(0, n_chunks, body, 0, unroll=1 if exact else HG_UNROLL)

    lax.cond(fast, lambda: scan(False), lambda: scan(True))

    def post_body(i, carry):
        rows = pl.ds(pl.multiple_of(i * pr, pr), pr)
        o = o_s[rows, :]
        ms = jnp.mean(o * o, axis=-1, keepdims=True)
        y = (o * lax.rsqrt(ms + EPS)) * ghg_ref[...]
        y_ref[rows, :] = (y * _silu(hg_ref[rows, :])).astype(y_ref.dtype)
        return carry
    lax.fori_loop(0, seq // pr, post_body, 0)


def _hgrn(z, lb_fwd, lb_bwd, g_hg, batch, seq, layer):
    t = batch * seq

    def col(c0):
        return pl.BlockSpec((seq, LANES), lambda b, h, c0=c0: (b, c0 + h))

    n_lb = lb_fwd.shape[0]
    lb_spec = pl.BlockSpec((n_lb, HG_DK), lambda b, h: (0, h))
    seq_f32 = pltpu.VMEM((seq, HG_DK), F32)
    state = pltpu.VMEM((HG_DK, HG_DK), F32)
    return pl.pallas_call(
        functools.partial(_hgrn_kernel, seq=seq, layer=layer),
        grid=(batch, HG_HEADS),
        in_specs=[col(COL_HQ), col(COL_HFF), col(COL_HFB), col(COL_HV), col(COL_HG),
                  lb_spec, lb_spec, pl.BlockSpec((1, HG_DK), lambda b, h: (0, 0))],
        out_specs=pl.BlockSpec((seq, HG_DK), lambda b, h: (b, h)),
        out_shape=jax.ShapeDtypeStruct((t, HG_W), BF16),
        scratch_shapes=[seq_f32] * 6 + [state, state, pltpu.VMEM((HG_CHUNK, HG_DK), F32)],
        compiler_params=pltpu.CompilerParams(
            dimension_semantics=("parallel", "parallel"), vmem_limit_bytes=VMEM_LIMIT),
        name="hgrn2",
    )(z, z, z, z, z, lb_fwd, lb_bwd, g_hg.reshape(1, HG_DK))


def _rms(y, g):
    ms = jnp.mean(y * y, axis=-1, keepdims=True)
    return (y * lax.rsqrt(ms + EPS)) * g


def _outproj_kernel(x_ref, ya_ref, yh_ref, p_ref, wo_ref, wpg_ref, wpp_ref, gpost_ref, gple_ref, o_ref):
    y = (jnp.dot(ya_ref[...], wo_ref[0:ATT_W, :], preferred_element_type=F32)
         + jnp.dot(yh_ref[...], wo_ref[ATT_W:, :], preferred_element_type=F32))
    h1 = x_ref[...] + _rms(y, gpost_ref[...])
    gate = jax.nn.sigmoid(jnp.dot(h1.astype(BF16), wpg_ref[...], preferred_element_type=F32))
    e = jnp.dot(p_ref[...].astype(BF16), wpp_ref[...], preferred_element_type=F32)
    o_ref[...] = h1 + _rms(gate * e, gple_ref[...])


def _outproj(x2, y_att, y_hg, p2, w_out, w_pg, w_pp, g_post, g_ple):
    t = x2.shape[0]
    tm = min(OUT_TM, t)
    row = lambda i: (i, 0)
    const = lambda i: (0, 0)
    resident = dict(pipeline_mode=pl.Buffered(1))
    return pl.pallas_call(
        _outproj_kernel,
        grid=(t // tm,),
        in_specs=[pl.BlockSpec((tm, D_MODEL), row),
                  pl.BlockSpec((tm, ATT_W), row),
                  pl.BlockSpec((tm, HG_W), row),
                  pl.BlockSpec((tm, PLE_DIM), row),
                  pl.BlockSpec((ATT_W + HG_W, D_MODEL), const, **resident),
                  pl.BlockSpec((D_MODEL, D_MODEL), const, **resident),
                  pl.BlockSpec((PLE_DIM, D_MODEL), const, **resident),
                  pl.BlockSpec((1, D_MODEL), const),
                  pl.BlockSpec((1, D_MODEL), const)],
        out_specs=pl.BlockSpec((tm, D_MODEL), row),
        out_shape=jax.ShapeDtypeStruct((t, D_MODEL), F32),
        compiler_params=pltpu.CompilerParams(
            dimension_semantics=("parallel",), vmem_limit_bytes=VMEM_LIMIT),
        name="outproj_ple",
    )(x2, y_att, y_hg, p2, w_out, w_pg, w_pp, g_post.reshape(1, D_MODEL), g_ple.reshape(1, D_MODEL))


def kernel(x, p, positions, w_in, w_out, g_pre, g_post, g_hg, lb_fwd, lb_bwd, w_pg, w_pp, g_ple):
    depth = w_in.shape[0]
    batch, seq, _ = x.shape
    assert seq % ATT_TILE == 0 and seq // max(DILATIONS) >= ATT_KB
    t = batch * seq
    cos_t, sa_t, sb_t = _rope_tables(positions)
    h = x.reshape(t, D_MODEL)
    for i in range(depth):
        z = _inproj(h, g_pre[i], w_in[i].astype(BF16), cos_t, sa_t, sb_t)
        y_att = _attention(z, batch, seq)
        y_hg = _hgrn(z, lb_fwd, lb_bwd, g_hg[i], batch, seq, i)
        h = _outproj(h, y_att, y_hg, p[i].reshape(t, PLE_DIM), w_out[i].astype(BF16),
                     w_pg[i].astype(BF16), w_pp[i].astype(BF16), g_post[i], g_ple[i])
    return h.reshape(batch, seq, D_MODEL)
```

```python
import functools

import jax
import jax.numpy as jnp
from jax import lax
from jax.experimental import pallas as pl
from jax.experimental.pallas import tpu as pltpu

F32 = jnp.float32
BF16 = jnp.bfloat16

D_MODEL = 2048
PLE_DIM = 256
ATT_HEADS = 16
ATT_HD = 64
ATT_W = ATT_HEADS * ATT_HD
ROT_DIM = ATT_HD // 4
ROPE_THETA = 500000.0
DILATION_CFG = ((128, 1), (512, 4), (2048, 16))
DILATIONS = tuple(d for _, d in DILATION_CFG)
HALF = (DILATION_CFG[0][0] // 2) // DILATION_CFG[0][1]
assert all((w // 2) // d == HALF for w, d in DILATION_CFG)
HG_HEADS = 8
HG_DK = 128
HG_W = HG_HEADS * HG_DK
HG_CHUNK = 64
IN_W = 4 * ATT_W + 5 * HG_W
EPS = 1e-6
NEG = -1e30
LOG2E = 1.4426950408889634

LANES = 128
VMEM_LIMIT = 56 * 1024 * 1024

COL_AQ, COL_AK, COL_AV, COL_AG = (i * ATT_W // LANES for i in range(4))
COL_HQ, COL_HFF, COL_HFB, COL_HV, COL_HG = (4 * ATT_W // LANES + i * HG_W // LANES for i in range(5))

ROPE_TR = 1024
INPROJ_TM = 1024
INPROJ_TN = 512
ATT_TILE = 1024
ATT_QB = HALF
ATT_KB = 3 * HALF
ATT_UNROLL = 16
HG_UNROLL = 4
HG_PRE_ROWS = 512
OUT_TM = 256
HG_FAST_MIN_LOGDECAY = -120.0


def _bdot(a, b):
    return jnp.dot(a.astype(BF16), b.astype(BF16), preferred_element_type=F32)


def _bdot_nt(a, b):
    return lax.dot_general(a.astype(BF16), b.astype(BF16), (((1,), (1,)), ((), ())),
                           preferred_element_type=F32)


def _bdot_tn(a, b):
    return lax.dot_general(a.astype(BF16), b.astype(BF16), (((0,), (0,)), ((), ())),
                           preferred_element_type=F32)


def _silu(x):
    return x * jax.nn.sigmoid(x)


def _rope_table_kernel(pos_ref, inv_ref, cos_ref, sa_ref, sb_ref):
    ang = pos_ref[...].astype(F32) * inv_ref[...]
    lane = lax.broadcasted_iota(jnp.int32, ang.shape, 1) & (ATT_HD - 1)
    c = jnp.cos(ang)
    s = jnp.sin(ang)
    half = ROT_DIM // 2
    cos_ref[...] = jnp.where(lane < ROT_DIM, c, 1.0)
    sa_ref[...] = jnp.where(lane < half, -s, 0.0)
    sb_ref[...] = jnp.where((lane >= half) & (lane < ROT_DIM), s, 0.0)


def _rope_tables(positions):
    t = positions.size
    pos = positions.reshape(t, 1)
    inv = jnp.power(ROPE_THETA, -jnp.arange(0, ROT_DIM, 2, dtype=F32) / ROT_DIM)
    inv_lane = jnp.tile(inv, LANES // inv.shape[0]).reshape(1, LANES)
    tr = min(ROPE_TR, t)
    out = jax.ShapeDtypeStruct((t, LANES), F32)
    return pl.pallas_call(
        _rope_table_kernel,
        grid=(t // tr,),
        in_specs=[pl.BlockSpec((tr, 1), lambda i: (i, 0)),
                  pl.BlockSpec((1, LANES), lambda i: (0, 0))],
        out_specs=[pl.BlockSpec((tr, LANES), lambda i: (i, 0))] * 3,
        out_shape=[out] * 3,
        name="rope_tables",
    )(pos, inv_lane)


def _inproj_kernel(x_ref, g_ref, w_ref, cos_ref, sa_ref, sb_ref, z_ref, u_ref, *, tm, tn):
    j = pl.program_id(1)
    rb = 128

    @pl.when(j == 0)
    def _():
        def body(i, carry):
            rows = pl.ds(pl.multiple_of(i * rb, rb), rb)
            xb = x_ref[rows, :]
            ms = jnp.mean(xb * xb, axis=-1, keepdims=True)
            u_ref[rows, :] = ((xb * lax.rsqrt(ms + EPS)) * g_ref[...]).astype(BF16)
            return carry
        lax.fori_loop(0, tm // rb, body, 0)

    z_ref[...] = jnp.dot(u_ref[...], w_ref[...], preferred_element_type=F32)

    n_q = ATT_W // tn
    n_rope = 2 * ATT_W // tn

    @pl.when(j < n_rope)
    def _():
        scale = jnp.where(j < n_q, ATT_HD ** -0.5 * LOG2E, 1.0).astype(F32)

        def body(i, carry):
            rows = pl.ds(pl.multiple_of(i * rb, rb), rb)
            c = cos_ref[rows, :]
            sa = sa_ref[rows, :]
            sb = sb_ref[rows, :]
            for cb in range(tn // LANES):
                cols = slice(cb * LANES, (cb + 1) * LANES)
                zc = z_ref[rows, cols]
                hi = pltpu.roll(zc, LANES - ROT_DIM // 2, 1)
                lo = pltpu.roll(zc, ROT_DIM // 2, 1)
                z_ref[rows, cols] = (zc * c + hi * sa + lo * sb) * scale
            return carry
        lax.fori_loop(0, tm // rb, body, 0)


def _inproj(x2, g_pre, w_in_bf16, cos_t, sa_t, sb_t):
    t = x2.shape[0]
    tm = min(INPROJ_TM, t)
    tn = INPROJ_TN
    row = lambda i, j: (i, 0)
    return pl.pallas_call(
        functools.partial(_inproj_kernel, tm=tm, tn=tn),
        grid=(t // tm, IN_W // tn),
        in_specs=[pl.BlockSpec((tm, D_MODEL), row),
                  pl.BlockSpec((1, D_MODEL), lambda i, j: (0, 0)),
                  pl.BlockSpec((D_MODEL, tn), lambda i, j: (0, j)),
                  pl.BlockSpec((tm, LANES), row),
                  pl.BlockSpec((tm, LANES), row),
                  pl.BlockSpec((tm, LANES), row)],
        out_specs=pl.BlockSpec((tm, tn), lambda i, j: (i, j)),
        out_shape=jax.ShapeDtypeStruct((t, IN_W), F32),
        scratch_shapes=[pltpu.VMEM((tm, D_MODEL), BF16)],
        compiler_params=pltpu.CompilerParams(
            dimension_semantics=("parallel", "arbitrary"), vmem_limit_bytes=VMEM_LIMIT),
        name="inproj",
    )(x2, g_pre.reshape(1, D_MODEL), w_in_bf16, cos_t, sa_t, sb_t)


def _att_kernel(q_ref, k_ref, v_ref, g_ref, y_ref, m_s, l_s, n_s, bias_s, p_s, *, seq):
    n_tiles = seq // ATT_TILE
    lane = lax.broadcasted_iota(jnp.int32, (ATT_QB, LANES), 1)
    head_a = lane < ATT_HD
    qa_off = lax.broadcasted_iota(jnp.int32, (2 * ATT_QB, ATT_KB), 0) & (ATT_QB - 1)
    ka_off = lax.broadcasted_iota(jnp.int32, (2 * ATT_QB, ATT_KB), 1)
    ones_kv = jnp.ones((ATT_KB, LANES), BF16)
    for idx, delta in enumerate((-HALF, 0, -2 * HALF)):
        bias_s[idx] = jnp.where(jnp.abs(ka_off - qa_off + delta) <= HALF, 0.0, NEG)

    def rows_of(start, size, d):
        return pl.ds(start, size) if d == 1 else pl.ds(start, size, stride=d)

    def store_heads(ref, gi, dst, x):
        ref[gi, dst, :] = jnp.where(head_a, x[:ATT_QB], x[ATT_QB:])

    def score_block(bi, gi, d, tile0, r, a0):
        n_cls = seq // d
        ka0 = jnp.clip(a0 - HALF, 0, n_cls - ATT_KB)
        q2 = q_ref[rows_of(r + d * a0, ATT_QB, d), :]
        kk = k_ref[rows_of(r + d * ka0, ATT_KB, d), :].astype(BF16)
        which = jnp.where(a0 == 0, 1, jnp.where(a0 == n_cls - ATT_QB, 2, 0))
        qs = jnp.concatenate([jnp.where(head_a, q2, 0.0), jnp.where(head_a, 0.0, q2)], axis=0)
        s = _bdot_nt(qs, kk) + bias_s[which]
        m = jnp.max(s, axis=-1, keepdims=True)
        p_s[bi] = jnp.exp2(s - m).astype(BF16)
        dst = rows_of(r + d * a0 - tile0, ATT_QB, d)
        store_heads(m_s, gi, dst, jnp.broadcast_to(m, (2 * ATT_QB, LANES)))

    def value_block(bi, gi, d, tile0, r, a0):
        n_cls = seq // d
        ka0 = jnp.clip(a0 - HALF, 0, n_cls - ATT_KB)
        vv = v_ref[rows_of(r + d * ka0, ATT_KB, d), :].astype(BF16)
        nl = jnp.dot(p_s[bi], jnp.concatenate([vv, ones_kv], axis=1), preferred_element_type=F32)
        n = nl[:, :LANES]
        l = nl[:, LANES:]
        dst = rows_of(r + d * a0 - tile0, ATT_QB, d)
        store_heads(l_s, gi, dst, l)
        store_heads(n_s, gi, dst, n)

    def tile_body(tj, carry):
        tile0 = tj * ATT_TILE
        for gi, d in enumerate(DILATIONS):
            per_cls = ATT_TILE // d // ATT_QB
            for block_fn in (score_block, value_block):
                def blk_body(bi, c, gi=gi, d=d, per_cls=per_cls, block_fn=block_fn):
                    r = bi // per_cls
                    a0 = tile0 // d + (bi % per_cls) * ATT_QB
                    block_fn(bi, gi, d, tile0, r, a0)
                    return c
                lax.fori_loop(0, ATT_TILE // ATT_QB, blk_body, 0, unroll=ATT_UNROLL)

        cb = 256

        def comb_body(ci, c):
            rows = pl.ds(pl.multiple_of(ci * cb, cb), cb)
            ms = [m_s[gi, rows, :] for gi in range(len(DILATIONS))]
            mx = functools.reduce(jnp.maximum, ms)
            num = jnp.zeros((cb, LANES), F32)
            den = jnp.zeros((cb, LANES), F32)
            for gi in range(len(DILATIONS)):
                w = jnp.exp2(ms[gi] - mx)
                num = num + w * n_s[gi, rows, :]
                den = den + w * l_s[gi, rows, :]
            orow = pl.ds(pl.multiple_of(tile0 + ci * cb, cb), cb)
            y_ref[orow, :] = ((num / den) * _silu(g_ref[orow, :])).astype(y_ref.dtype)
            return c
        lax.fori_loop(0, ATT_TILE // cb, comb_body, 0)
        return carry

    lax.fori_loop(0, n_tiles, tile_body, 0)


def _attention(z, batch, seq):
    t = batch * seq
    n_pairs = ATT_W // LANES

    def col(c0):
        return pl.BlockSpec((seq, LANES), lambda b, h, c0=c0: (b, c0 + h))

    scratch = pltpu.VMEM((len(DILATIONS), ATT_TILE, LANES), F32)
    return pl.pallas_call(
        functools.partial(_att_kernel, seq=seq),
        grid=(batch, n_pairs),
        in_specs=[col(COL_AQ), col(COL_AK), col(COL_AV), col(COL_AG)],
        out_specs=pl.BlockSpec((seq, LANES), lambda b, h: (b, h)),
        out_shape=jax.ShapeDtypeStruct((t, ATT_W), BF16),
        scratch_shapes=[scratch, scratch, scratch, pltpu.VMEM((3, 2 * ATT_QB, ATT_KB), F32),
                        pltpu.VMEM((ATT_TILE // ATT_QB, 2 * ATT_QB, ATT_KB), BF16)],
        compiler_params=pltpu.CompilerParams(
            dimension_semantics=("parallel", "parallel"), vmem_limit_bytes=VMEM_LIMIT),
        name="band_attention",
    )(z, z, z, z)


def _chunk_cumsum(g, reverse):
    n = g.shape[0]
    row = lax.broadcasted_iota(jnp.int32, g.shape, 0) & (HG_CHUNK - 1)
    x = g
    sh = 1
    while sh < HG_CHUNK:
        if reverse:
            x = x + jnp.where(row < HG_CHUNK - sh, pltpu.roll(x, n - sh, 0), 0.0)
        else:
            x = x + jnp.where(row >= sh, pltpu.roll(x, sh, 0), 0.0)
        sh *= 2
    return x


def _hgrn_kernel(hq_ref, ff_ref, fb_ref, hv_ref, hg_ref, lbf_ref, lbb_ref, ghg_ref, y_ref,
                 q_s, kf_s, kb_s, bf_s, bb_s, o_s,
                 qtf_s, ktf_s, qef_s, kdf_s, qtb_s, ktb_s, qeb_s, kdb_s, v16_s,
                 ebf_s, ebb_s, stf_s, stb_s, *, seq, layer):
    C = HG_CHUNK
    n_chunks = seq // C

    def lower_bound(ref):
        a = ref[...]
        e = jnp.exp(a - jnp.max(a, axis=0, keepdims=True))
        return jnp.sum(e[0:layer + 1, :], axis=0, keepdims=True) / jnp.sum(e, axis=0, keepdims=True)

    lbf = lower_bound(lbf_ref)
    lbb = lower_bound(lbb_ref)

    pr = HG_PRE_ROWS

    cpb = pr // C
    fwd = (ff_ref, lbf, kf_s, bf_s, qtf_s, ktf_s, qef_s, kdf_s, ebf_s, False)
    bwd = (fb_ref, lbb, kb_s, bb_s, qtb_s, ktb_s, qeb_s, kdb_s, ebb_s, True)

    def per_chunk_rows(x3):
        return jnp.broadcast_to(x3, (cpb, C, HG_DK)).reshape(pr, HG_DK)

    def pre_body(i, lo):
        rows = pl.ds(pl.multiple_of(i * pr, pr), pr)
        q = _silu(hq_ref[rows, :])
        q_s[rows, :] = q
        o_s[rows, :] = jnp.zeros((pr, HG_DK), F32)
        v16_s[rows, :] = hv_ref[rows, :].astype(BF16)
        for f_ref, lb, k_s, b_s, qt_s, kt_s, qe_s, kd_s, eb_s, reverse in (fwd, bwd):
            f = lb + (1.0 - lb) * jax.nn.sigmoid(f_ref[rows, :])
            kk = 1.0 - f
            b = _chunk_cumsum(jnp.log(f), reverse)
            last = 0 if reverse else C - 1
            btot = b.reshape(cpb, C, HG_DK)[:, last:last + 1, :]
            mid = 0.5 * btot
            midr = per_chunk_rows(mid)
            emid = per_chunk_rows(jnp.exp(mid))
            qt = q * jnp.exp(b - midr)
            kt = kk * jnp.exp(midr - b)
            k_s[rows, :] = kk
            b_s[rows, :] = b
            qt_s[rows, :] = qt.astype(BF16)
            kt_s[rows, :] = kt.astype(BF16)
            qe_s[rows, :] = (qt * emid).astype(BF16)
            kd_s[rows, :] = (kt * emid).astype(BF16)
            eb_s[pl.ds(pl.multiple_of(i * cpb, cpb), cpb), :] = jnp.exp(btot).reshape(cpb, HG_DK)
            lo = jnp.minimum(lo, btot.reshape(cpb, HG_DK))
        return lo

    lo = lax.fori_loop(0, seq // pr, pre_body, jnp.zeros((cpb, HG_DK), F32))
    fast = jnp.min(lo) >= HG_FAST_MIN_LOGDECAY

    stf_s[...] = jnp.zeros_like(stf_s)
    stb_s[...] = jnp.zeros_like(stb_s)

    ti = lax.broadcasted_iota(jnp.int32, (C, C), 0)
    si = lax.broadcasted_iota(jnp.int32, (C, C), 1)
    tri_f = si <= ti
    tri_b = si >= ti

    def chunk(ci, tri, d, st_ref, exact):
        _, _, k_s, b_s, qt_s, kt_s, qe_s, kd_s, eb_s, _ = d
        r0 = pl.multiple_of(ci * C, C)
        rows = pl.ds(r0, C)
        if exact:
            q = q_s[rows, :]
            b = b_s[rows, :]

            def col_body(s, att):
                w = q * k_s[pl.ds(r0 + s, 1), :] * jnp.exp(jnp.minimum(b - b_s[pl.ds(r0 + s, 1), :], 0.0))
                return att + jnp.where(si == s, jnp.sum(w, axis=-1, keepdims=True), 0.0)
            att = lax.fori_loop(0, C, col_body, jnp.zeros((C, C), F32))
        else:
            att = _bdot_nt(qt_s[rows, :], kt_s[rows, :])
        att = jnp.where(tri, att, 0.0)
        v = v16_s[rows, :]
        st = st_ref[...]
        o = _bdot(att, v) + _bdot_nt(qe_s[rows, :], st)
        st_ref[...] = st * eb_s[pl.ds(ci, 1), :] + _bdot_tn(v, kd_s[rows, :])
        o_s[rows, :] = o_s[rows, :] + o

    def scan(exact):
        def body(n, carry):
            chunk(n, tri_f, fwd, stf_s, exact)
            chunk(n_chunks - 1 - n, tri_b, bwd, stb_s, exact)
            return carry
        lax.fori_loop(0, n_chunks, body, 0, unroll=1 if exact else HG_UNROLL)

    lax.cond(fast, lambda: scan(False), lambda: scan(True))

    def post_body(i, carry):
        rows = pl.ds(pl.multiple_of(i * pr, pr), pr)
        o = o_s[rows, :]
        ms = jnp.mean(o * o, axis=-1, keepdims=True)
        y = (o * lax.rsqrt(ms + EPS)) * ghg_ref[...]
        y_ref[rows, :] = (y * _silu(hg_ref[rows, :])).astype(y_ref.dtype)
        return carry
    lax.fori_loop(0, seq // pr, post_body, 0)


def _hgrn(z, lb_fwd, lb_bwd, g_hg, batch, seq, layer):
    t = batch * seq

    def col(c0):
        return pl.BlockSpec((seq, LANES), lambda b, h, c0=c0: (b, c0 + h))

    n_lb = lb_fwd.shape[0]
    lb_spec = pl.BlockSpec((n_lb, HG_DK), lambda b, h: (0, h))
    seq_f32 = pltpu.VMEM((seq, HG_DK), F32)
    seq_bf16 = pltpu.VMEM((seq, HG_DK), BF16)
    chunk_f32 = pltpu.VMEM((seq // HG_CHUNK, HG_DK), F32)
    state = pltpu.VMEM((HG_DK, HG_DK), F32)
    return pl.pallas_call(
        functools.partial(_hgrn_kernel, seq=seq, layer=layer),
        grid=(batch, HG_HEADS),
        in_specs=[col(COL_HQ), col(COL_HFF), col(COL_HFB), col(COL_HV), col(COL_HG),
                  lb_spec, lb_spec, pl.BlockSpec((1, HG_DK), lambda b, h: (0, 0))],
        out_specs=pl.BlockSpec((seq, HG_DK), lambda b, h: (b, h)),
        out_shape=jax.ShapeDtypeStruct((t, HG_W), BF16),
        scratch_shapes=[seq_f32] * 6 + [seq_bf16] * 9 + [chunk_f32, chunk_f32, state, state],
        compiler_params=pltpu.CompilerParams(
            dimension_semantics=("parallel", "parallel"), vmem_limit_bytes=VMEM_LIMIT),
        name="hgrn2",
    )(z, z, z, z, z, lb_fwd, lb_bwd, g_hg.reshape(1, HG_DK))


def _rms(y, g):
    ms = jnp.mean(y * y, axis=-1, keepdims=True)
    return (y * lax.rsqrt(ms + EPS)) * g


def _outproj_kernel(x_ref, ya_ref, yh_ref, p_ref, wo_ref, wpg_ref, wpp_ref, gpost_ref, gple_ref, o_ref):
    y = (jnp.dot(ya_ref[...], wo_ref[0:ATT_W, :], preferred_element_type=F32)
         + jnp.dot(yh_ref[...], wo_ref[ATT_W:, :], preferred_element_type=F32))
    h1 = x_ref[...] + _rms(y, gpost_ref[...])
    gate = jax.nn.sigmoid(jnp.dot(h1.astype(BF16), wpg_ref[...], preferred_element_type=F32))
    e = jnp.dot(p_ref[...].astype(BF16), wpp_ref[...], preferred_element_type=F32)
    o_ref[...] = h1 + _rms(gate * e, gple_ref[...])


def _outproj(x2, y_att, y_hg, p2, w_out, w_pg, w_pp, g_post, g_ple):
    t = x2.shape[0]
    tm = min(OUT_TM, t)
    row = lambda i: (i, 0)
    const = lambda i: (0, 0)
    resident = dict(pipeline_mode=pl.Buffered(1))
    return pl.pallas_call(
        _outproj_kernel,
        grid=(t // tm,),
        in_specs=[pl.BlockSpec((tm, D_MODEL), row),
                  pl.BlockSpec((tm, ATT_W), row),
                  pl.BlockSpec((tm, HG_W), row),
                  pl.BlockSpec((tm, PLE_DIM), row),
                  pl.BlockSpec((ATT_W + HG_W, D_MODEL), const, **resident),
                  pl.BlockSpec((D_MODEL, D_MODEL), const, **resident),
                  pl.BlockSpec((PLE_DIM, D_MODEL), const, **resident),
                  pl.BlockSpec((1, D_MODEL), const),
                  pl.BlockSpec((1, D_MODEL), const)],
        out_specs=pl.BlockSpec((tm, D_MODEL), row),
        out_shape=jax.ShapeDtypeStruct((t, D_MODEL), F32),
        compiler_params=pltpu.CompilerParams(
            dimension_semantics=("parallel",), vmem_limit_bytes=VMEM_LIMIT),
        name="outproj_ple",
    )(x2, y_att, y_hg, p2, w_out, w_pg, w_pp, g_post.reshape(1, D_MODEL), g_ple.reshape(1, D_MODEL))


def kernel(x, p, positions, w_in, w_out, g_pre, g_post, g_hg, lb_fwd, lb_bwd, w_pg, w_pp, g_ple):
    depth = w_in.shape[0]
    batch, seq, _ = x.shape
    assert seq % ATT_TILE == 0 and seq // max(DILATIONS) >= ATT_KB
    t = batch * seq
    cos_t, sa_t, sb_t = _rope_tables(positions)
    h = x.reshape(t, D_MODEL)
    for i in range(depth):
        z = _inproj(h, g_pre[i], w_in[i].astype(BF16), cos_t, sa_t, sb_t)
        y_att = _attention(z, batch, seq)
        y_hg = _hgrn(z, lb_fwd, lb_bwd, g_hg[i], batch, seq, i)
        h = _outproj(h, y_att, y_hg, p[i].reshape(t, PLE_DIM), w_out[i].astype(BF16),
                     w_pg[i].astype(BF16), w_pp[i].astype(BF16), g_post[i], g_ple[i])
    return h.reshape(batch, seq, D_MODEL)
```

```python
import functools

import jax
import jax.numpy as jnp
from jax import lax
from jax.experimental import pallas as pl
from jax.experimental.pallas import tpu as pltpu

F32 = jnp.float32
BF16 = jnp.bfloat16

D_MODEL = 2048
PLE_DIM = 256
ATT_HEADS = 16
ATT_HD = 64
ATT_W = ATT_HEADS * ATT_HD
ROT_DIM = ATT_HD // 4
ROPE_THETA = 500000.0
DILATION_CFG = ((128, 1), (512, 4), (2048, 16))
DILATIONS = tuple(d for _, d in DILATION_CFG)
HALF = (DILATION_CFG[0][0] // 2) // DILATION_CFG[0][1]
assert all((w // 2) // d == HALF for w, d in DILATION_CFG)
HG_HEADS = 8
HG_DK = 128
HG_W = HG_HEADS * HG_DK
HG_CHUNK = 128
IN_W = 4 * ATT_W + 5 * HG_W
EPS = 1e-6
NEG = -1e30
LOG2E = 1.4426950408889634

LANES = 128
VMEM_LIMIT = 56 * 1024 * 1024

COL_AQ, COL_AK, COL_AV, COL_AG = (i * ATT_W // LANES for i in range(4))
COL_HQ, COL_HFF, COL_HFB, COL_HV, COL_HG = (4 * ATT_W // LANES + i * HG_W // LANES for i in range(5))

ROPE_TR = 1024
INPROJ_TM = 1024
INPROJ_TN = 512
ATT_TILE = 1024
ATT_QB = HALF
ATT_KB = 3 * HALF
ATT_UNROLL = 16
HG_UNROLL = 4
HG_PRE_ROWS = 512
OUT_TM = 256
HG_FAST_MIN_LOGDECAY = -120.0


def _bdot(a, b):
    return jnp.dot(a.astype(BF16), b.astype(BF16), preferred_element_type=F32)


def _bdot_nt(a, b):
    return lax.dot_general(a.astype(BF16), b.astype(BF16), (((1,), (1,)), ((), ())),
                           preferred_element_type=F32)


def _bdot_tn(a, b):
    return lax.dot_general(a.astype(BF16), b.astype(BF16), (((0,), (0,)), ((), ())),
                           preferred_element_type=F32)


def _sigmoid(x):
    return 0.5 * jnp.tanh(0.5 * x) + 0.5


def _silu(x):
    return x * _sigmoid(x)


def _rope_table_kernel(pos_ref, inv_ref, cos_ref, sa_ref, sb_ref):
    ang = pos_ref[...].astype(F32) * inv_ref[...]
    lane = lax.broadcasted_iota(jnp.int32, ang.shape, 1) & (ATT_HD - 1)
    c = jnp.cos(ang)
    s = jnp.sin(ang)
    half = ROT_DIM // 2
    cos_ref[...] = jnp.where(lane < ROT_DIM, c, 1.0)
    sa_ref[...] = jnp.where(lane < half, -s, 0.0)
    sb_ref[...] = jnp.where((lane >= half) & (lane < ROT_DIM), s, 0.0)


def _rope_tables(positions):
    t = positions.size
    pos = positions.reshape(t, 1)
    inv = jnp.power(ROPE_THETA, -jnp.arange(0, ROT_DIM, 2, dtype=F32) / ROT_DIM)
    inv_lane = jnp.tile(inv, LANES // inv.shape[0]).reshape(1, LANES)
    tr = min(ROPE_TR, t)
    out = jax.ShapeDtypeStruct((t, LANES), F32)
    return pl.pallas_call(
        _rope_table_kernel,
        grid=(t // tr,),
        in_specs=[pl.BlockSpec((tr, 1), lambda i: (i, 0)),
                  pl.BlockSpec((1, LANES), lambda i: (0, 0))],
        out_specs=[pl.BlockSpec((tr, LANES), lambda i: (i, 0))] * 3,
        out_shape=[out] * 3,
        name="rope_tables",
    )(pos, inv_lane)


def _inproj_kernel(x_ref, g_ref, w_ref, cos_ref, sa_ref, sb_ref, z_ref, u_ref, *, tm, tn):
    j = pl.program_id(1)
    rb = 128

    @pl.when(j == 0)
    def _():
        def body(i, carry):
            rows = pl.ds(pl.multiple_of(i * rb, rb), rb)
            xb = x_ref[rows, :]
            ms = jnp.mean(xb * xb, axis=-1, keepdims=True)
            u_ref[rows, :] = ((xb * lax.rsqrt(ms + EPS)) * g_ref[...]).astype(BF16)
            return carry
        lax.fori_loop(0, tm // rb, body, 0)

    z_ref[...] = jnp.dot(u_ref[...], w_ref[...], preferred_element_type=F32)

    n_q = ATT_W // tn
    n_rope = 2 * ATT_W // tn

    @pl.when(j < n_rope)
    def _():
        scale = jnp.where(j < n_q, ATT_HD ** -0.5 * LOG2E, 1.0).astype(F32)

        def body(i, carry):
            rows = pl.ds(pl.multiple_of(i * rb, rb), rb)
            c = cos_ref[rows, :]
            sa = sa_ref[rows, :]
            sb = sb_ref[rows, :]
            for cb in range(tn // LANES):
                cols = slice(cb * LANES, (cb + 1) * LANES)
                zc = z_ref[rows, cols]
                hi = pltpu.roll(zc, LANES - ROT_DIM // 2, 1)
                lo = pltpu.roll(zc, ROT_DIM // 2, 1)
                z_ref[rows, cols] = (zc * c + hi * sa + lo * sb) * scale
            return carry
        lax.fori_loop(0, tm // rb, body, 0)


def _inproj(x2, g_pre, w_in_bf16, cos_t, sa_t, sb_t):
    t = x2.shape[0]
    tm = min(INPROJ_TM, t)
    tn = INPROJ_TN
    row = lambda i, j: (i, 0)
    return pl.pallas_call(
        functools.partial(_inproj_kernel, tm=tm, tn=tn),
        grid=(t // tm, IN_W // tn),
        in_specs=[pl.BlockSpec((tm, D_MODEL), row),
                  pl.BlockSpec((1, D_MODEL), lambda i, j: (0, 0)),
                  pl.BlockSpec((D_MODEL, tn), lambda i, j: (0, j)),
                  pl.BlockSpec((tm, LANES), row),
                  pl.BlockSpec((tm, LANES), row),
                  pl.BlockSpec((tm, LANES), row)],
        out_specs=pl.BlockSpec((tm, tn), lambda i, j: (i, j)),
        out_shape=jax.ShapeDtypeStruct((t, IN_W), F32),
        scratch_shapes=[pltpu.VMEM((tm, D_MODEL), BF16)],
        compiler_params=pltpu.CompilerParams(
            dimension_semantics=("parallel", "arbitrary"), vmem_limit_bytes=VMEM_LIMIT),
        name="inproj",
    )(x2, g_pre.reshape(1, D_MODEL), w_in_bf16, cos_t, sa_t, sb_t)


def _att_kernel(q_ref, k_ref, v_ref, g_ref, y_ref, m_s, l_s, n_s, bias_s, p_s, *, seq):
    n_tiles = seq // ATT_TILE
    lane = lax.broadcasted_iota(jnp.int32, (ATT_QB, LANES), 1)
    head_a = lane < ATT_HD
    qa_off = lax.broadcasted_iota(jnp.int32, (2 * ATT_QB, ATT_KB), 0) & (ATT_QB - 1)
    ka_off = lax.broadcasted_iota(jnp.int32, (2 * ATT_QB, ATT_KB), 1)
    ones_kv = jnp.ones((ATT_KB, LANES), BF16)
    for idx, delta in enumerate((-HALF, 0, -2 * HALF)):
        bias_s[idx] = jnp.where(jnp.abs(ka_off - qa_off + delta) <= HALF, 0.0, NEG)

    def rows_of(start, size, d):
        return pl.ds(start, size) if d == 1 else pl.ds(start, size, stride=d)

    def store_heads(ref, gi, dst, x):
        ref[gi, dst, :] = jnp.where(head_a, x[:ATT_QB], x[ATT_QB:])

    def score_block(bi, gi, d, tile0, r, a0):
        n_cls = seq // d
        ka0 = jnp.clip(a0 - HALF, 0, n_cls - ATT_KB)
        q2 = q_ref[rows_of(r + d * a0, ATT_QB, d), :]
        kk = k_ref[rows_of(r + d * ka0, ATT_KB, d), :].astype(BF16)
        which = jnp.where(a0 == 0, 1, jnp.where(a0 == n_cls - ATT_QB, 2, 0))
        qs = jnp.concatenate([jnp.where(head_a, q2, 0.0), jnp.where(head_a, 0.0, q2)], axis=0)
        s = _bdot_nt(qs, kk) + bias_s[which]
        m = jnp.max(s, axis=-1, keepdims=True)
        p_s[bi] = jnp.exp2(s - m).astype(BF16)
        dst = rows_of(r + d * a0 - tile0, ATT_QB, d)
        store_heads(m_s, gi, dst, jnp.broadcast_to(m, (2 * ATT_QB, LANES)))

    def value_block(bi, gi, d, tile0, r, a0):
        n_cls = seq // d
        ka0 = jnp.clip(a0 - HALF, 0, n_cls - ATT_KB)
        vv = v_ref[rows_of(r + d * ka0, ATT_KB, d), :].astype(BF16)
        nl = jnp.dot(p_s[bi], jnp.concatenate([vv, ones_kv], axis=1), preferred_element_type=F32)
        n = nl[:, :LANES]
        l = nl[:, LANES:]
        dst = rows_of(r + d * a0 - tile0, ATT_QB, d)
        store_heads(l_s, gi, dst, l)
        store_heads(n_s, gi, dst, n)

    def tile_body(tj, carry):
        tile0 = tj * ATT_TILE
        for gi, d in enumerate(DILATIONS):
            per_cls = ATT_TILE // d // ATT_QB
            for block_fn in (score_block, value_block):
                def blk_body(bi, c, gi=gi, d=d, per_cls=per_cls, block_fn=block_fn):
                    r = bi // per_cls
                    a0 = tile0 // d + (bi % per_cls) * ATT_QB
                    block_fn(bi, gi, d, tile0, r, a0)
                    return c
                lax.fori_loop(0, ATT_TILE // ATT_QB, blk_body, 0, unroll=ATT_UNROLL)

        cb = 256

        def comb_body(ci, c):
            rows = pl.ds(pl.multiple_of(ci * cb, cb), cb)
            ms = [m_s[gi, rows, :] for gi in range(len(DILATIONS))]
            mx = functools.reduce(jnp.maximum, ms)
            num = jnp.zeros((cb, LANES), F32)
            den = jnp.zeros((cb, LANES), F32)
            for gi in range(len(DILATIONS)):
                w = jnp.exp2(ms[gi] - mx)
                num = num + w * n_s[gi, rows, :]
                den = den + w * l_s[gi, rows, :]
            orow = pl.ds(pl.multiple_of(tile0 + ci * cb, cb), cb)
            y_ref[orow, :] = ((num / den) * _silu(g_ref[orow, :])).astype(y_ref.dtype)
            return c
        lax.fori_loop(0, ATT_TILE // cb, comb_body, 0)
        return carry

    lax.fori_loop(0, n_tiles, tile_body, 0)


def _attention(z, batch, seq):
    t = batch * seq
    n_pairs = ATT_W // LANES

    def col(c0):
        return pl.BlockSpec((seq, LANES), lambda b, h, c0=c0: (b, c0 + h))

    scratch = pltpu.VMEM((len(DILATIONS), ATT_TILE, LANES), F32)
    return pl.pallas_call(
        functools.partial(_att_kernel, seq=seq),
        grid=(batch, n_pairs),
        in_specs=[col(COL_AQ), col(COL_AK), col(COL_AV), col(COL_AG)],
        out_specs=pl.BlockSpec((seq, LANES), lambda b, h: (b, h)),
        out_shape=jax.ShapeDtypeStruct((t, ATT_W), BF16),
        scratch_shapes=[scratch, scratch, scratch, pltpu.VMEM((3, 2 * ATT_QB, ATT_KB), F32),
                        pltpu.VMEM((ATT_TILE // ATT_QB, 2 * ATT_QB, ATT_KB), BF16)],
        compiler_params=pltpu.CompilerParams(
            dimension_semantics=("parallel", "parallel"), vmem_limit_bytes=VMEM_LIMIT),
        name="band_attention",
    )(z, z, z, z)


def _chunk_cumsum(g, tri):
    c = HG_CHUNK
    g1 = g.astype(BF16)
    r1 = g - g1.astype(F32)
    g2 = r1.astype(BF16)
    g3 = (r1 - g2.astype(F32)).astype(BF16)
    terms = jnp.concatenate([g1, g2, g3], axis=1)
    k = g.shape[1]
    outs = []
    for i in range(g.shape[0] // c):
        part = jnp.dot(tri, terms[i * c:(i + 1) * c, :], preferred_element_type=F32)
        outs.append((part[:, 2 * k:] + part[:, k:2 * k]) + part[:, :k])
    return jnp.concatenate(outs, axis=0)


def _hgrn_kernel(hq_ref, ff_ref, fb_ref, hv_ref, hg_ref, lbf_ref, lbb_ref, ghg_ref, y_ref,
                 q_s, kf_s, kb_s, bf_s, bb_s, o_s,
                 qtf_s, ktf_s, qef_s, kdf_s, qtb_s, ktb_s, qeb_s, kdb_s, v16_s,
                 ebf_s, ebb_s, stf_s, stb_s, *, seq, layer):
    C = HG_CHUNK
    n_chunks = seq // C

    def lower_bound(ref):
        a = ref[...]
        e = jnp.exp(a - jnp.max(a, axis=0, keepdims=True))
        return jnp.sum(e[0:layer + 1, :], axis=0, keepdims=True) / jnp.sum(e, axis=0, keepdims=True)

    lbf = lower_bound(lbf_ref)
    lbb = lower_bound(lbb_ref)

    ti = lax.broadcasted_iota(jnp.int32, (C, C), 0)
    si = lax.broadcasted_iota(jnp.int32, (C, C), 1)
    tri_f = si <= ti
    tri_b = si >= ti

    pr = HG_PRE_ROWS

    cpb = pr // C
    fwd = (ff_ref, lbf, kf_s, bf_s, qtf_s, ktf_s, qef_s, kdf_s, ebf_s, False)
    bwd = (fb_ref, lbb, kb_s, bb_s, qtb_s, ktb_s, qeb_s, kdb_s, ebb_s, True)

    def per_chunk_rows(x3):
        return jnp.broadcast_to(x3, (cpb, C, HG_DK)).reshape(pr, HG_DK)

    def pre_body(i, lo):
        rows = pl.ds(pl.multiple_of(i * pr, pr), pr)
        q = _silu(hq_ref[rows, :])
        q_s[rows, :] = q
        o_s[rows, :] = jnp.zeros((pr, HG_DK), F32)
        v16_s[rows, :] = hv_ref[rows, :].astype(BF16)
        for f_ref, lb, k_s, b_s, qt_s, kt_s, qe_s, kd_s, eb_s, reverse in (fwd, bwd):
            f = lb + (1.0 - lb) * _sigmoid(f_ref[rows, :])
            kk = 1.0 - f
            tri = tri_b if reverse else tri_f
            b = _chunk_cumsum(jnp.log(f), tri.astype(BF16))
            last = 0 if reverse else C - 1
            btot = b.reshape(cpb, C, HG_DK)[:, last:last + 1, :]
            mid = 0.5 * btot
            midr = per_chunk_rows(mid)
            emid = per_chunk_rows(jnp.exp(mid))
            qt = q * jnp.exp(b - midr)
            kt = kk * jnp.exp(midr - b)
            k_s[rows, :] = kk
            b_s[rows, :] = b
            qt_s[rows, :] = qt.astype(BF16)
            kt_s[rows, :] = kt.astype(BF16)
            qe_s[rows, :] = (qt * emid).astype(BF16)
            kd_s[rows, :] = (kt * emid).astype(BF16)
            eb_s[pl.ds(pl.multiple_of(i * cpb, cpb), cpb), :] = jnp.exp(btot).reshape(cpb, HG_DK)
            lo = jnp.minimum(lo, btot.reshape(cpb, HG_DK))
        return lo

    lo = lax.fori_loop(0, seq // pr, pre_body, jnp.zeros((cpb, HG_DK), F32))
    fast = jnp.min(lo) >= HG_FAST_MIN_LOGDECAY

    stf_s[...] = jnp.zeros_like(stf_s)
    stb_s[...] = jnp.zeros_like(stb_s)

    def chunk(ci, tri, d, st_ref, exact):
        _, _, k_s, b_s, qt_s, kt_s, qe_s, kd_s, eb_s, _ = d
        r0 = pl.multiple_of(ci * C, C)
        rows = pl.ds(r0, C)
        if exact:
            q = q_s[rows, :]
            b = b_s[rows, :]

            def col_body(s, att):
                w = q * k_s[pl.ds(r0 + s, 1), :] * jnp.exp(jnp.minimum(b - b_s[pl.ds(r0 + s, 1), :], 0.0))
                return att + jnp.where(si == s, jnp.sum(w, axis=-1, keepdims=True), 0.0)
            att = lax.fori_loop(0, C, col_body, jnp.zeros((C, C), F32))
        else:
            att = _bdot_nt(qt_s[rows, :], kt_s[rows, :])
        att = jnp.where(tri, att, 0.0)
        v = v16_s[rows, :]
        st = st_ref[...]
        o = _bdot(att, v) + _bdot_nt(qe_s[rows, :], st)
        st_ref[...] = st * eb_s[pl.ds(ci, 1), :] + _bdot_tn(v, kd_s[rows, :])
        o_s[rows, :] = o_s[rows, :] + o

    def scan(exact):
        def body(n, carry):
            chunk(n, tri_f, fwd, stf_s, exact)
            chunk(n_chunks - 1 - n, tri_b, bwd, stb_s, exact)
            return carry
        lax.fori_loop(0, n_chunks, body, 0, unroll=1 if exact else HG_UNROLL)

    lax.cond(fast, lambda: scan(False), lambda: scan(True))

    def post_body(i, carry):
        rows = pl.ds(pl.multiple_of(i * pr, pr), pr)
        o = o_s[rows, :]
        ms = jnp.mean(o * o, axis=-1, keepdims=True)
        y = (o * lax.rsqrt(ms + EPS)) * ghg_ref[...]
        y_ref[rows, :] = (y * _silu(hg_ref[rows, :])).astype(y_ref.dtype)
        return carry
    lax.fori_loop(0, seq // pr, post_body, 0)


def _hgrn(z, lb_fwd, lb_bwd, g_hg, batch, seq, layer):
    t = batch * seq

    def col(c0):
        return pl.BlockSpec((seq, LANES), lambda b, h, c0=c0: (b, c0 + h))

    n_lb = lb_fwd.shape[0]
    lb_spec = pl.BlockSpec((n_lb, HG_DK), lambda b, h: (0, h))
    seq_f32 = pltpu.VMEM((seq, HG_DK), F32)
    seq_bf16 = pltpu.VMEM((seq, HG_DK), BF16)
    chunk_f32 = pltpu.VMEM((seq // HG_CHUNK, HG_DK), F32)
    state = pltpu.VMEM((HG_DK, HG_DK), F32)
    return pl.pallas_call(
        functools.partial(_hgrn_kernel, seq=seq, layer=layer),
        grid=(batch, HG_HEADS),
        in_specs=[col(COL_HQ), col(COL_HFF), col(COL_HFB), col(COL_HV), col(COL_HG),
                  lb_spec, lb_spec, pl.BlockSpec((1, HG_DK), lambda b, h: (0, 0))],
        out_specs=pl.BlockSpec((seq, HG_DK), lambda b, h: (b, h)),
        out_shape=jax.ShapeDtypeStruct((t, HG_W), BF16),
        scratch_shapes=[seq_f32] * 6 + [seq_bf16] * 9 + [chunk_f32, chunk_f32, state, state],
        compiler_params=pltpu.CompilerParams(
            dimension_semantics=("parallel", "parallel"), vmem_limit_bytes=VMEM_LIMIT),
        name="hgrn2",
    )(z, z, z, z, z, lb_fwd, lb_bwd, g_hg.reshape(1, HG_DK))


def _rms(y, g):
    ms = jnp.mean(y * y, axis=-1, keepdims=True)
    return (y * lax.rsqrt(ms + EPS)) * g


def _outproj_kernel(x_ref, ya_ref, yh_ref, p_ref, wo_ref, wpg_ref, wpp_ref, gpost_ref, gple_ref, o_ref):
    y = (jnp.dot(ya_ref[...], wo_ref[0:ATT_W, :], preferred_element_type=F32)
         + jnp.dot(yh_ref[...], wo_ref[ATT_W:, :], preferred_element_type=F32))
    h1 = x_ref[...] + _rms(y, gpost_ref[...])
    gate = jax.nn.sigmoid(jnp.dot(h1.astype(BF16), wpg_ref[...], preferred_element_type=F32))
    e = jnp.dot(p_ref[...].astype(BF16), wpp_ref[...], preferred_element_type=F32)
    o_ref[...] = h1 + _rms(gate * e, gple_ref[...])


def _outproj(x2, y_att, y_hg, p2, w_out, w_pg, w_pp, g_post, g_ple):
    t = x2.shape[0]
    tm = min(OUT_TM, t)
    row = lambda i: (i, 0)
    const = lambda i: (0, 0)
    resident = dict(pipeline_mode=pl.Buffered(1))
    return pl.pallas_call(
        _outproj_kernel,
        grid=(t // tm,),
        in_specs=[pl.BlockSpec((tm, D_MODEL), row),
                  pl.BlockSpec((tm, ATT_W), row),
                  pl.BlockSpec((tm, HG_W), row),
                  pl.BlockSpec((tm, PLE_DIM), row),
                  pl.BlockSpec((ATT_W + HG_W, D_MODEL), const, **resident),
                  pl.BlockSpec((D_MODEL, D_MODEL), const, **resident),
                  pl.BlockSpec((PLE_DIM, D_MODEL), const, **resident),
                  pl.BlockSpec((1, D_MODEL), const),
                  pl.BlockSpec((1, D_MODEL), const)],
        out_specs=pl.BlockSpec((tm, D_MODEL), row),
        out_shape=jax.ShapeDtypeStruct((t, D_MODEL), F32),
        compiler_params=pltpu.CompilerParams(
            dimension_semantics=("parallel",), vmem_limit_bytes=VMEM_LIMIT),
        name="outproj_ple",
    )(x2, y_att, y_hg, p2, w_out, w_pg, w_pp, g_post.reshape(1, D_MODEL), g_ple.reshape(1, D_MODEL))


def kernel(x, p, positions, w_in, w_out, g_pre, g_post, g_hg, lb_fwd, lb_bwd, w_pg, w_pp, g_ple):
    depth = w_in.shape[0]
    batch, seq, _ = x.shape
    assert seq % ATT_TILE == 0 and seq // max(DILATIONS) >= ATT_KB
    t = batch * seq
    cos_t, sa_t, sb_t = _rope_tables(positions)
    h = x.reshape(t, D_MODEL)
    for i in range(depth):
        z = _inproj(h, g_pre[i], w_in[i].astype(BF16), cos_t, sa_t, sb_t)
        y_att = _attention(z, batch, seq)
        y_hg = _hgrn(z, lb_fwd, lb_bwd, g_hg[i], batch, seq, i)
        h = _outproj(h, y_att, y_hg, p[i].reshape(t, PLE_DIM), w_out[i].astype(BF16),
                     w_pg[i].astype(BF16), w_pp[i].astype(BF16), g_post[i], g_ple[i])
    return h.reshape(batch, seq, D_MODEL)
```

```python
import functools

import jax
import jax.numpy as jnp
from jax import lax
from jax.experimental import pallas as pl
from jax.experimental.pallas import tpu as pltpu

F32 = jnp.float32
BF16 = jnp.bfloat16

D_MODEL = 2048
PLE_DIM = 256
ATT_HEADS = 16
ATT_HD = 64
ATT_W = ATT_HEADS * ATT_HD
ROT_DIM = ATT_HD // 4
ROPE_THETA = 500000.0
DILATION_CFG = ((128, 1), (512, 4), (2048, 16))
DILATIONS = tuple(d for _, d in DILATION_CFG)
HALF = (DILATION_CFG[0][0] // 2) // DILATION_CFG[0][1]
assert all((w // 2) // d == HALF for w, d in DILATION_CFG)
HG_HEADS = 8
HG_DK = 128
HG_W = HG_HEADS * HG_DK
HG_CHUNK = 128
IN_W = 4 * ATT_W + 5 * HG_W
EPS = 1e-6
NEG = -1e30
LOG2E = 1.4426950408889634

LANES = 128
VMEM_LIMIT = 56 * 1024 * 1024

COL_AQ, COL_AK, COL_AV, COL_AG = (i * ATT_W // LANES for i in range(4))
COL_HQ, COL_HFF, COL_HFB, COL_HV, COL_HG = (4 * ATT_W // LANES + i * HG_W // LANES for i in range(5))

ROPE_TR = 1024
INPROJ_TM = 1024
INPROJ_TN = 1024
ATT_TILE = 1024
ATT_QB = HALF
ATT_KB = 3 * HALF
ATT_UNROLL = 16
HG_UNROLL = 4
HG_PRE_ROWS = 512
OUT_TM = 512
OUT_SUBBLOCKS = 2
HG_FAST_MIN_LOGDECAY = -120.0


def _bdot(a, b):
    return jnp.dot(a.astype(BF16), b.astype(BF16), preferred_element_type=F32)


def _bdot_nt(a, b):
    return lax.dot_general(a.astype(BF16), b.astype(BF16), (((1,), (1,)), ((), ())),
                           preferred_element_type=F32)


def _bdot_tn(a, b):
    return lax.dot_general(a.astype(BF16), b.astype(BF16), (((0,), (0,)), ((), ())),
                           preferred_element_type=F32)


def _sigmoid(x):
    return 0.5 * jnp.tanh(0.5 * x) + 0.5


def _silu(x):
    return x * _sigmoid(x)


def _pair_swap_columns(w):
    h = ROT_DIM // 2
    p = w.reshape(w.shape[:-1] + (w.shape[-1] // LANES, LANES))
    parts = [p[..., 0:h], p[..., ATT_HD:ATT_HD + h], p[..., ROT_DIM:ATT_HD],
             p[..., h:ROT_DIM], p[..., ATT_HD + h:ATT_HD + ROT_DIM], p[..., ATT_HD + ROT_DIM:]]
    return jnp.concatenate(parts, axis=-1).reshape(w.shape)


def _head_a_lanes(lane):
    h = ROT_DIM // 2
    return (lane < h) | ((lane >= ROT_DIM) & (lane < ATT_HD + h))


def _rope_table_kernel(pos_ref, inv_ref, cos_ref, sin_ref):
    ang = pos_ref[...].astype(F32) * inv_ref[...]
    lane = lax.broadcasted_iota(jnp.int32, ang.shape, 1)
    c = jnp.cos(ang)
    s = jnp.sin(ang)
    is_x1 = lane < ROT_DIM
    is_x2 = (lane >= ATT_HD) & (lane < ATT_HD + ROT_DIM)
    cos_ref[...] = jnp.where(is_x1 | is_x2, c, 1.0)
    sin_ref[...] = jnp.where(is_x1, -s, jnp.where(is_x2, s, 0.0))


def _rope_tables(positions):
    t = positions.size
    pos = positions.reshape(t, 1)
    inv = jnp.power(ROPE_THETA, -jnp.arange(0, ROT_DIM, 2, dtype=F32) / ROT_DIM)
    inv_lane = jnp.tile(inv, LANES // inv.shape[0]).reshape(1, LANES)
    tr = min(ROPE_TR, t)
    out = jax.ShapeDtypeStruct((t, LANES), F32)
    return pl.pallas_call(
        _rope_table_kernel,
        grid=(t // tr,),
        in_specs=[pl.BlockSpec((tr, 1), lambda i: (i, 0)),
                  pl.BlockSpec((1, LANES), lambda i: (0, 0))],
        out_specs=[pl.BlockSpec((tr, LANES), lambda i: (i, 0))] * 2,
        out_shape=[out] * 2,
        name="rope_tables",
    )(pos, inv_lane)


def _inproj_kernel(x_ref, g_ref, w_ref, cos_ref, sin_ref, z_ref, u_ref, *, tm, tn):
    j = pl.program_id(1)
    rb = 128

    @pl.when(j == 0)
    def _():
        def body(i, carry):
            rows = pl.ds(pl.multiple_of(i * rb, rb), rb)
            xb = x_ref[rows, :]
            ms = jnp.mean(xb * xb, axis=-1, keepdims=True)
            u_ref[rows, :] = ((xb * lax.rsqrt(ms + EPS)) * g_ref[...]).astype(BF16)
            return carry
        lax.fori_loop(0, tm // rb, body, 0)

    z_ref[...] = jnp.dot(u_ref[...], w_ref[...], preferred_element_type=F32)

    n_q = ATT_W // tn
    n_rope = 2 * ATT_W // tn

    @pl.when(j < n_rope)
    def _():
        scale = jnp.where(j < n_q, ATT_HD ** -0.5 * LOG2E, 1.0).astype(F32)

        def body(i, carry):
            rows = pl.ds(pl.multiple_of(i * rb, rb), rb)
            c = cos_ref[rows, :] * scale
            s = sin_ref[rows, :] * scale
            for cb in range(tn // LANES):
                cols = slice(cb * LANES, (cb + 1) * LANES)
                zc = z_ref[rows, cols]
                z_ref[rows, cols] = zc * c + pltpu.roll(zc, LANES // 2, 1) * s
            return carry
        lax.fori_loop(0, tm // rb, body, 0)


def _inproj(x2, g_pre, w_in_bf16, cos_t, sin_t):
    t = x2.shape[0]
    tm = min(INPROJ_TM, t)
    tn = INPROJ_TN
    row = lambda i, j: (i, 0)
    return pl.pallas_call(
        functools.partial(_inproj_kernel, tm=tm, tn=tn),
        grid=(t // tm, IN_W // tn),
        in_specs=[pl.BlockSpec((tm, D_MODEL), row),
                  pl.BlockSpec((1, D_MODEL), lambda i, j: (0, 0)),
                  pl.BlockSpec((D_MODEL, tn), lambda i, j: (0, j)),
                  pl.BlockSpec((tm, LANES), row),
                  pl.BlockSpec((tm, LANES), row)],
        out_specs=pl.BlockSpec((tm, tn), lambda i, j: (i, j)),
        out_shape=jax.ShapeDtypeStruct((t, IN_W), F32),
        scratch_shapes=[pltpu.VMEM((tm, D_MODEL), BF16)],
        compiler_params=pltpu.CompilerParams(
            dimension_semantics=("parallel", "arbitrary"), vmem_limit_bytes=VMEM_LIMIT),
        name="inproj",
    )(x2, g_pre.reshape(1, D_MODEL), w_in_bf16, cos_t, sin_t)


def _att_kernel(q_ref, k_ref, v_ref, g_ref, y_ref, m_s, l_s, n_s, bias_s, p_s, *, seq):
    n_tiles = seq // ATT_TILE
    lane = lax.broadcasted_iota(jnp.int32, (ATT_QB, LANES), 1)
    head_a = lane < ATT_HD
    qk_head_a = _head_a_lanes(lane)
    qa_off = lax.broadcasted_iota(jnp.int32, (2 * ATT_QB, ATT_KB), 0) & (ATT_QB - 1)
    ka_off = lax.broadcasted_iota(jnp.int32, (2 * ATT_QB, ATT_KB), 1)
    ones_kv = jnp.ones((ATT_KB, LANES), BF16)
    for idx, delta in enumerate((-HALF, 0, -2 * HALF)):
        bias_s[idx] = jnp.where(jnp.abs(ka_off - qa_off + delta) <= HALF, 0.0, NEG)

    def rows_of(start, size, d):
        return pl.ds(start, size) if d == 1 else pl.ds(start, size, stride=d)

    def store_heads(ref, gi, dst, x):
        ref[gi, dst, :] = jnp.where(head_a, x[:ATT_QB], x[ATT_QB:])

    def score_block(bi, gi, d, tile0, r, a0):
        n_cls = seq // d
        ka0 = jnp.clip(a0 - HALF, 0, n_cls - ATT_KB)
        q2 = q_ref[rows_of(r + d * a0, ATT_QB, d), :]
        kk = k_ref[rows_of(r + d * ka0, ATT_KB, d), :].astype(BF16)
        which = jnp.where(a0 == 0, 1, jnp.where(a0 == n_cls - ATT_QB, 2, 0))
        qs = jnp.concatenate([jnp.where(qk_head_a, q2, 0.0), jnp.where(qk_head_a, 0.0, q2)], axis=0)
        s = _bdot_nt(qs, kk) + bias_s[which]
        m = jnp.max(s, axis=-1, keepdims=True)
        p_s[bi] = jnp.exp2(s - m).astype(BF16)
        dst = rows_of(r + d * a0 - tile0, ATT_QB, d)
        store_heads(m_s, gi, dst, jnp.broadcast_to(m, (2 * ATT_QB, LANES)))

    def value_block(bi, gi, d, tile0, r, a0):
        n_cls = seq // d
        ka0 = jnp.clip(a0 - HALF, 0, n_cls - ATT_KB)
        vv = v_ref[rows_of(r + d * ka0, ATT_KB, d), :].astype(BF16)
        nl = jnp.dot(p_s[bi], jnp.concatenate([vv, ones_kv], axis=1), preferred_element_type=F32)
        n = nl[:, :LANES]
        l = nl[:, LANES:]
        dst = rows_of(r + d * a0 - tile0, ATT_QB, d)
        store_heads(l_s, gi, dst, l)
        store_heads(n_s, gi, dst, n)

    def tile_body(tj, carry):
        tile0 = tj * ATT_TILE
        for gi, d in enumerate(DILATIONS):
            per_cls = ATT_TILE // d // ATT_QB
            for block_fn in (score_block, value_block):
                def blk_body(bi, c, gi=gi, d=d, per_cls=per_cls, block_fn=block_fn):
                    r = bi // per_cls
                    a0 = tile0 // d + (bi % per_cls) * ATT_QB
                    block_fn(bi, gi, d, tile0, r, a0)
                    return c
                lax.fori_loop(0, ATT_TILE // ATT_QB, blk_body, 0, unroll=ATT_UNROLL)

        cb = 256

        def comb_body(ci, c):
            rows = pl.ds(pl.multiple_of(ci * cb, cb), cb)
            ms = [m_s[gi, rows, :] for gi in range(len(DILATIONS))]
            mx = functools.reduce(jnp.maximum, ms)
            num = jnp.zeros((cb, LANES), F32)
            den = jnp.zeros((cb, LANES), F32)
            for gi in range(len(DILATIONS)):
                w = jnp.exp2(ms[gi] - mx)
                num = num + w * n_s[gi, rows, :]
                den = den + w * l_s[gi, rows, :]
            orow = pl.ds(pl.multiple_of(tile0 + ci * cb, cb), cb)
            y_ref[orow, :] = ((num / den) * _silu(g_ref[orow, :])).astype(y_ref.dtype)
            return c
        lax.fori_loop(0, ATT_TILE // cb, comb_body, 0)
        return carry

    lax.fori_loop(0, n_tiles, tile_body, 0)


def _attention(z, batch, seq):
    t = batch * seq
    n_pairs = ATT_W // LANES

    def col(c0):
        return pl.BlockSpec((seq, LANES), lambda b, h, c0=c0: (b, c0 + h))

    scratch = pltpu.VMEM((len(DILATIONS), ATT_TILE, LANES), F32)
    return pl.pallas_call(
        functools.partial(_att_kernel, seq=seq),
        grid=(batch, n_pairs),
        in_specs=[col(COL_AQ), col(COL_AK), col(COL_AV), col(COL_AG)],
        out_specs=pl.BlockSpec((seq, LANES), lambda b, h: (b, h)),
        out_shape=jax.ShapeDtypeStruct((t, ATT_W), BF16),
        scratch_shapes=[scratch, scratch, scratch, pltpu.VMEM((3, 2 * ATT_QB, ATT_KB), F32),
                        pltpu.VMEM((ATT_TILE // ATT_QB, 2 * ATT_QB, ATT_KB), BF16)],
        compiler_params=pltpu.CompilerParams(
            dimension_semantics=("parallel", "parallel"), vmem_limit_bytes=VMEM_LIMIT),
        name="band_attention",
    )(z, z, z, z)


def _chunk_cumsum(g, tri):
    c = HG_CHUNK
    g1 = g.astype(BF16)
    r1 = g - g1.astype(F32)
    g2 = r1.astype(BF16)
    g3 = (r1 - g2.astype(F32)).astype(BF16)
    terms = jnp.concatenate([g1, g2, g3], axis=1)
    k = g.shape[1]
    outs = []
    for i in range(g.shape[0] // c):
        part = jnp.dot(tri, terms[i * c:(i + 1) * c, :], preferred_element_type=F32)
        outs.append((part[:, 2 * k:] + part[:, k:2 * k]) + part[:, :k])
    return jnp.concatenate(outs, axis=0)


def _hgrn_kernel(hq_ref, ff_ref, fb_ref, hv_ref, hg_ref, lbf_ref, lbb_ref, ghg_ref, y_ref,
                 q_s, kf_s, kb_s, bf_s, bb_s, o_s,
                 qtf_s, ktf_s, qef_s, kdf_s, qtb_s, ktb_s, qeb_s, kdb_s, v16_s,
                 ebf_s, ebb_s, stf_s, stb_s, *, seq, layer):
    C = HG_CHUNK
    n_chunks = seq // C

    def lower_bound(ref):
        a = ref[...]
        e = jnp.exp(a - jnp.max(a, axis=0, keepdims=True))
        return jnp.sum(e[0:layer + 1, :], axis=0, keepdims=True) / jnp.sum(e, axis=0, keepdims=True)

    lbf = lower_bound(lbf_ref)
    lbb = lower_bound(lbb_ref)

    ti = lax.broadcasted_iota(jnp.int32, (C, C), 0)
    si = lax.broadcasted_iota(jnp.int32, (C, C), 1)
    tri_f = si <= ti
    tri_b = si >= ti

    pr = HG_PRE_ROWS

    cpb = pr // C
    fwd = (ff_ref, lbf, kf_s, bf_s, qtf_s, ktf_s, qef_s, kdf_s, ebf_s, False)
    bwd = (fb_ref, lbb, kb_s, bb_s, qtb_s, ktb_s, qeb_s, kdb_s, ebb_s, True)

    def per_chunk_rows(x3):
        return jnp.broadcast_to(x3, (cpb, C, HG_DK)).reshape(pr, HG_DK)

    def pre_body(i, lo):
        rows = pl.ds(pl.multiple_of(i * pr, pr), pr)
        q = _silu(hq_ref[rows, :])
        q_s[rows, :] = q
        o_s[rows, :] = jnp.zeros((pr, HG_DK), F32)
        v16_s[rows, :] = hv_ref[rows, :].astype(BF16)
        for f_ref, lb, k_s, b_s, qt_s, kt_s, qe_s, kd_s, eb_s, reverse in (fwd, bwd):
            f = lb + (1.0 - lb) * _sigmoid(f_ref[rows, :])
            kk = 1.0 - f
            tri = tri_b if reverse else tri_f
            b = _chunk_cumsum(jnp.log(f), tri.astype(BF16))
            last = 0 if reverse else C - 1
            btot = b.reshape(cpb, C, HG_DK)[:, last:last + 1, :]
            mid = 0.5 * btot
            midr = per_chunk_rows(mid)
            emid = per_chunk_rows(jnp.exp(mid))
            qt = q * jnp.exp(b - midr)
            kt = kk * jnp.exp(midr - b)
            k_s[rows, :] = kk
            b_s[rows, :] = b
            qt_s[rows, :] = qt.astype(BF16)
            kt_s[rows, :] = kt.astype(BF16)
            qe_s[rows, :] = (qt * emid).astype(BF16)
            kd_s[rows, :] = (kt * emid).astype(BF16)
            eb_s[pl.ds(pl.multiple_of(i * cpb, cpb), cpb), :] = jnp.exp(btot).reshape(cpb, HG_DK)
            lo = jnp.minimum(lo, btot.reshape(cpb, HG_DK))
        return lo

    lo = lax.fori_loop(0, seq // pr, pre_body, jnp.zeros((cpb, HG_DK), F32))
    fast = jnp.min(lo) >= HG_FAST_MIN_LOGDECAY

    stf_s[...] = jnp.zeros_like(stf_s)
    stb_s[...] = jnp.zeros_like(stb_s)

    def chunk(ci, tri, d, st_ref, exact):
        _, _, k_s, b_s, qt_s, kt_s, qe_s, kd_s, eb_s, _ = d
        r0 = pl.multiple_of(ci * C, C)
        rows = pl.ds(r0, C)
        if exact:
            q = q_s[rows, :]
            b = b_s[rows, :]

            def col_body(s, att):
                w = q * k_s[pl.ds(r0 + s, 1), :] * jnp.exp(jnp.minimum(b - b_s[pl.ds(r0 + s, 1), :], 0.0))
                return att + jnp.where(si == s, jnp.sum(w, axis=-1, keepdims=True), 0.0)
            att = lax.fori_loop(0, C, col_body, jnp.zeros((C, C), F32))
        else:
            att = _bdot_nt(qt_s[rows, :], kt_s[rows, :])
        att = jnp.where(tri, att, 0.0)
        v = v16_s[rows, :]
        st = st_ref[...]
        o = _bdot(att, v) + _bdot_nt(qe_s[rows, :], st)
        st_ref[...] = st * eb_s[pl.ds(ci, 1), :] + _bdot_tn(v, kd_s[rows, :])
        o_s[rows, :] = o_s[rows, :] + o

    def scan(exact):
        def body(n, carry):
            chunk(n, tri_f, fwd, stf_s, exact)
            chunk(n_chunks - 1 - n, tri_b, bwd, stb_s, exact)
            return carry
        lax.fori_loop(0, n_chunks, body, 0, unroll=1 if exact else HG_UNROLL)

    lax.cond(fast, lambda: scan(False), lambda: scan(True))

    def post_body(i, carry):
        rows = pl.ds(pl.multiple_of(i * pr, pr), pr)
        o = o_s[rows, :]
        ms = jnp.mean(o * o, axis=-1, keepdims=True)
        y = (o * lax.rsqrt(ms + EPS)) * ghg_ref[...]
        y_ref[rows, :] = (y * _silu(hg_ref[rows, :])).astype(y_ref.dtype)
        return carry
    lax.fori_loop(0, seq // pr, post_body, 0)


def _hgrn(z, lb_fwd, lb_bwd, g_hg, batch, seq, layer):
    t = batch * seq

    def col(c0):
        return pl.BlockSpec((seq, LANES), lambda b, h, c0=c0: (b, c0 + h))

    n_lb = lb_fwd.shape[0]
    lb_spec = pl.BlockSpec((n_lb, HG_DK), lambda b, h: (0, h))
    seq_f32 = pltpu.VMEM((seq, HG_DK), F32)
    seq_bf16 = pltpu.VMEM((seq, HG_DK), BF16)
    chunk_f32 = pltpu.VMEM((seq // HG_CHUNK, HG_DK), F32)
    state = pltpu.VMEM((HG_DK, HG_DK), F32)
    return pl.pallas_call(
        functools.partial(_hgrn_kernel, seq=seq, layer=layer),
        grid=(batch, HG_HEADS),
        in_specs=[col(COL_HQ), col(COL_HFF), col(COL_HFB), col(COL_HV), col(COL_HG),
                  lb_spec, lb_spec, pl.BlockSpec((1, HG_DK), lambda b, h: (0, 0))],
        out_specs=pl.BlockSpec((seq, HG_DK), lambda b, h: (b, h)),
        out_shape=jax.ShapeDtypeStruct((t, HG_W), BF16),
        scratch_shapes=[seq_f32] * 6 + [seq_bf16] * 9 + [chunk_f32, chunk_f32, state, state],
        compiler_params=pltpu.CompilerParams(
            dimension_semantics=("parallel", "parallel"), vmem_limit_bytes=VMEM_LIMIT),
        name="hgrn2",
    )(z, z, z, z, z, lb_fwd, lb_bwd, g_hg.reshape(1, HG_DK))


def _rms(y, g):
    ms = jnp.mean(y * y, axis=-1, keepdims=True)
    return (y * lax.rsqrt(ms + EPS)) * g


def _outproj_kernel(x_ref, ya_ref, yh_ref, p_ref, wo_ref, wpg_ref, wpp_ref, gpost_ref, gple_ref, o_ref):
    sub = x_ref.shape[0] // OUT_SUBBLOCKS
    for r in range(OUT_SUBBLOCKS):
        rows = slice(r * sub, (r + 1) * sub)
        y = (jnp.dot(ya_ref[rows, :], wo_ref[0:ATT_W, :], preferred_element_type=F32)
             + jnp.dot(yh_ref[rows, :], wo_ref[ATT_W:, :], preferred_element_type=F32))
        h1 = x_ref[rows, :] + _rms(y, gpost_ref[...])
        gate = jax.nn.sigmoid(jnp.dot(h1.astype(BF16), wpg_ref[...], preferred_element_type=F32))
        e = jnp.dot(p_ref[rows, :].astype(BF16), wpp_ref[...], preferred_element_type=F32)
        o_ref[rows, :] = h1 + _rms(gate * e, gple_ref[...])


def _outproj(x2, y_att, y_hg, p2, w_out, w_pg, w_pp, g_post, g_ple):
    t = x2.shape[0]
    tm = min(OUT_TM, t)
    row = lambda i: (i, 0)
    const = lambda i: (0, 0)
    resident = dict(pipeline_mode=pl.Buffered(1))
    return pl.pallas_call(
        _outproj_kernel,
        grid=(t // tm,),
        in_specs=[pl.BlockSpec((tm, D_MODEL), row),
                  pl.BlockSpec((tm, ATT_W), row),
                  pl.BlockSpec((tm, HG_W), row),
                  pl.BlockSpec((tm, PLE_DIM), row),
                  pl.BlockSpec((ATT_W + HG_W, D_MODEL), const, **resident),
                  pl.BlockSpec((D_MODEL, D_MODEL), const, **resident),
                  pl.BlockSpec((PLE_DIM, D_MODEL), const, **resident),
                  pl.BlockSpec((1, D_MODEL), const),
                  pl.BlockSpec((1, D_MODEL), const)],
        out_specs=pl.BlockSpec((tm, D_MODEL), row),
        out_shape=jax.ShapeDtypeStruct((t, D_MODEL), F32),
        compiler_params=pltpu.CompilerParams(
            dimension_semantics=("parallel",), vmem_limit_bytes=VMEM_LIMIT),
        name="outproj_ple",
    )(x2, y_att, y_hg, p2, w_out, w_pg, w_pp, g_post.reshape(1, D_MODEL), g_ple.reshape(1, D_MODEL))


def kernel(x, p, positions, w_in, w_out, g_pre, g_post, g_hg, lb_fwd, lb_bwd, w_pg, w_pp, g_ple):
    depth = w_in.shape[0]
    batch, seq, _ = x.shape
    assert seq % ATT_TILE == 0 and seq // max(DILATIONS) >= ATT_KB
    t = batch * seq
    cos_t, sin_t = _rope_tables(positions)
    h = x.reshape(t, D_MODEL)
    for i in range(depth):
        w_qk = _pair_swap_columns(w_in[i][:, :2 * ATT_W])
        w_i = jnp.concatenate([w_qk, w_in[i][:, 2 * ATT_W:]], axis=1).astype(BF16)
        z = _inproj(h, g_pre[i], w_i, cos_t, sin_t)
        y_att = _attention(z, batch, seq)
        y_hg = _hgrn(z, lb_fwd, lb_bwd, g_hg[i], batch, seq, i)
        h = _outproj(h, y_att, y_hg, p[i].reshape(t, PLE_DIM), w_out[i].astype(BF16),
                     w_pg[i].astype(BF16), w_pp[i].astype(BF16), g_post[i], g_ple[i])
    return h.reshape(batch, seq, D_MODEL)
```

```python
import functools

import jax
import jax.numpy as jnp
from jax import lax
from jax.experimental import pallas as pl
from jax.experimental.pallas import tpu as pltpu

F32 = jnp.float32
BF16 = jnp.bfloat16

D_MODEL = 2048
PLE_DIM = 256
ATT_HEADS = 16
ATT_HD = 64
ATT_W = ATT_HEADS * ATT_HD
ROT_DIM = ATT_HD // 4
ROPE_THETA = 500000.0
DILATION_CFG = ((128, 1), (512, 4), (2048, 16))
DILATIONS = tuple(d for _, d in DILATION_CFG)
HALF = (DILATION_CFG[0][0] // 2) // DILATION_CFG[0][1]
assert all((w // 2) // d == HALF for w, d in DILATION_CFG)
HG_HEADS = 8
HG_DK = 128
HG_W = HG_HEADS * HG_DK
HG_CHUNK = 128
IN_W = 4 * ATT_W + 5 * HG_W
EPS = 1e-6
NEG = -1e30
LOG2E = 1.4426950408889634

LANES = 128
VMEM_LIMIT = 56 * 1024 * 1024

COL_AQ, COL_AK, COL_AV, COL_AG = (i * ATT_W // LANES for i in range(4))
COL_HQ, COL_HFF, COL_HFB, COL_HV, COL_HG = (4 * ATT_W // LANES + i * HG_W // LANES for i in range(5))

ROPE_TR = 1024
WPREP_TN = 512
INPROJ_TM = 1024
INPROJ_TN = 1024
ATT_TILE = 1024
ATT_QB = HALF
ATT_KB = 3 * HALF
ATT_UNROLL = 16
HG_UNROLL = 4
HG_PRE_ROWS = 512
OUT_TM = 512
OUT_SUBBLOCKS = 2
HG_FAST_MIN_LOGDECAY = -120.0


def _bdot(a, b):
    return jnp.dot(a.astype(BF16), b.astype(BF16), preferred_element_type=F32)


def _bdot_nt(a, b):
    return lax.dot_general(a.astype(BF16), b.astype(BF16), (((1,), (1,)), ((), ())),
                           preferred_element_type=F32)


def _bdot_tn(a, b):
    return lax.dot_general(a.astype(BF16), b.astype(BF16), (((0,), (0,)), ((), ())),
                           preferred_element_type=F32)


def _sigmoid(x):
    return 0.5 * jnp.tanh(0.5 * x) + 0.5


def _silu(x):
    return x * _sigmoid(x)


def _pair_swap_matrix():
    h = ROT_DIM // 2
    src = lax.broadcasted_iota(jnp.int32, (LANES, LANES), 0)
    dst = lax.broadcasted_iota(jnp.int32, (LANES, LANES), 1)
    shift = ATT_HD - h
    sigma = jnp.where((dst >= h) & (dst < ROT_DIM), dst + shift,
                      jnp.where((dst >= ATT_HD) & (dst < ATT_HD + h), dst - shift, dst))
    return (src == sigma).astype(BF16)


def _wprep_kernel(w_ref, o_ref, *, swap_tiles, rb):
    j = pl.program_id(0)
    n_rows, tn = w_ref.shape

    @pl.when(j < swap_tiles)
    def _():
        perm = _pair_swap_matrix()

        def body(i, carry):
            rows = pl.ds(pl.multiple_of(i * rb, rb), rb)
            for cb in range(tn // LANES):
                cols = slice(cb * LANES, (cb + 1) * LANES)
                o_ref[rows, cols] = jnp.dot(w_ref[rows, cols].astype(BF16), perm,
                                            preferred_element_type=F32).astype(BF16)
            return carry
        lax.fori_loop(0, n_rows // rb, body, 0)

    @pl.when(j >= swap_tiles)
    def _():
        def body(i, carry):
            rows = pl.ds(pl.multiple_of(i * rb, rb), rb)
            o_ref[rows, :] = w_ref[rows, :].astype(BF16)
            return carry
        lax.fori_loop(0, n_rows // rb, body, 0)


def _prep_w_in(w):
    k, n = w.shape
    tn = WPREP_TN
    return pl.pallas_call(
        functools.partial(_wprep_kernel, swap_tiles=2 * ATT_W // tn, rb=256),
        grid=(n // tn,),
        in_specs=[pl.BlockSpec((k, tn), lambda j: (0, j))],
        out_specs=pl.BlockSpec((k, tn), lambda j: (0, j)),
        out_shape=jax.ShapeDtypeStruct((k, n), BF16),
        compiler_params=pltpu.CompilerParams(
            dimension_semantics=("parallel",), vmem_limit_bytes=VMEM_LIMIT),
        name="w_in_prep",
    )(w)


def _head_a_lanes(lane):
    h = ROT_DIM // 2
    return (lane < h) | ((lane >= ROT_DIM) & (lane < ATT_HD + h))


def _rope_table_kernel(pos_ref, inv_ref, cos_ref, sin_ref):
    ang = pos_ref[...].astype(F32) * inv_ref[...]
    lane = lax.broadcasted_iota(jnp.int32, ang.shape, 1)
    c = jnp.cos(ang)
    s = jnp.sin(ang)
    is_x1 = lane < ROT_DIM
    is_x2 = (lane >= ATT_HD) & (lane < ATT_HD + ROT_DIM)
    cos_ref[...] = jnp.where(is_x1 | is_x2, c, 1.0)
    sin_ref[...] = jnp.where(is_x1, -s, jnp.where(is_x2, s, 0.0))


def _rope_tables(positions):
    t = positions.size
    pos = positions.reshape(t, 1)
    inv = jnp.power(ROPE_THETA, -jnp.arange(0, ROT_DIM, 2, dtype=F32) / ROT_DIM)
    inv_lane = jnp.tile(inv, LANES // inv.shape[0]).reshape(1, LANES)
    tr = min(ROPE_TR, t)
    out = jax.ShapeDtypeStruct((t, LANES), F32)
    return pl.pallas_call(
        _rope_table_kernel,
        grid=(t // tr,),
        in_specs=[pl.BlockSpec((tr, 1), lambda i: (i, 0)),
                  pl.BlockSpec((1, LANES), lambda i: (0, 0))],
        out_specs=[pl.BlockSpec((tr, LANES), lambda i: (i, 0))] * 2,
        out_shape=[out] * 2,
        name="rope_tables",
    )(pos, inv_lane)


def _inproj_kernel(x_ref, g_ref, w_ref, cos_ref, sin_ref, z_ref, u_ref, *, tm, tn):
    j = pl.program_id(1)
    rb = 128

    @pl.when(j == 0)
    def _():
        def body(i, carry):
            rows = pl.ds(pl.multiple_of(i * rb, rb), rb)
            xb = x_ref[rows, :]
            ms = jnp.mean(xb * xb, axis=-1, keepdims=True)
            u_ref[rows, :] = ((xb * lax.rsqrt(ms + EPS)) * g_ref[...]).astype(BF16)
            return carry
        lax.fori_loop(0, tm // rb, body, 0)

    z_ref[...] = jnp.dot(u_ref[...], w_ref[...], preferred_element_type=F32)

    n_q = ATT_W // tn
    n_rope = 2 * ATT_W // tn

    @pl.when(j < n_rope)
    def _():
        scale = jnp.where(j < n_q, ATT_HD ** -0.5 * LOG2E, 1.0).astype(F32)

        def body(i, carry):
            rows = pl.ds(pl.multiple_of(i * rb, rb), rb)
            c = cos_ref[rows, :] * scale
            s = sin_ref[rows, :] * scale
            for cb in range(tn // LANES):
                cols = slice(cb * LANES, (cb + 1) * LANES)
                zc = z_ref[rows, cols]
                z_ref[rows, cols] = zc * c + pltpu.roll(zc, LANES // 2, 1) * s
            return carry
        lax.fori_loop(0, tm // rb, body, 0)


def _inproj(x2, g_pre, w_in_bf16, cos_t, sin_t):
    t = x2.shape[0]
    tm = min(INPROJ_TM, t)
    tn = INPROJ_TN
    row = lambda i, j: (i, 0)
    return pl.pallas_call(
        functools.partial(_inproj_kernel, tm=tm, tn=tn),
        grid=(t // tm, IN_W // tn),
        in_specs=[pl.BlockSpec((tm, D_MODEL), row),
                  pl.BlockSpec((1, D_MODEL), lambda i, j: (0, 0)),
                  pl.BlockSpec((D_MODEL, tn), lambda i, j: (0, j)),
                  pl.BlockSpec((tm, LANES), row),
                  pl.BlockSpec((tm, LANES), row)],
        out_specs=pl.BlockSpec((tm, tn), lambda i, j: (i, j)),
        out_shape=jax.ShapeDtypeStruct((t, IN_W), F32),
        scratch_shapes=[pltpu.VMEM((tm, D_MODEL), BF16)],
        compiler_params=pltpu.CompilerParams(
            dimension_semantics=("parallel", "arbitrary"), vmem_limit_bytes=VMEM_LIMIT),
        name="inproj",
    )(x2, g_pre.reshape(1, D_MODEL), w_in_bf16, cos_t, sin_t)


def _att_kernel(q_ref, k_ref, v_ref, g_ref, y_ref, m_s, l_s, n_s, bias_s, p_s, *, seq):
    n_tiles = seq // ATT_TILE
    lane = lax.broadcasted_iota(jnp.int32, (ATT_QB, LANES), 1)
    head_a = lane < ATT_HD
    qk_head_a = _head_a_lanes(lane)
    qa_off = lax.broadcasted_iota(jnp.int32, (2 * ATT_QB, ATT_KB), 0) & (ATT_QB - 1)
    ka_off = lax.broadcasted_iota(jnp.int32, (2 * ATT_QB, ATT_KB), 1)
    ones_kv = jnp.ones((ATT_KB, LANES), BF16)
    for idx, delta in enumerate((-HALF, 0, -2 * HALF)):
        bias_s[idx] = jnp.where(jnp.abs(ka_off - qa_off + delta) <= HALF, 0.0, NEG)

    def rows_of(start, size, d):
        return pl.ds(start, size) if d == 1 else pl.ds(start, size, stride=d)

    def store_heads(ref, gi, dst, x):
        ref[gi, dst, :] = jnp.where(head_a, x[:ATT_QB], x[ATT_QB:])

    def score_block(bi, gi, d, tile0, r, a0):
        n_cls = seq // d
        ka0 = jnp.clip(a0 - HALF, 0, n_cls - ATT_KB)
        q2 = q_ref[rows_of(r + d * a0, ATT_QB, d), :]
        kk = k_ref[rows_of(r + d * ka0, ATT_KB, d), :].astype(BF16)
        which = jnp.where(a0 == 0, 1, jnp.where(a0 == n_cls - ATT_QB, 2, 0))
        qs = jnp.concatenate([jnp.where(qk_head_a, q2, 0.0), jnp.where(qk_head_a, 0.0, q2)], axis=0)
        s = _bdot_nt(qs, kk) + bias_s[which]
        m = jnp.max(s, axis=-1, keepdims=True)
        p_s[bi] = jnp.exp2(s - m).astype(BF16)
        dst = rows_of(r + d * a0 - tile0, ATT_QB, d)
        store_heads(m_s, gi, dst, jnp.broadcast_to(m, (2 * ATT_QB, LANES)))

    def value_block(bi, gi, d, tile0, r, a0):
        n_cls = seq // d
        ka0 = jnp.clip(a0 - HALF, 0, n_cls - ATT_KB)
        vv = v_ref[rows_of(r + d * ka0, ATT_KB, d), :].astype(BF16)
        nl = jnp.dot(p_s[bi], jnp.concatenate([vv, ones_kv], axis=1), preferred_element_type=F32)
        n = nl[:, :LANES]
        l = nl[:, LANES:]
        dst = rows_of(r + d * a0 - tile0, ATT_QB, d)
        store_heads(l_s, gi, dst, l)
        store_heads(n_s, gi, dst, n)

    def tile_body(tj, carry):
        tile0 = tj * ATT_TILE
        for gi, d in enumerate(DILATIONS):
            per_cls = ATT_TILE // d // ATT_QB
            for block_fn in (score_block, value_block):
                def blk_body(bi, c, gi=gi, d=d, per_cls=per_cls, block_fn=block_fn):
                    r = bi // per_cls
                    a0 = tile0 // d + (bi % per_cls) * ATT_QB
                    block_fn(bi, gi, d, tile0, r, a0)
                    return c
                lax.fori_loop(0, ATT_TILE // ATT_QB, blk_body, 0, unroll=ATT_UNROLL)

        cb = 256

        def comb_body(ci, c):
            rows = pl.ds(pl.multiple_of(ci * cb, cb), cb)
            ms = [m_s[gi, rows, :] for gi in range(len(DILATIONS))]
            mx = functools.reduce(jnp.maximum, ms)
            num = jnp.zeros((cb, LANES), F32)
            den = jnp.zeros((cb, LANES), F32)
            for gi in range(len(DILATIONS)):
                w = jnp.exp2(ms[gi] - mx)
                num = num + w * n_s[gi, rows, :]
                den = den + w * l_s[gi, rows, :]
            orow = pl.ds(pl.multiple_of(tile0 + ci * cb, cb), cb)
            y_ref[orow, :] = ((num / den) * _silu(g_ref[orow, :])).astype(y_ref.dtype)
            return c
        lax.fori_loop(0, ATT_TILE // cb, comb_body, 0)
        return carry

    lax.fori_loop(0, n_tiles, tile_body, 0)


def _attention(z, batch, seq):
    t = batch * seq
    n_pairs = ATT_W // LANES

    def col(c0):
        return pl.BlockSpec((seq, LANES), lambda b, h, c0=c0: (b, c0 + h))

    scratch = pltpu.VMEM((len(DILATIONS), ATT_TILE, LANES), F32)
    return pl.pallas_call(
        functools.partial(_att_kernel, seq=seq),
        grid=(batch, n_pairs),
        in_specs=[col(COL_AQ), col(COL_AK), col(COL_AV), col(COL_AG)],
        out_specs=pl.BlockSpec((seq, LANES), lambda b, h: (b, h)),
        out_shape=jax.ShapeDtypeStruct((t, ATT_W), BF16),
        scratch_shapes=[scratch, scratch, scratch, pltpu.VMEM((3, 2 * ATT_QB, ATT_KB), F32),
                        pltpu.VMEM((ATT_TILE // ATT_QB, 2 * ATT_QB, ATT_KB), BF16)],
        compiler_params=pltpu.CompilerParams(
            dimension_semantics=("parallel", "parallel"), vmem_limit_bytes=VMEM_LIMIT),
        name="band_attention",
    )(z, z, z, z)


def _chunk_cumsum(g, tri):
    c = HG_CHUNK
    g1 = g.astype(BF16)
    r1 = g - g1.astype(F32)
    g2 = r1.astype(BF16)
    g3 = (r1 - g2.astype(F32)).astype(BF16)
    terms = jnp.concatenate([g1, g2, g3], axis=1)
    k = g.shape[1]
    outs = []
    for i in range(g.shape[0] // c):
        part = jnp.dot(tri, terms[i * c:(i + 1) * c, :], preferred_element_type=F32)
        outs.append((part[:, 2 * k:] + part[:, k:2 * k]) + part[:, :k])
    return jnp.concatenate(outs, axis=0)


def _hgrn_kernel(hq_ref, ff_ref, fb_ref, hv_ref, hg_ref, lbf_ref, lbb_ref, ghg_ref, y_ref,
                 q_s, kf_s, kb_s, bf_s, bb_s, o_s,
                 qtf_s, ktf_s, qef_s, kdf_s, qtb_s, ktb_s, qeb_s, kdb_s, v16_s,
                 ebf_s, ebb_s, stf_s, stb_s, *, seq, layer):
    C = HG_CHUNK
    n_chunks = seq // C

    def lower_bound(ref):
        a = ref[...]
        e = jnp.exp(a - jnp.max(a, axis=0, keepdims=True))
        return jnp.sum(e[0:layer + 1, :], axis=0, keepdims=True) / jnp.sum(e, axis=0, keepdims=True)

    lbf = lower_bound(lbf_ref)
    lbb = lower_bound(lbb_ref)

    ti = lax.broadcasted_iota(jnp.int32, (C, C), 0)
    si = lax.broadcasted_iota(jnp.int32, (C, C), 1)
    tri_f = si <= ti
    tri_b = si >= ti

    pr = HG_PRE_ROWS

    cpb = pr // C
    fwd = (ff_ref, lbf, kf_s, bf_s, qtf_s, ktf_s, qef_s, kdf_s, ebf_s, False)
    bwd = (fb_ref, lbb, kb_s, bb_s, qtb_s, ktb_s, qeb_s, kdb_s, ebb_s, True)

    def per_chunk_rows(x3):
        return jnp.broadcast_to(x3, (cpb, C, HG_DK)).reshape(pr, HG_DK)

    def pre_body(i, lo):
        rows = pl.ds(pl.multiple_of(i * pr, pr), pr)
        q = _silu(hq_ref[rows, :])
        q_s[rows, :] = q
        o_s[rows, :] = jnp.zeros((pr, HG_DK), F32)
        v16_s[rows, :] = hv_ref[rows, :].astype(BF16)
        for f_ref, lb, k_s, b_s, qt_s, kt_s, qe_s, kd_s, eb_s, reverse in (fwd, bwd):
            f = lb + (1.0 - lb) * _sigmoid(f_ref[rows, :])
            kk = 1.0 - f
            tri = tri_b if reverse else tri_f
            b = _chunk_cumsum(jnp.log(f), tri.astype(BF16))
            last = 0 if reverse else C - 1
            btot = b.reshape(cpb, C, HG_DK)[:, last:last + 1, :]
            mid = 0.5 * btot
            midr = per_chunk_rows(mid)
            emid = per_chunk_rows(jnp.exp(mid))
            qt = q * jnp.exp(b - midr)
            kt = kk * jnp.exp(midr - b)
            k_s[rows, :] = kk
            b_s[rows, :] = b
            qt_s[rows, :] = qt.astype(BF16)
            kt_s[rows, :] = kt.astype(BF16)
            qe_s[rows, :] = (qt * emid).astype(BF16)
            kd_s[rows, :] = (kt * emid).astype(BF16)
            eb_s[pl.ds(pl.multiple_of(i * cpb, cpb), cpb), :] = jnp.exp(btot).reshape(cpb, HG_DK)
            lo = jnp.minimum(lo, btot.reshape(cpb, HG_DK))
        return lo

    lo = lax.fori_loop(0, seq // pr, pre_body, jnp.zeros((cpb, HG_DK), F32))
    fast = jnp.min(lo) >= HG_FAST_MIN_LOGDECAY

    stf_s[...] = jnp.zeros_like(stf_s)
    stb_s[...] = jnp.zeros_like(stb_s)

    def chunk(ci, tri, d, st_ref, exact):
        _, _, k_s, b_s, qt_s, kt_s, qe_s, kd_s, eb_s, _ = d
        r0 = pl.multiple_of(ci * C, C)
        rows = pl.ds(r0, C)
        if exact:
            q = q_s[rows, :]
            b = b_s[rows, :]

            def col_body(s, att):
                w = q * k_s[pl.ds(r0 + s, 1), :] * jnp.exp(jnp.minimum(b - b_s[pl.ds(r0 + s, 1), :], 0.0))
                return att + jnp.where(si == s, jnp.sum(w, axis=-1, keepdims=True), 0.0)
            att = lax.fori_loop(0, C, col_body, jnp.zeros((C, C), F32))
        else:
            att = _bdot_nt(qt_s[rows, :], kt_s[rows, :])
        att = jnp.where(tri, att, 0.0)
        v = v16_s[rows, :]
        st = st_ref[...]
        o = _bdot(att, v) + _bdot_nt(qe_s[rows, :], st)
        st_ref[...] = st * eb_s[pl.ds(ci, 1), :] + _bdot_tn(v, kd_s[rows, :])
        o_s[rows, :] = o_s[rows, :] + o

    def scan(exact):
        def body(n, carry):
            chunk(n, tri_f, fwd, stf_s, exact)
            chunk(n_chunks - 1 - n, tri_b, bwd, stb_s, exact)
            return carry
        lax.fori_loop(0, n_chunks, body, 0, unroll=1 if exact else HG_UNROLL)

    lax.cond(fast, lambda: scan(False), lambda: scan(True))

    def post_body(i, carry):
        rows = pl.ds(pl.multiple_of(i * pr, pr), pr)
        o = o_s[rows, :]
        ms = jnp.mean(o * o, axis=-1, keepdims=True)
        y = (o * lax.rsqrt(ms + EPS)) * ghg_ref[...]
        y_ref[rows, :] = (y * _silu(hg_ref[rows, :])).astype(y_ref.dtype)
        return carry
    lax.fori_loop(0, seq // pr, post_body, 0)


def _hgrn(z, lb_fwd, lb_bwd, g_hg, batch, seq, layer):
    t = batch * seq

    def col(c0):
        return pl.BlockSpec((seq, LANES), lambda b, h, c0=c0: (b, c0 + h))

    n_lb = lb_fwd.shape[0]
    lb_spec = pl.BlockSpec((n_lb, HG_DK), lambda b, h: (0, h))
    seq_f32 = pltpu.VMEM((seq, HG_DK), F32)
    seq_bf16 = pltpu.VMEM((seq, HG_DK), BF16)
    chunk_f32 = pltpu.VMEM((seq // HG_CHUNK, HG_DK), F32)
    state = pltpu.VMEM((HG_DK, HG_DK), F32)
    return pl.pallas_call(
        functools.partial(_hgrn_kernel, seq=seq, layer=layer),
        grid=(batch, HG_HEADS),
        in_specs=[col(COL_HQ), col(COL_HFF), col(COL_HFB), col(COL_HV), col(COL_HG),
                  lb_spec, lb_spec, pl.BlockSpec((1, HG_DK), lambda b, h: (0, 0))],
        out_specs=pl.BlockSpec((seq, HG_DK), lambda b, h: (b, h)),
        out_shape=jax.ShapeDtypeStruct((t, HG_W), BF16),
        scratch_shapes=[seq_f32] * 6 + [seq_bf16] * 9 + [chunk_f32, chunk_f32, state, state],
        compiler_params=pltpu.CompilerParams(
            dimension_semantics=("parallel", "parallel"), vmem_limit_bytes=VMEM_LIMIT),
        name="hgrn2",
    )(z, z, z, z, z, lb_fwd, lb_bwd, g_hg.reshape(1, HG_DK))


def _rms(y, g):
    ms = jnp.mean(y * y, axis=-1, keepdims=True)
    return (y * lax.rsqrt(ms + EPS)) * g


def _outproj_kernel(x_ref, ya_ref, yh_ref, p_ref, wo_ref, wpg_ref, wpp_ref, gpost_ref, gple_ref, o_ref):
    sub = x_ref.shape[0] // OUT_SUBBLOCKS
    for r in range(OUT_SUBBLOCKS):
        rows = slice(r * sub, (r + 1) * sub)
        y = (jnp.dot(ya_ref[rows, :], wo_ref[0:ATT_W, :], preferred_element_type=F32)
             + jnp.dot(yh_ref[rows, :], wo_ref[ATT_W:, :], preferred_element_type=F32))
        h1 = x_ref[rows, :] + _rms(y, gpost_ref[...])
        gate = jax.nn.sigmoid(jnp.dot(h1.astype(BF16), wpg_ref[...], preferred_element_type=F32))
        e = jnp.dot(p_ref[rows, :].astype(BF16), wpp_ref[...], preferred_element_type=F32)
        o_ref[rows, :] = h1 + _rms(gate * e, gple_ref[...])


def _outproj(x2, y_att, y_hg, p2, w_out, w_pg, w_pp, g_post, g_ple):
    t = x2.shape[0]
    tm = min(OUT_TM, t)
    row = lambda i: (i, 0)
    const = lambda i: (0, 0)
    resident = dict(pipeline_mode=pl.Buffered(1))
    return pl.pallas_call(
        _outproj_kernel,
        grid=(t // tm,),
        in_specs=[pl.BlockSpec((tm, D_MODEL), row),
                  pl.BlockSpec((tm, ATT_W), row),
                  pl.BlockSpec((tm, HG_W), row),
                  pl.BlockSpec((tm, PLE_DIM), row),
                  pl.BlockSpec((ATT_W + HG_W, D_MODEL), const, **resident),
                  pl.BlockSpec((D_MODEL, D_MODEL), const, **resident),
                  pl.BlockSpec((PLE_DIM, D_MODEL), const, **resident),
                  pl.BlockSpec((1, D_MODEL), const),
                  pl.BlockSpec((1, D_MODEL), const)],
        out_specs=pl.BlockSpec((tm, D_MODEL), row),
        out_shape=jax.ShapeDtypeStruct((t, D_MODEL), F32),
        compiler_params=pltpu.CompilerParams(
            dimension_semantics=("parallel",), vmem_limit_bytes=VMEM_LIMIT),
        name="outproj_ple",
    )(x2, y_att, y_hg, p2, w_out, w_pg, w_pp, g_post.reshape(1, D_MODEL), g_ple.reshape(1, D_MODEL))


def kernel(x, p, positions, w_in, w_out, g_pre, g_post, g_hg, lb_fwd, lb_bwd, w_pg, w_pp, g_ple):
    depth = w_in.shape[0]
    batch, seq, _ = x.shape
    assert seq % ATT_TILE == 0 and seq // max(DILATIONS) >= ATT_KB
    t = batch * seq
    cos_t, sin_t = _rope_tables(positions)
    h = x.reshape(t, D_MODEL)
    for i in range(depth):
        z = _inproj(h, g_pre[i], _prep_w_in(w_in[i]), cos_t, sin_t)
        y_att = _attention(z, batch, seq)
        y_hg = _hgrn(z, lb_fwd, lb_bwd, g_hg[i], batch, seq, i)
        h = _outproj(h, y_att, y_hg, p[i].reshape(t, PLE_DIM), w_out[i].astype(BF16),
                     w_pg[i].astype(BF16), w_pp[i].astype(BF16), g_post[i], g_ple[i])
    return h.reshape(batch, seq, D_MODEL)
```

```python
import functools

import jax
import jax.numpy as jnp
from jax import lax
from jax.experimental import pallas as pl
from jax.experimental.pallas import tpu as pltpu

F32 = jnp.float32
BF16 = jnp.bfloat16

D_MODEL = 2048
PLE_DIM = 256
ATT_HEADS = 16
ATT_HD = 64
ATT_W = ATT_HEADS * ATT_HD
ROT_DIM = ATT_HD // 4
ROPE_THETA = 500000.0
DILATION_CFG = ((128, 1), (512, 4), (2048, 16))
DILATIONS = tuple(d for _, d in DILATION_CFG)
HALF = (DILATION_CFG[0][0] // 2) // DILATION_CFG[0][1]
assert all((w // 2) // d == HALF for w, d in DILATION_CFG)
HG_HEADS = 8
HG_DK = 128
HG_W = HG_HEADS * HG_DK
HG_CHUNK = 128
IN_W = 4 * ATT_W + 5 * HG_W
EPS = 1e-6
NEG = -1e30
LOG2E = 1.4426950408889634

LANES = 128
VMEM_LIMIT = 56 * 1024 * 1024

COL_AQ, COL_AK, COL_AV, COL_AG = (i * ATT_W // LANES for i in range(4))
COL_HQ, COL_HFF, COL_HFB, COL_HV, COL_HG = (4 * ATT_W // LANES + i * HG_W // LANES for i in range(5))

ROPE_TR = 1024
WPREP_TN = 512
INPROJ_TM = 1024
INPROJ_TN = 1024
INPROJ_ROPE_CHUNK = 256
ATT_TILE = 1024
ATT_QB = HALF
ATT_KB = 3 * HALF
ATT_UNROLL = 16
HG_UNROLL = 4
HG_PRE_ROWS = 512
OUT_TM = 512
OUT_SUBBLOCKS = 2
HG_FAST_MIN_LOGDECAY = -120.0


def _bdot(a, b):
    return jnp.dot(a.astype(BF16), b.astype(BF16), preferred_element_type=F32)


def _bdot_nt(a, b):
    return lax.dot_general(a.astype(BF16), b.astype(BF16), (((1,), (1,)), ((), ())),
                           preferred_element_type=F32)


def _bdot_tn(a, b):
    return lax.dot_general(a.astype(BF16), b.astype(BF16), (((0,), (0,)), ((), ())),
                           preferred_element_type=F32)


def _sigmoid(x):
    return 0.5 * jnp.tanh(0.5 * x) + 0.5


def _silu(x):
    return x * _sigmoid(x)


def _pair_swap_matrix():
    h = ROT_DIM // 2
    src = lax.broadcasted_iota(jnp.int32, (LANES, LANES), 0)
    dst = lax.broadcasted_iota(jnp.int32, (LANES, LANES), 1)
    shift = ATT_HD - h
    sigma = jnp.where((dst >= h) & (dst < ROT_DIM), dst + shift,
                      jnp.where((dst >= ATT_HD) & (dst < ATT_HD + h), dst - shift, dst))
    return (src == sigma).astype(BF16)


def _wprep_kernel(w_ref, o_ref, *, swap_tiles, rb):
    j = pl.program_id(0)
    n_rows, tn = w_ref.shape

    @pl.when(j < swap_tiles)
    def _():
        perm = _pair_swap_matrix()

        def body(i, carry):
            rows = pl.ds(pl.multiple_of(i * rb, rb), rb)
            for cb in range(tn // LANES):
                cols = slice(cb * LANES, (cb + 1) * LANES)
                o_ref[rows, cols] = jnp.dot(w_ref[rows, cols].astype(BF16), perm,
                                            preferred_element_type=F32).astype(BF16)
            return carry
        lax.fori_loop(0, n_rows // rb, body, 0)

    @pl.when(j >= swap_tiles)
    def _():
        def body(i, carry):
            rows = pl.ds(pl.multiple_of(i * rb, rb), rb)
            o_ref[rows, :] = w_ref[rows, :].astype(BF16)
            return carry
        lax.fori_loop(0, n_rows // rb, body, 0)


def _prep_w_in(w):
    k, n = w.shape
    tn = WPREP_TN
    return pl.pallas_call(
        functools.partial(_wprep_kernel, swap_tiles=2 * ATT_W // tn, rb=256),
        grid=(n // tn,),
        in_specs=[pl.BlockSpec((k, tn), lambda j: (0, j))],
        out_specs=pl.BlockSpec((k, tn), lambda j: (0, j)),
        out_shape=jax.ShapeDtypeStruct((k, n), BF16),
        compiler_params=pltpu.CompilerParams(
            dimension_semantics=("parallel",), vmem_limit_bytes=VMEM_LIMIT),
        name="w_in_prep",
    )(w)


def _head_a_lanes(lane):
    h = ROT_DIM // 2
    return (lane < h) | ((lane >= ROT_DIM) & (lane < ATT_HD + h))


def _rope_table_kernel(pos_ref, inv_ref, cos_ref, sin_ref):
    ang = pos_ref[...].astype(F32) * inv_ref[...]
    lane = lax.broadcasted_iota(jnp.int32, ang.shape, 1)
    c = jnp.cos(ang)
    s = jnp.sin(ang)
    is_x1 = lane < ROT_DIM
    is_x2 = (lane >= ATT_HD) & (lane < ATT_HD + ROT_DIM)
    cos_ref[...] = jnp.where(is_x1 | is_x2, c, 1.0)
    sin_ref[...] = jnp.where(is_x1, -s, jnp.where(is_x2, s, 0.0))


def _rope_tables(positions):
    t = positions.size
    pos = positions.reshape(t, 1)
    inv = jnp.power(ROPE_THETA, -jnp.arange(0, ROT_DIM, 2, dtype=F32) / ROT_DIM)
    inv_lane = jnp.tile(inv, LANES // inv.shape[0]).reshape(1, LANES)
    tr = min(ROPE_TR, t)
    out = jax.ShapeDtypeStruct((t, LANES), F32)
    return pl.pallas_call(
        _rope_table_kernel,
        grid=(t // tr,),
        in_specs=[pl.BlockSpec((tr, 1), lambda i: (i, 0)),
                  pl.BlockSpec((1, LANES), lambda i: (0, 0))],
        out_specs=[pl.BlockSpec((tr, LANES), lambda i: (i, 0))] * 2,
        out_shape=[out] * 2,
        name="rope_tables",
    )(pos, inv_lane)


def _inproj_kernel(x_ref, g_ref, w_ref, cos_ref, sin_ref, z_ref, u_ref, *, tm, tn):
    j = pl.program_id(1)
    rb = 128
    n_q = ATT_W // tn
    n_rope = 2 * ATT_W // tn

    def norm_rows(r0, n):
        for i in range(n // rb):
            rows = slice(r0 + i * rb, r0 + (i + 1) * rb)
            xb = x_ref[rows, :]
            ms = jnp.mean(xb * xb, axis=-1, keepdims=True)
            u_ref[rows, :] = ((xb * lax.rsqrt(ms + EPS)) * g_ref[...]).astype(BF16)

    def project_rotate(r0, n):
        rows = slice(r0, r0 + n)
        scale = jnp.where(j < n_q, ATT_HD ** -0.5 * LOG2E, 1.0).astype(F32)
        c = cos_ref[rows, :] * scale
        s = sin_ref[rows, :] * scale
        for cb in range(tn // INPROJ_ROPE_CHUNK):
            cols = slice(cb * INPROJ_ROPE_CHUNK, (cb + 1) * INPROJ_ROPE_CHUNK)
            zc = jnp.dot(u_ref[rows, :], w_ref[:, cols], preferred_element_type=F32)
            parts = [zc[:, b * LANES:(b + 1) * LANES] for b in range(INPROJ_ROPE_CHUNK // LANES)]
            z_ref[rows, cols] = jnp.concatenate(
                [blk * c + pltpu.roll(blk, LANES // 2, 1) * s for blk in parts], axis=1)

    @pl.when(j == 0)
    def _():
        for r0 in range(0, tm, tm // 2):
            norm_rows(r0, tm // 2)
            project_rotate(r0, tm // 2)

    @pl.when((j > 0) & (j < n_rope))
    def _():
        project_rotate(0, tm)

    @pl.when(j >= n_rope)
    def _():
        z_ref[...] = jnp.dot(u_ref[...], w_ref[...], preferred_element_type=F32)


def _inproj(x2, g_pre, w_in_bf16, cos_t, sin_t):
    t = x2.shape[0]
    tm = min(INPROJ_TM, t)
    tn = INPROJ_TN
    row = lambda i, j: (i, 0)
    return pl.pallas_call(
        functools.partial(_inproj_kernel, tm=tm, tn=tn),
        grid=(t // tm, IN_W // tn),
        in_specs=[pl.BlockSpec((tm, D_MODEL), row),
                  pl.BlockSpec((1, D_MODEL), lambda i, j: (0, 0)),
                  pl.BlockSpec((D_MODEL, tn), lambda i, j: (0, j)),
                  pl.BlockSpec((tm, LANES), row),
                  pl.BlockSpec((tm, LANES), row)],
        out_specs=pl.BlockSpec((tm, tn), lambda i, j: (i, j)),
        out_shape=jax.ShapeDtypeStruct((t, IN_W), F32),
        scratch_shapes=[pltpu.VMEM((tm, D_MODEL), BF16)],
        compiler_params=pltpu.CompilerParams(
            dimension_semantics=("parallel", "arbitrary"), vmem_limit_bytes=VMEM_LIMIT),
        name="inproj",
    )(x2, g_pre.reshape(1, D_MODEL), w_in_bf16, cos_t, sin_t)


def _att_kernel(q_ref, k_ref, v_ref, g_ref, y_ref, m_s, l_s, n_s, bias_s, p_s, *, seq):
    n_tiles = seq // ATT_TILE
    lane = lax.broadcasted_iota(jnp.int32, (ATT_QB, LANES), 1)
    head_a = lane < ATT_HD
    qk_head_a = _head_a_lanes(lane)
    qa_off = lax.broadcasted_iota(jnp.int32, (2 * ATT_QB, ATT_KB), 0) & (ATT_QB - 1)
    ka_off = lax.broadcasted_iota(jnp.int32, (2 * ATT_QB, ATT_KB), 1)
    ones_kv = jnp.ones((ATT_KB, LANES), BF16)
    for idx, delta in enumerate((-HALF, 0, -2 * HALF)):
        bias_s[idx] = jnp.where(jnp.abs(ka_off - qa_off + delta) <= HALF, 0.0, NEG)

    def rows_of(start, size, d):
        return pl.ds(start, size) if d == 1 else pl.ds(start, size, stride=d)

    def store_heads(ref, gi, dst, x):
        ref[gi, dst, :] = jnp.where(head_a, x[:ATT_QB], x[ATT_QB:])

    def score_block(bi, gi, d, tile0, r, a0):
        n_cls = seq // d
        ka0 = jnp.clip(a0 - HALF, 0, n_cls - ATT_KB)
        q2 = q_ref[rows_of(r + d * a0, ATT_QB, d), :]
        kk = k_ref[rows_of(r + d * ka0, ATT_KB, d), :].astype(BF16)
        which = jnp.where(a0 == 0, 1, jnp.where(a0 == n_cls - ATT_QB, 2, 0))
        qs = jnp.concatenate([jnp.where(qk_head_a, q2, 0.0), jnp.where(qk_head_a, 0.0, q2)], axis=0)
        s = _bdot_nt(qs, kk) + bias_s[which]
        m = jnp.max(s, axis=-1, keepdims=True)
        p_s[bi] = jnp.exp2(s - m).astype(BF16)
        dst = rows_of(r + d * a0 - tile0, ATT_QB, d)
        store_heads(m_s, gi, dst, jnp.broadcast_to(m, (2 * ATT_QB, LANES)))

    def value_block(bi, gi, d, tile0, r, a0):
        n_cls = seq // d
        ka0 = jnp.clip(a0 - HALF, 0, n_cls - ATT_KB)
        vv = v_ref[rows_of(r + d * ka0, ATT_KB, d), :].astype(BF16)
        nl = jnp.dot(p_s[bi], jnp.concatenate([vv, ones_kv], axis=1), preferred_element_type=F32)
        n = nl[:, :LANES]
        l = nl[:, LANES:]
        dst = rows_of(r + d * a0 - tile0, ATT_QB, d)
        store_heads(l_s, gi, dst, l)
        store_heads(n_s, gi, dst, n)

    def tile_body(tj, carry):
        tile0 = tj * ATT_TILE
        for gi, d in enumerate(DILATIONS):
            per_cls = ATT_TILE // d // ATT_QB
            for block_fn in (score_block, value_block):
                def blk_body(bi, c, gi=gi, d=d, per_cls=per_cls, block_fn=block_fn):
                    r = bi // per_cls
                    a0 = tile0 // d + (bi % per_cls) * ATT_QB
                    block_fn(bi, gi, d, tile0, r, a0)
                    return c
                lax.fori_loop(0, ATT_TILE // ATT_QB, blk_body, 0, unroll=ATT_UNROLL)

        cb = 256

        def comb_body(ci, c):
            rows = pl.ds(pl.multiple_of(ci * cb, cb), cb)
            ms = [m_s[gi, rows, :] for gi in range(len(DILATIONS))]
            mx = functools.reduce(jnp.maximum, ms)
            num = jnp.zeros((cb, LANES), F32)
            den = jnp.zeros((cb, LANES), F32)
            for gi in range(len(DILATIONS)):
                w = jnp.exp2(ms[gi] - mx)
                num = num + w * n_s[gi, rows, :]
                den = den + w * l_s[gi, rows, :]
            orow = pl.ds(pl.multiple_of(tile0 + ci * cb, cb), cb)
            y_ref[orow, :] = ((num / den) * _silu(g_ref[orow, :])).astype(y_ref.dtype)
            return c
        lax.fori_loop(0, ATT_TILE // cb, comb_body, 0)
        return carry

    lax.fori_loop(0, n_tiles, tile_body, 0)


def _attention(z, batch, seq):
    t = batch * seq
    n_pairs = ATT_W // LANES

    def col(c0):
        return pl.BlockSpec((seq, LANES), lambda b, h, c0=c0: (b, c0 + h))

    scratch = pltpu.VMEM((len(DILATIONS), ATT_TILE, LANES), F32)
    return pl.pallas_call(
        functools.partial(_att_kernel, seq=seq),
        grid=(batch, n_pairs),
        in_specs=[col(COL_AQ), col(COL_AK), col(COL_AV), col(COL_AG)],
        out_specs=pl.BlockSpec((seq, LANES), lambda b, h: (b, h)),
        out_shape=jax.ShapeDtypeStruct((t, ATT_W), BF16),
        scratch_shapes=[scratch, scratch, scratch, pltpu.VMEM((3, 2 * ATT_QB, ATT_KB), F32),
                        pltpu.VMEM((ATT_TILE // ATT_QB, 2 * ATT_QB, ATT_KB), BF16)],
        compiler_params=pltpu.CompilerParams(
            dimension_semantics=("parallel", "parallel"), vmem_limit_bytes=VMEM_LIMIT),
        name="band_attention",
    )(z, z, z, z)


def _chunk_cumsum(g, tri):
    c = HG_CHUNK
    g1 = g.astype(BF16)
    r1 = g - g1.astype(F32)
    g2 = r1.astype(BF16)
    g3 = (r1 - g2.astype(F32)).astype(BF16)
    terms = jnp.concatenate([g1, g2, g3], axis=1)
    k = g.shape[1]
    outs = []
    for i in range(g.shape[0] // c):
        part = jnp.dot(tri, terms[i * c:(i + 1) * c, :], preferred_element_type=F32)
        outs.append((part[:, 2 * k:] + part[:, k:2 * k]) + part[:, :k])
    return jnp.concatenate(outs, axis=0)


def _hgrn_kernel(hq_ref, ff_ref, fb_ref, hv_ref, hg_ref, lbf_ref, lbb_ref, ghg_ref, y_ref,
                 q_s, kf_s, kb_s, bf_s, bb_s, o_s,
                 qtf_s, ktf_s, qef_s, kdf_s, qtb_s, ktb_s, qeb_s, kdb_s, v16_s,
                 ebf_s, ebb_s, stf_s, stb_s, *, seq, layer):
    C = HG_CHUNK
    n_chunks = seq // C

    def lower_bound(ref):
        a = ref[...]
        e = jnp.exp(a - jnp.max(a, axis=0, keepdims=True))
        return jnp.sum(e[0:layer + 1, :], axis=0, keepdims=True) / jnp.sum(e, axis=0, keepdims=True)

    lbf = lower_bound(lbf_ref)
    lbb = lower_bound(lbb_ref)

    ti = lax.broadcasted_iota(jnp.int32, (C, C), 0)
    si = lax.broadcasted_iota(jnp.int32, (C, C), 1)
    tri_f = si <= ti
    tri_b = si >= ti

    pr = HG_PRE_ROWS

    cpb = pr // C
    fwd = (ff_ref, lbf, kf_s, bf_s, qtf_s, ktf_s, qef_s, kdf_s, ebf_s, False)
    bwd = (fb_ref, lbb, kb_s, bb_s, qtb_s, ktb_s, qeb_s, kdb_s, ebb_s, True)

    def per_chunk_rows(x3):
        return jnp.broadcast_to(x3, (cpb, C, HG_DK)).reshape(pr, HG_DK)

    def pre_body(i, lo):
        rows = pl.ds(pl.multiple_of(i * pr, pr), pr)
        q = _silu(hq_ref[rows, :])
        q_s[rows, :] = q
        o_s[rows, :] = jnp.zeros((pr, HG_DK), F32)
        v16_s[rows, :] = hv_ref[rows, :].astype(BF16)
        for f_ref, lb, k_s, b_s, qt_s, kt_s, qe_s, kd_s, eb_s, reverse in (fwd, bwd):
            f = lb + (1.0 - lb) * _sigmoid(f_ref[rows, :])
            kk = 1.0 - f
            tri = tri_b if reverse else tri_f
            b = _chunk_cumsum(jnp.log(f), tri.astype(BF16))
            last = 0 if reverse else C - 1
            btot = b.reshape(cpb, C, HG_DK)[:, last:last + 1, :]
            mid = 0.5 * btot
            midr = per_chunk_rows(mid)
            emid = per_chunk_rows(jnp.exp(mid))
            qt = q * jnp.exp(b - midr)
            kt = kk * jnp.exp(midr - b)
            k_s[rows, :] = kk
            b_s[rows, :] = b
            qt_s[rows, :] = qt.astype(BF16)
            kt_s[rows, :] = kt.astype(BF16)
            qe_s[rows, :] = (qt * emid).astype(BF16)
            kd_s[rows, :] = (kt * emid).astype(BF16)
            eb_s[pl.ds(pl.multiple_of(i * cpb, cpb), cpb), :] = jnp.exp(btot).reshape(cpb, HG_DK)
            lo = jnp.minimum(lo, btot.reshape(cpb, HG_DK))
        return lo

    lo = lax.fori_loop(0, seq // pr, pre_body, jnp.zeros((cpb, HG_DK), F32))
    fast = jnp.min(lo) >= HG_FAST_MIN_LOGDECAY

    stf_s[...] = jnp.zeros_like(stf_s)
    stb_s[...] = jnp.zeros_like(stb_s)

    def chunk(ci, tri, d, st_ref, exact):
        _, _, k_s, b_s, qt_s, kt_s, qe_s, kd_s, eb_s, _ = d
        r0 = pl.multiple_of(ci * C, C)
        rows = pl.ds(r0, C)
        if exact:
            q = q_s[rows, :]
            b = b_s[rows, :]

            def col_body(s, att):
                w = q * k_s[pl.ds(r0 + s, 1), :] * jnp.exp(jnp.minimum(b - b_s[pl.ds(r0 + s, 1), :], 0.0))
                return att + jnp.where(si == s, jnp.sum(w, axis=-1, keepdims=True), 0.0)
            att = lax.fori_loop(0, C, col_body, jnp.zeros((C, C), F32))
        else:
            att = _bdot_nt(qt_s[rows, :], kt_s[rows, :])
        att = jnp.where(tri, att, 0.0)
        v = v16_s[rows, :]
        st = st_ref[...]
        o = _bdot(att, v) + _bdot_nt(qe_s[rows, :], st)
        st_ref[...] = st * eb_s[pl.ds(ci, 1), :] + _bdot_tn(v, kd_s[rows, :])
        o_s[rows, :] = o_s[rows, :] + o

    def scan(exact):
        def body(n, carry):
            chunk(n, tri_f, fwd, stf_s, exact)
            chunk(n_chunks - 1 - n, tri_b, bwd, stb_s, exact)
            return carry
        lax.fori_loop(0, n_chunks, body, 0, unroll=1 if exact else HG_UNROLL)

    lax.cond(fast, lambda: scan(False), lambda: scan(True))

    def post_body(i, carry):
        rows = pl.ds(pl.multiple_of(i * pr, pr), pr)
        o = o_s[rows, :]
        ms = jnp.mean(o * o, axis=-1, keepdims=True)
        y = (o * lax.rsqrt(ms + EPS)) * ghg_ref[...]
        y_ref[rows, :] = (y * _silu(hg_ref[rows, :])).astype(y_ref.dtype)
        return carry
    lax.fori_loop(0, seq // pr, post_body, 0)


def _hgrn(z, lb_fwd, lb_bwd, g_hg, batch, seq, layer):
    t = batch * seq

    def col(c0):
        return pl.BlockSpec((seq, LANES), lambda b, h, c0=c0: (b, c0 + h))

    n_lb = lb_fwd.shape[0]
    lb_spec = pl.BlockSpec((n_lb, HG_DK), lambda b, h: (0, h))
    seq_f32 = pltpu.VMEM((seq, HG_DK), F32)
    seq_bf16 = pltpu.VMEM((seq, HG_DK), BF16)
    chunk_f32 = pltpu.VMEM((seq // HG_CHUNK, HG_DK), F32)
    state = pltpu.VMEM((HG_DK, HG_DK), F32)
    return pl.pallas_call(
        functools.partial(_hgrn_kernel, seq=seq, layer=layer),
        grid=(batch, HG_HEADS),
        in_specs=[col(COL_HQ), col(COL_HFF), col(COL_HFB), col(COL_HV), col(COL_HG),
                  lb_spec, lb_spec, pl.BlockSpec((1, HG_DK), lambda b, h: (0, 0))],
        out_specs=pl.BlockSpec((seq, HG_DK), lambda b, h: (b, h)),
        out_shape=jax.ShapeDtypeStruct((t, HG_W), BF16),
        scratch_shapes=[seq_f32] * 6 + [seq_bf16] * 9 + [chunk_f32, chunk_f32, state, state],
        compiler_params=pltpu.CompilerParams(
            dimension_semantics=("parallel", "parallel"), vmem_limit_bytes=VMEM_LIMIT),
        name="hgrn2",
    )(z, z, z, z, z, lb_fwd, lb_bwd, g_hg.reshape(1, HG_DK))


def _rms(y, g):
    ms = jnp.mean(y * y, axis=-1, keepdims=True)
    return (y * lax.rsqrt(ms + EPS)) * g


def _outproj_kernel(x_ref, ya_ref, yh_ref, p_ref, wo_ref, wpg_ref, wpp_ref, gpost_ref, gple_ref, o_ref):
    sub = x_ref.shape[0] // OUT_SUBBLOCKS
    for r in range(OUT_SUBBLOCKS):
        rows = slice(r * sub, (r + 1) * sub)
        y = (jnp.dot(ya_ref[rows, :], wo_ref[0:ATT_W, :], preferred_element_type=F32)
             + jnp.dot(yh_ref[rows, :], wo_ref[ATT_W:, :], preferred_element_type=F32))
        h1 = x_ref[rows, :] + _rms(y, gpost_ref[...])
        gate = jax.nn.sigmoid(jnp.dot(h1.astype(BF16), wpg_ref[...], preferred_element_type=F32))
        e = jnp.dot(p_ref[rows, :].astype(BF16), wpp_ref[...], preferred_element_type=F32)
        o_ref[rows, :] = h1 + _rms(gate * e, gple_ref[...])


def _outproj(x2, y_att, y_hg, p2, w_out, w_pg, w_pp, g_post, g_ple):
    t = x2.shape[0]
    tm = min(OUT_TM, t)
    row = lambda i: (i, 0)
    const = lambda i: (0, 0)
    resident = dict(pipeline_mode=pl.Buffered(1))
    return pl.pallas_call(
        _outproj_kernel,
        grid=(t // tm,),
        in_specs=[pl.BlockSpec((tm, D_MODEL), row),
                  pl.BlockSpec((tm, ATT_W), row),
                  pl.BlockSpec((tm, HG_W), row),
                  pl.BlockSpec((tm, PLE_DIM), row),
                  pl.BlockSpec((ATT_W + HG_W, D_MODEL), const, **resident),
                  pl.BlockSpec((D_MODEL, D_MODEL), const, **resident),
                  pl.BlockSpec((PLE_DIM, D_MODEL), const, **resident),
                  pl.BlockSpec((1, D_MODEL), const),
                  pl.BlockSpec((1, D_MODEL), const)],
        out_specs=pl.BlockSpec((tm, D_MODEL), row),
        out_shape=jax.ShapeDtypeStruct((t, D_MODEL), F32),
        compiler_params=pltpu.CompilerParams(
            dimension_semantics=("parallel",), vmem_limit_bytes=VMEM_LIMIT),
        name="outproj_ple",
    )(x2, y_att, y_hg, p2, w_out, w_pg, w_pp, g_post.reshape(1, D_MODEL), g_ple.reshape(1, D_MODEL))


def kernel(x, p, positions, w_in, w_out, g_pre, g_post, g_hg, lb_fwd, lb_bwd, w_pg, w_pp, g_ple):
    depth = w_in.shape[0]
    batch, seq, _ = x.shape
    assert seq % ATT_TILE == 0 and seq // max(DILATIONS) >= ATT_KB
    t = batch * seq
    cos_t, sin_t = _rope_tables(positions)
    h = x.reshape(t, D_MODEL)
    for i in range(depth):
        z = _inproj(h, g_pre[i], _prep_w_in(w_in[i]), cos_t, sin_t)
        y_att = _attention(z, batch, seq)
        y_hg = _hgrn(z, lb_fwd, lb_bwd, g_hg[i], batch, seq, i)
        h = _outproj(h, y_att, y_hg, p[i].reshape(t, PLE_DIM), w_out[i].astype(BF16),
                     w_pg[i].astype(BF16), w_pp[i].astype(BF16), g_post[i], g_ple[i])
    return h.reshape(batch, seq, D_MODEL)
```

```python
import functools

import jax
import jax.numpy as jnp
from jax import lax
from jax.experimental import pallas as pl
from jax.experimental.pallas import tpu as pltpu

F32 = jnp.float32
BF16 = jnp.bfloat16

D_MODEL = 2048
PLE_DIM = 256
ATT_HEADS = 16
ATT_HD = 64
ATT_W = ATT_HEADS * ATT_HD
ROT_DIM = ATT_HD // 4
ROPE_THETA = 500000.0
DILATION_CFG = ((128, 1), (512, 4), (2048, 16))
DILATIONS = tuple(d for _, d in DILATION_CFG)
HALF = (DILATION_CFG[0][0] // 2) // DILATION_CFG[0][1]
assert all((w // 2) // d == HALF for w, d in DILATION_CFG)
HG_HEADS = 8
HG_DK = 128
HG_W = HG_HEADS * HG_DK
HG_CHUNK = 128
IN_W = 4 * ATT_W + 5 * HG_W
EPS = 1e-6
NEG = -1e30
LOG2E = 1.4426950408889634

LANES = 128
VMEM_LIMIT = 56 * 1024 * 1024

COL_AQ, COL_AK, COL_AV, COL_AG = (i * ATT_W // LANES for i in range(4))
COL_HQ, COL_HFF, COL_HFB, COL_HV, COL_HG = (4 * ATT_W // LANES + i * HG_W // LANES for i in range(5))

ROPE_TR = 1024
WPREP_TN = 512
INPROJ_TM = 1024
INPROJ_TN = 1024
INPROJ_ROPE_CHUNK = 256
ATT_TILE = 1024
ATT_QB = HALF
ATT_KB = 3 * HALF
ATT_UNROLL = 16
HG_UNROLL = 16
HG_PRE_ROWS = 512
OUT_TM = 512
OUT_SUBBLOCKS = 2
HG_FAST_MIN_LOGDECAY = -120.0


def _bdot(a, b):
    return jnp.dot(a.astype(BF16), b.astype(BF16), preferred_element_type=F32)


def _bdot_nt(a, b):
    return lax.dot_general(a.astype(BF16), b.astype(BF16), (((1,), (1,)), ((), ())),
                           preferred_element_type=F32)


def _bdot_tn(a, b):
    return lax.dot_general(a.astype(BF16), b.astype(BF16), (((0,), (0,)), ((), ())),
                           preferred_element_type=F32)


def _sigmoid(x):
    return 0.5 * jnp.tanh(0.5 * x) + 0.5


def _silu(x):
    return x * _sigmoid(x)


def _pair_swap_matrix():
    h = ROT_DIM // 2
    src = lax.broadcasted_iota(jnp.int32, (LANES, LANES), 0)
    dst = lax.broadcasted_iota(jnp.int32, (LANES, LANES), 1)
    shift = ATT_HD - h
    sigma = jnp.where((dst >= h) & (dst < ROT_DIM), dst + shift,
                      jnp.where((dst >= ATT_HD) & (dst < ATT_HD + h), dst - shift, dst))
    return (src == sigma).astype(BF16)


def _wprep_kernel(w_ref, o_ref, *, swap_tiles, rb):
    j = pl.program_id(0)
    n_rows, tn = w_ref.shape

    @pl.when(j < swap_tiles)
    def _():
        perm = _pair_swap_matrix()

        def body(i, carry):
            rows = pl.ds(pl.multiple_of(i * rb, rb), rb)
            for cb in range(tn // LANES):
                cols = slice(cb * LANES, (cb + 1) * LANES)
                o_ref[rows, cols] = jnp.dot(w_ref[rows, cols].astype(BF16), perm,
                                            preferred_element_type=F32).astype(BF16)
            return carry
        lax.fori_loop(0, n_rows // rb, body, 0)

    @pl.when(j >= swap_tiles)
    def _():
        def body(i, carry):
            rows = pl.ds(pl.multiple_of(i * rb, rb), rb)
            o_ref[rows, :] = w_ref[rows, :].astype(BF16)
            return carry
        lax.fori_loop(0, n_rows // rb, body, 0)


def _prep_w_in(w):
    k, n = w.shape
    tn = WPREP_TN
    return pl.pallas_call(
        functools.partial(_wprep_kernel, swap_tiles=2 * ATT_W // tn, rb=256),
        grid=(n // tn,),
        in_specs=[pl.BlockSpec((k, tn), lambda j: (0, j))],
        out_specs=pl.BlockSpec((k, tn), lambda j: (0, j)),
        out_shape=jax.ShapeDtypeStruct((k, n), BF16),
        compiler_params=pltpu.CompilerParams(
            dimension_semantics=("parallel",), vmem_limit_bytes=VMEM_LIMIT),
        name="w_in_prep",
    )(w)


def _head_a_lanes(lane):
    h = ROT_DIM // 2
    return (lane < h) | ((lane >= ROT_DIM) & (lane < ATT_HD + h))


def _rope_table_kernel(pos_ref, inv_ref, cos_ref, sin_ref):
    ang = pos_ref[...].astype(F32) * inv_ref[...]
    lane = lax.broadcasted_iota(jnp.int32, ang.shape, 1)
    c = jnp.cos(ang)
    s = jnp.sin(ang)
    is_x1 = lane < ROT_DIM
    is_x2 = (lane >= ATT_HD) & (lane < ATT_HD + ROT_DIM)
    cos_ref[...] = jnp.where(is_x1 | is_x2, c, 1.0)
    sin_ref[...] = jnp.where(is_x1, -s, jnp.where(is_x2, s, 0.0))


def _rope_tables(positions):
    t = positions.size
    pos = positions.reshape(t, 1)
    inv = jnp.power(ROPE_THETA, -jnp.arange(0, ROT_DIM, 2, dtype=F32) / ROT_DIM)
    inv_lane = jnp.tile(inv, LANES // inv.shape[0]).reshape(1, LANES)
    tr = min(ROPE_TR, t)
    out = jax.ShapeDtypeStruct((t, LANES), F32)
    return pl.pallas_call(
        _rope_table_kernel,
        grid=(t // tr,),
        in_specs=[pl.BlockSpec((tr, 1), lambda i: (i, 0)),
                  pl.BlockSpec((1, LANES), lambda i: (0, 0))],
        out_specs=[pl.BlockSpec((tr, LANES), lambda i: (i, 0))] * 2,
        out_shape=[out] * 2,
        name="rope_tables",
    )(pos, inv_lane)


def _inproj_kernel(x_ref, g_ref, w_ref, cos_ref, sin_ref, z_ref, u_ref, *, tm, tn):
    j = pl.program_id(1)
    rb = 128
    n_q = ATT_W // tn
    n_rope = 2 * ATT_W // tn

    def norm_rows(r0, n):
        for i in range(n // rb):
            rows = slice(r0 + i * rb, r0 + (i + 1) * rb)
            xb = x_ref[rows, :]
            ms = jnp.mean(xb * xb, axis=-1, keepdims=True)
            u_ref[rows, :] = ((xb * lax.rsqrt(ms + EPS)) * g_ref[...]).astype(BF16)

    def project_rotate(r0, n):
        rows = slice(r0, r0 + n)
        scale = jnp.where(j < n_q, ATT_HD ** -0.5 * LOG2E, 1.0).astype(F32)
        c = cos_ref[rows, :] * scale
        s = sin_ref[rows, :] * scale
        for cb in range(tn // INPROJ_ROPE_CHUNK):
            cols = slice(cb * INPROJ_ROPE_CHUNK, (cb + 1) * INPROJ_ROPE_CHUNK)
            zc = jnp.dot(u_ref[rows, :], w_ref[:, cols], preferred_element_type=F32)
            parts = [zc[:, b * LANES:(b + 1) * LANES] for b in range(INPROJ_ROPE_CHUNK // LANES)]
            z_ref[rows, cols] = jnp.concatenate(
                [blk * c + pltpu.roll(blk, LANES // 2, 1) * s for blk in parts], axis=1)

    @pl.when(j == 0)
    def _():
        for r0 in range(0, tm, tm // 2):
            norm_rows(r0, tm // 2)
            project_rotate(r0, tm // 2)

    @pl.when((j > 0) & (j < n_rope))
    def _():
        project_rotate(0, tm)

    @pl.when(j >= n_rope)
    def _():
        z_ref[...] = jnp.dot(u_ref[...], w_ref[...], preferred_element_type=F32)


def _inproj(x2, g_pre, w_in_bf16, cos_t, sin_t):
    t = x2.shape[0]
    tm = min(INPROJ_TM, t)
    tn = INPROJ_TN
    row = lambda i, j: (i, 0)
    return pl.pallas_call(
        functools.partial(_inproj_kernel, tm=tm, tn=tn),
        grid=(t // tm, IN_W // tn),
        in_specs=[pl.BlockSpec((tm, D_MODEL), row),
                  pl.BlockSpec((1, D_MODEL), lambda i, j: (0, 0)),
                  pl.BlockSpec((D_MODEL, tn), lambda i, j: (0, j)),
                  pl.BlockSpec((tm, LANES), row),
                  pl.BlockSpec((tm, LANES), row)],
        out_specs=pl.BlockSpec((tm, tn), lambda i, j: (i, j)),
        out_shape=jax.ShapeDtypeStruct((t, IN_W), F32),
        scratch_shapes=[pltpu.VMEM((tm, D_MODEL), BF16)],
        compiler_params=pltpu.CompilerParams(
            dimension_semantics=("parallel", "arbitrary"), vmem_limit_bytes=VMEM_LIMIT),
        name="inproj",
    )(x2, g_pre.reshape(1, D_MODEL), w_in_bf16, cos_t, sin_t)


def _att_kernel(q_ref, k_ref, v_ref, g_ref, y_ref, m_s, l_s, n_s, bias_s, p_s, *, seq):
    n_tiles = seq // ATT_TILE
    lane = lax.broadcasted_iota(jnp.int32, (ATT_QB, LANES), 1)
    head_a = lane < ATT_HD
    qk_head_a = _head_a_lanes(lane)
    qa_off = lax.broadcasted_iota(jnp.int32, (2 * ATT_QB, ATT_KB), 0) & (ATT_QB - 1)
    ka_off = lax.broadcasted_iota(jnp.int32, (2 * ATT_QB, ATT_KB), 1)
    ones_kv = jnp.ones((ATT_KB, LANES), BF16)
    for idx, delta in enumerate((-HALF, 0, -2 * HALF)):
        bias_s[idx] = jnp.where(jnp.abs(ka_off - qa_off + delta) <= HALF, 0.0, NEG)

    def rows_of(start, size, d):
        return pl.ds(start, size) if d == 1 else pl.ds(start, size, stride=d)

    def store_heads(ref, gi, dst, x):
        ref[gi, dst, :] = jnp.where(head_a, x[:ATT_QB], x[ATT_QB:])

    def score_block(bi, gi, d, tile0, r, a0):
        n_cls = seq // d
        ka0 = jnp.clip(a0 - HALF, 0, n_cls - ATT_KB)
        q2 = q_ref[rows_of(r + d * a0, ATT_QB, d), :]
        kk = k_ref[rows_of(r + d * ka0, ATT_KB, d), :].astype(BF16)
        which = jnp.where(a0 == 0, 1, jnp.where(a0 == n_cls - ATT_QB, 2, 0))
        qs = jnp.concatenate([jnp.where(qk_head_a, q2, 0.0), jnp.where(qk_head_a, 0.0, q2)], axis=0)
        s = _bdot_nt(qs, kk) + bias_s[which]
        m = jnp.max(s, axis=-1, keepdims=True)
        p_s[bi] = jnp.exp2(s - m).astype(BF16)
        dst = rows_of(r + d * a0 - tile0, ATT_QB, d)
        store_heads(m_s, gi, dst, jnp.broadcast_to(m, (2 * ATT_QB, LANES)))

    def value_block(bi, gi, d, tile0, r, a0):
        n_cls = seq // d
        ka0 = jnp.clip(a0 - HALF, 0, n_cls - ATT_KB)
        vv = v_ref[rows_of(r + d * ka0, ATT_KB, d), :].astype(BF16)
        nl = jnp.dot(p_s[bi], jnp.concatenate([vv, ones_kv], axis=1), preferred_element_type=F32)
        n = nl[:, :LANES]
        l = nl[:, LANES:]
        dst = rows_of(r + d * a0 - tile0, ATT_QB, d)
        store_heads(l_s, gi, dst, l)
        store_heads(n_s, gi, dst, n)

    def tile_body(tj, carry):
        tile0 = tj * ATT_TILE
        for gi, d in enumerate(DILATIONS):
            per_cls = ATT_TILE // d // ATT_QB
            for block_fn in (score_block, value_block):
                def blk_body(bi, c, gi=gi, d=d, per_cls=per_cls, block_fn=block_fn):
                    r = bi // per_cls
                    a0 = tile0 // d + (bi % per_cls) * ATT_QB
                    block_fn(bi, gi, d, tile0, r, a0)
                    return c
                lax.fori_loop(0, ATT_TILE // ATT_QB, blk_body, 0, unroll=ATT_UNROLL)

        cb = 256

        def comb_body(ci, c):
            rows = pl.ds(pl.multiple_of(ci * cb, cb), cb)
            ms = [m_s[gi, rows, :] for gi in range(len(DILATIONS))]
            mx = functools.reduce(jnp.maximum, ms)
            num = jnp.zeros((cb, LANES), F32)
            den = jnp.zeros((cb, LANES), F32)
            for gi in range(len(DILATIONS)):
                w = jnp.exp2(ms[gi] - mx)
                num = num + w * n_s[gi, rows, :]
                den = den + w * l_s[gi, rows, :]
            orow = pl.ds(pl.multiple_of(tile0 + ci * cb, cb), cb)
            y_ref[orow, :] = ((num / den) * _silu(g_ref[orow, :])).astype(y_ref.dtype)
            return c
        lax.fori_loop(0, ATT_TILE // cb, comb_body, 0)
        return carry

    lax.fori_loop(0, n_tiles, tile_body, 0)


def _attention(z, batch, seq):
    t = batch * seq
    n_pairs = ATT_W // LANES

    def col(c0):
        return pl.BlockSpec((seq, LANES), lambda b, h, c0=c0: (b, c0 + h))

    scratch = pltpu.VMEM((len(DILATIONS), ATT_TILE, LANES), F32)
    return pl.pallas_call(
        functools.partial(_att_kernel, seq=seq),
        grid=(batch, n_pairs),
        in_specs=[col(COL_AQ), col(COL_AK), col(COL_AV), col(COL_AG)],
        out_specs=pl.BlockSpec((seq, LANES), lambda b, h: (b, h)),
        out_shape=jax.ShapeDtypeStruct((t, ATT_W), BF16),
        scratch_shapes=[scratch, scratch, scratch, pltpu.VMEM((3, 2 * ATT_QB, ATT_KB), F32),
                        pltpu.VMEM((ATT_TILE // ATT_QB, 2 * ATT_QB, ATT_KB), BF16)],
        compiler_params=pltpu.CompilerParams(
            dimension_semantics=("parallel", "parallel"), vmem_limit_bytes=VMEM_LIMIT),
        name="band_attention",
    )(z, z, z, z)


def _chunk_cumsum(g, tri):
    c = HG_CHUNK
    g1 = g.astype(BF16)
    g2 = (g - g1.astype(F32)).astype(BF16)
    terms = jnp.concatenate([g1, g2], axis=1)
    k = g.shape[1]
    outs = []
    for i in range(g.shape[0] // c):
        part = jnp.dot(tri, terms[i * c:(i + 1) * c, :], preferred_element_type=F32)
        outs.append(part[:, k:] + part[:, :k])
    return jnp.concatenate(outs, axis=0)


def _hgrn_kernel(hq_ref, ff_ref, fb_ref, hv_ref, hg_ref, lbf_ref, lbb_ref, ghg_ref, y_ref,
                 q_s, kf_s, kb_s, bf_s, bb_s, o_s,
                 qtf_s, ktf_s, qtb_s, ktb_s, v16_s,
                 emf_s, emb_s, stf_s, stb_s, *, seq, layer):
    C = HG_CHUNK
    n_chunks = seq // C

    def lower_bound(ref):
        a = ref[...]
        e = jnp.exp(a - jnp.max(a, axis=0, keepdims=True))
        return jnp.sum(e[0:layer + 1, :], axis=0, keepdims=True) / jnp.sum(e, axis=0, keepdims=True)

    lbf = lower_bound(lbf_ref)
    lbb = lower_bound(lbb_ref)

    ti = lax.broadcasted_iota(jnp.int32, (C, C), 0)
    si = lax.broadcasted_iota(jnp.int32, (C, C), 1)
    tri_f = si <= ti
    tri_b = si >= ti

    pr = HG_PRE_ROWS

    cpb = pr // C
    fwd = (ff_ref, lbf, kf_s, bf_s, qtf_s, ktf_s, emf_s, False)
    bwd = (fb_ref, lbb, kb_s, bb_s, qtb_s, ktb_s, emb_s, True)

    def per_chunk_rows(x3):
        return jnp.broadcast_to(x3, (cpb, C, HG_DK)).reshape(pr, HG_DK)

    def pre_body(i, lo):
        rows = pl.ds(pl.multiple_of(i * pr, pr), pr)
        q = _silu(hq_ref[rows, :])
        q_s[rows, :] = q
        o_s[rows, :] = jnp.zeros((pr, HG_DK), F32)
        v16_s[rows, :] = hv_ref[rows, :].astype(BF16)
        for f_ref, lb, k_s, b_s, qt_s, kt_s, em_s, reverse in (fwd, bwd):
            f = lb + (1.0 - lb) * _sigmoid(f_ref[rows, :])
            kk = 1.0 - f
            tri = tri_b if reverse else tri_f
            b = _chunk_cumsum(jnp.log(f), tri.astype(BF16))
            last = 0 if reverse else C - 1
            btot = b.reshape(cpb, C, HG_DK)[:, last:last + 1, :]
            mid = 0.5 * btot
            midr = per_chunk_rows(mid)
            k_s[rows, :] = kk
            b_s[rows, :] = b
            qt_s[rows, :] = (q * jnp.exp(b - midr)).astype(BF16)
            kt_s[rows, :] = (kk * jnp.exp(midr - b)).astype(BF16)
            em_s[pl.ds(pl.multiple_of(i * cpb, cpb), cpb), :] = jnp.exp(mid).reshape(cpb, HG_DK)
            lo = jnp.minimum(lo, btot.reshape(cpb, HG_DK))
        return lo

    lo = lax.fori_loop(0, seq // pr, pre_body, jnp.zeros((cpb, HG_DK), F32))
    fast = jnp.min(lo) >= HG_FAST_MIN_LOGDECAY

    stf_s[...] = jnp.zeros_like(stf_s)
    stb_s[...] = jnp.zeros_like(stb_s)

    def chunk(ci, tri, d, st_ref, exact):
        _, _, k_s, b_s, qt_s, kt_s, em_s, reverse = d
        r0 = pl.multiple_of(ci * C, C)
        rows = pl.ds(r0, C)
        v = v16_s[rows, :]
        st = st_ref[...]
        if exact:
            q = q_s[rows, :]
            b = b_s[rows, :]
            kk = k_s[rows, :]
            btot = b_s[pl.ds(r0 + (0 if reverse else C - 1), 1), :]

            def col_body(s, att):
                w = q * k_s[pl.ds(r0 + s, 1), :] * jnp.exp(jnp.minimum(b - b_s[pl.ds(r0 + s, 1), :], 0.0))
                return att + jnp.where(si == s, jnp.sum(w, axis=-1, keepdims=True), 0.0)
            att = lax.fori_loop(0, C, col_body, jnp.zeros((C, C), F32))
            o = _bdot(jnp.where(tri, att, 0.0), v) + _bdot_nt(q * jnp.exp(b), st)
            st_ref[...] = st * jnp.exp(btot) + _bdot_tn(v, kk * jnp.exp(btot - b))
        else:
            em = em_s[pl.ds(ci, 1), :]
            qt = qt_s[rows, :]
            kt = kt_s[rows, :]
            att = jnp.where(tri, _bdot_nt(qt, kt), 0.0)
            o = _bdot(att, v) + _bdot_nt(qt, st * em)
            st_ref[...] = st * (em * em) + _bdot_tn(v, kt) * em
        o_s[rows, :] = o_s[rows, :] + o

    def scan(exact):
        def body(n, carry):
            chunk(n, tri_f, fwd, stf_s, exact)
            chunk(n_chunks - 1 - n, tri_b, bwd, stb_s, exact)
            return carry
        lax.fori_loop(0, n_chunks, body, 0, unroll=1 if exact else HG_UNROLL)

    lax.cond(fast, lambda: scan(False), lambda: scan(True))

    def post_body(i, carry):
        rows = pl.ds(pl.multiple_of(i * pr, pr), pr)
        o = o_s[rows, :]
        ms = jnp.mean(o * o, axis=-1, keepdims=True)
        y = (o * lax.rsqrt(ms + EPS)) * ghg_ref[...]
        y_ref[rows, :] = (y * _silu(hg_ref[rows, :])).astype(y_ref.dtype)
        return carry
    lax.fori_loop(0, seq // pr, post_body, 0)


def _hgrn(z, lb_fwd, lb_bwd, g_hg, batch, seq, layer):
    t = batch * seq

    def col(c0):
        return pl.BlockSpec((seq, LANES), lambda b, h, c0=c0: (b, c0 + h))

    n_lb = lb_fwd.shape[0]
    lb_spec = pl.BlockSpec((n_lb, HG_DK), lambda b, h: (0, h))
    seq_f32 = pltpu.VMEM((seq, HG_DK), F32)
    seq_bf16 = pltpu.VMEM((seq, HG_DK), BF16)
    chunk_f32 = pltpu.VMEM((seq // HG_CHUNK, HG_DK), F32)
    state = pltpu.VMEM((HG_DK, HG_DK), F32)
    return pl.pallas_call(
        functools.partial(_hgrn_kernel, seq=seq, layer=layer),
        grid=(batch, HG_HEADS),
        in_specs=[col(COL_HQ), col(COL_HFF), col(COL_HFB), col(COL_HV), col(COL_HG),
                  lb_spec, lb_spec, pl.BlockSpec((1, HG_DK), lambda b, h: (0, 0))],
        out_specs=pl.BlockSpec((seq, HG_DK), lambda b, h: (b, h)),
        out_shape=jax.ShapeDtypeStruct((t, HG_W), BF16),
        scratch_shapes=[seq_f32] * 6 + [seq_bf16] * 5 + [chunk_f32, chunk_f32, state, state],
        compiler_params=pltpu.CompilerParams(
            dimension_semantics=("parallel", "parallel"), vmem_limit_bytes=VMEM_LIMIT),
        name="hgrn2",
    )(z, z, z, z, z, lb_fwd, lb_bwd, g_hg.reshape(1, HG_DK))


def _rms(y, g):
    ms = jnp.mean(y * y, axis=-1, keepdims=True)
    return (y * lax.rsqrt(ms + EPS)) * g


def _outproj_kernel(x_ref, ya_ref, yh_ref, p_ref, wo_ref, wpg_ref, wpp_ref, gpost_ref, gple_ref, o_ref):
    sub = x_ref.shape[0] // OUT_SUBBLOCKS
    for r in range(OUT_SUBBLOCKS):
        rows = slice(r * sub, (r + 1) * sub)
        y = (jnp.dot(ya_ref[rows, :], wo_ref[0:ATT_W, :], preferred_element_type=F32)
             + jnp.dot(yh_ref[rows, :], wo_ref[ATT_W:, :], preferred_element_type=F32))
        h1 = x_ref[rows, :] + _rms(y, gpost_ref[...])
        gate = jax.nn.sigmoid(jnp.dot(h1.astype(BF16), wpg_ref[...], preferred_element_type=F32))
        e = jnp.dot(p_ref[rows, :].astype(BF16), wpp_ref[...], preferred_element_type=F32)
        o_ref[rows, :] = h1 + _rms(gate * e, gple_ref[...])


def _outproj(x2, y_att, y_hg, p2, w_out, w_pg, w_pp, g_post, g_ple):
    t = x2.shape[0]
    tm = min(OUT_TM, t)
    row = lambda i: (i, 0)
    const = lambda i: (0, 0)
    resident = dict(pipeline_mode=pl.Buffered(1))
    return pl.pallas_call(
        _outproj_kernel,
        grid=(t // tm,),
        in_specs=[pl.BlockSpec((tm, D_MODEL), row),
                  pl.BlockSpec((tm, ATT_W), row),
                  pl.BlockSpec((tm, HG_W), row),
                  pl.BlockSpec((tm, PLE_DIM), row),
                  pl.BlockSpec((ATT_W + HG_W, D_MODEL), const, **resident),
                  pl.BlockSpec((D_MODEL, D_MODEL), const, **resident),
                  pl.BlockSpec((PLE_DIM, D_MODEL), const, **resident),
                  pl.BlockSpec((1, D_MODEL), const),
                  pl.BlockSpec((1, D_MODEL), const)],
        out_specs=pl.BlockSpec((tm, D_MODEL), row),
        out_shape=jax.ShapeDtypeStruct((t, D_MODEL), F32),
        compiler_params=pltpu.CompilerParams(
            dimension_semantics=("parallel",), vmem_limit_bytes=VMEM_LIMIT),
        name="outproj_ple",
    )(x2, y_att, y_hg, p2, w_out, w_pg, w_pp, g_post.reshape(1, D_MODEL), g_ple.reshape(1, D_MODEL))


def kernel(x, p, positions, w_in, w_out, g_pre, g_post, g_hg, lb_fwd, lb_bwd, w_pg, w_pp, g_ple):
    depth = w_in.shape[0]
    batch, seq, _ = x.shape
    assert seq % ATT_TILE == 0 and seq // max(DILATIONS) >= ATT_KB
    t = batch * seq
    cos_t, sin_t = _rope_tables(positions)
    h = x.reshape(t, D_MODEL)
    for i in range(depth):
        z = _inproj(h, g_pre[i], _prep_w_in(w_in[i]), cos_t, sin_t)
        y_att = _attention(z, batch, seq)
        y_hg = _hgrn(z, lb_fwd, lb_bwd, g_hg[i], batch, seq, i)
        h = _outproj(h, y_att, y_hg, p[i].reshape(t, PLE_DIM), w_out[i].astype(BF16),
                     w_pg[i].astype(BF16), w_pp[i].astype(BF16), g_post[i], g_ple[i])
    return h.reshape(batch, seq, D_MODEL)
```

```python
import functools

import jax
import jax.numpy as jnp
from jax import lax
from jax.experimental import pallas as pl
from jax.experimental.pallas import tpu as pltpu

F32 = jnp.float32
BF16 = jnp.bfloat16

D_MODEL = 2048
PLE_DIM = 256
ATT_HEADS = 16
ATT_HD = 64
ATT_W = ATT_HEADS * ATT_HD
ROT_DIM = ATT_HD // 4
ROPE_THETA = 500000.0
DILATION_CFG = ((128, 1), (512, 4), (2048, 16))
DILATIONS = tuple(d for _, d in DILATION_CFG)
HALF = (DILATION_CFG[0][0] // 2) // DILATION_CFG[0][1]
assert all((w // 2) // d == HALF for w, d in DILATION_CFG)
HG_HEADS = 8
HG_DK = 128
HG_W = HG_HEADS * HG_DK
HG_CHUNK = 128
IN_W = 4 * ATT_W + 5 * HG_W
EPS = 1e-6
NEG = -1e30
LOG2E = 1.4426950408889634

LANES = 128
VMEM_LIMIT = 56 * 1024 * 1024

COL_AQ, COL_AK, COL_AV, COL_AG = (i * ATT_W // LANES for i in range(4))
COL_HQ, COL_HFF, COL_HFB, COL_HV, COL_HG = (4 * ATT_W // LANES + i * HG_W // LANES for i in range(5))

ROPE_TR = 1024
WPREP_TN = 512
INPROJ_TM = 1024
INPROJ_TN = 1024
INPROJ_ROPE_CHUNK = 256
ATT_TILE = 1024
ATT_QB = 2 * HALF
ATT_KB = ATT_QB + 2 * HALF
ATT_DENSE_MAX = 256
ATT_DENSE_UNROLL = 8
HG_UNROLL = 16
HG_PRE_ROWS = 512
OUT_TM = 512
OUT_SUBBLOCKS = 2
HG_FAST_MIN_LOGDECAY = -120.0


def _bdot(a, b):
    return jnp.dot(a.astype(BF16), b.astype(BF16), preferred_element_type=F32)


def _bdot_nt(a, b):
    return lax.dot_general(a.astype(BF16), b.astype(BF16), (((1,), (1,)), ((), ())),
                           preferred_element_type=F32)


def _bdot_tn(a, b):
    return lax.dot_general(a.astype(BF16), b.astype(BF16), (((0,), (0,)), ((), ())),
                           preferred_element_type=F32)


def _sigmoid(x):
    return 0.5 * jnp.tanh(0.5 * x) + 0.5


def _silu(x):
    return x * _sigmoid(x)


def _pair_swap_matrix():
    h = ROT_DIM // 2
    src = lax.broadcasted_iota(jnp.int32, (LANES, LANES), 0)
    dst = lax.broadcasted_iota(jnp.int32, (LANES, LANES), 1)
    shift = ATT_HD - h
    sigma = jnp.where((dst >= h) & (dst < ROT_DIM), dst + shift,
                      jnp.where((dst >= ATT_HD) & (dst < ATT_HD + h), dst - shift, dst))
    return (src == sigma).astype(BF16)


def _wprep_kernel(w_ref, o_ref, *, swap_tiles, rb):
    j = pl.program_id(0)
    n_rows, tn = w_ref.shape

    @pl.when(j < swap_tiles)
    def _():
        perm = _pair_swap_matrix()

        def body(i, carry):
            rows = pl.ds(pl.multiple_of(i * rb, rb), rb)
            for cb in range(tn // LANES):
                cols = slice(cb * LANES, (cb + 1) * LANES)
                o_ref[rows, cols] = jnp.dot(w_ref[rows, cols].astype(BF16), perm,
                                            preferred_element_type=F32).astype(BF16)
            return carry
        lax.fori_loop(0, n_rows // rb, body, 0)

    @pl.when(j >= swap_tiles)
    def _():
        def body(i, carry):
            rows = pl.ds(pl.multiple_of(i * rb, rb), rb)
            o_ref[rows, :] = w_ref[rows, :].astype(BF16)
            return carry
        lax.fori_loop(0, n_rows // rb, body, 0)


def _prep_w_in(w):
    k, n = w.shape
    tn = WPREP_TN
    return pl.pallas_call(
        functools.partial(_wprep_kernel, swap_tiles=2 * ATT_W // tn, rb=256),
        grid=(n // tn,),
        in_specs=[pl.BlockSpec((k, tn), lambda j: (0, j))],
        out_specs=pl.BlockSpec((k, tn), lambda j: (0, j)),
        out_shape=jax.ShapeDtypeStruct((k, n), BF16),
        compiler_params=pltpu.CompilerParams(
            dimension_semantics=("parallel",), vmem_limit_bytes=VMEM_LIMIT),
        name="w_in_prep",
    )(w)


def _head_a_lanes(lane):
    h = ROT_DIM // 2
    return (lane < h) | ((lane >= ROT_DIM) & (lane < ATT_HD + h))


def _rope_table_kernel(pos_ref, inv_ref, cos_ref, sin_ref):
    ang = pos_ref[...].astype(F32) * inv_ref[...]
    lane = lax.broadcasted_iota(jnp.int32, ang.shape, 1)
    c = jnp.cos(ang)
    s = jnp.sin(ang)
    is_x1 = lane < ROT_DIM
    is_x2 = (lane >= ATT_HD) & (lane < ATT_HD + ROT_DIM)
    cos_ref[...] = jnp.where(is_x1 | is_x2, c, 1.0)
    sin_ref[...] = jnp.where(is_x1, -s, jnp.where(is_x2, s, 0.0))


def _rope_tables(positions):
    t = positions.size
    pos = positions.reshape(t, 1)
    inv = jnp.power(ROPE_THETA, -jnp.arange(0, ROT_DIM, 2, dtype=F32) / ROT_DIM)
    inv_lane = jnp.tile(inv, LANES // inv.shape[0]).reshape(1, LANES)
    tr = min(ROPE_TR, t)
    out = jax.ShapeDtypeStruct((t, LANES), F32)
    return pl.pallas_call(
        _rope_table_kernel,
        grid=(t // tr,),
        in_specs=[pl.BlockSpec((tr, 1), lambda i: (i, 0)),
                  pl.BlockSpec((1, LANES), lambda i: (0, 0))],
        out_specs=[pl.BlockSpec((tr, LANES), lambda i: (i, 0))] * 2,
        out_shape=[out] * 2,
        name="rope_tables",
    )(pos, inv_lane)


def _inproj_kernel(x_ref, g_ref, w_ref, cos_ref, sin_ref, z_ref, u_ref, *, tm, tn):
    j = pl.program_id(1)
    rb = 128
    n_q = ATT_W // tn
    n_rope = 2 * ATT_W // tn

    def norm_rows(r0, n):
        for i in range(n // rb):
            rows = slice(r0 + i * rb, r0 + (i + 1) * rb)
            xb = x_ref[rows, :]
            ms = jnp.mean(xb * xb, axis=-1, keepdims=True)
            u_ref[rows, :] = ((xb * lax.rsqrt(ms + EPS)) * g_ref[...]).astype(BF16)

    def project_rotate(r0, n):
        rows = slice(r0, r0 + n)
        scale = jnp.where(j < n_q, ATT_HD ** -0.5 * LOG2E, 1.0).astype(F32)
        c = cos_ref[rows, :] * scale
        s = sin_ref[rows, :] * scale
        for cb in range(tn // INPROJ_ROPE_CHUNK):
            cols = slice(cb * INPROJ_ROPE_CHUNK, (cb + 1) * INPROJ_ROPE_CHUNK)
            zc = jnp.dot(u_ref[rows, :], w_ref[:, cols], preferred_element_type=F32)
            parts = [zc[:, b * LANES:(b + 1) * LANES] for b in range(INPROJ_ROPE_CHUNK // LANES)]
            z_ref[rows, cols] = jnp.concatenate(
                [blk * c + pltpu.roll(blk, LANES // 2, 1) * s for blk in parts], axis=1)

    @pl.when(j == 0)
    def _():
        for r0 in range(0, tm, tm // 2):
            norm_rows(r0, tm // 2)
            project_rotate(r0, tm // 2)

    @pl.when((j > 0) & (j < n_rope))
    def _():
        project_rotate(0, tm)

    @pl.when(j >= n_rope)
    def _():
        z_ref[...] = jnp.dot(u_ref[...], w_ref[...], preferred_element_type=F32)


def _inproj(x2, g_pre, w_in_bf16, cos_t, sin_t):
    t = x2.shape[0]
    tm = min(INPROJ_TM, t)
    tn = INPROJ_TN
    row = lambda i, j: (i, 0)
    return pl.pallas_call(
        functools.partial(_inproj_kernel, tm=tm, tn=tn),
        grid=(t // tm, IN_W // tn),
        in_specs=[pl.BlockSpec((tm, D_MODEL), row),
                  pl.BlockSpec((1, D_MODEL), lambda i, j: (0, 0)),
                  pl.BlockSpec((D_MODEL, tn), lambda i, j: (0, j)),
                  pl.BlockSpec((tm, LANES), row),
                  pl.BlockSpec((tm, LANES), row)],
        out_specs=pl.BlockSpec((tm, tn), lambda i, j: (i, j)),
        out_shape=jax.ShapeDtypeStruct((t, IN_W), F32),
        scratch_shapes=[pltpu.VMEM((tm, D_MODEL), BF16)],
        compiler_params=pltpu.CompilerParams(
            dimension_semantics=("parallel", "arbitrary"), vmem_limit_bytes=VMEM_LIMIT),
        name="inproj",
    )(x2, g_pre.reshape(1, D_MODEL), w_in_bf16, cos_t, sin_t)


def _att_geometry(seq):
    geo = {}
    for d in DILATIONS:
        n_cls = seq // d
        dense = n_cls <= ATT_DENSE_MAX
        geo[d] = (dense, n_cls if dense else ATT_QB, n_cls if dense else ATT_KB)
    return geo


def _att_kernel(q_ref, k_ref, v_ref, g_ref, y_ref, mt_s, lt_s, nt_s, md_s, ld_s, nd_s,
                bias_s, biasd_s, p_s, pd_s, *, seq):
    geo = _att_geometry(seq)
    tiled = [d for d in DILATIONS if not geo[d][0]]
    dense = [d for d in DILATIONS if geo[d][0]]
    n_tiles = seq // ATT_TILE
    ones_kv = {kb: jnp.ones((kb, LANES), BF16) for kb in {geo[d][2] for d in DILATIONS}}

    def lanes(rows):
        lane = lax.broadcasted_iota(jnp.int32, (rows, LANES), 1)
        return lane < ATT_HD, _head_a_lanes(lane)

    def band_bias(qb, kb, delta):
        qa = lax.broadcasted_iota(jnp.int32, (2 * qb, kb), 0) & (qb - 1)
        ka = lax.broadcasted_iota(jnp.int32, (2 * qb, kb), 1)
        return jnp.where(jnp.abs(ka - qa + delta) <= HALF, 0.0, NEG)

    for idx, delta in enumerate((-HALF, 0, -2 * HALF)):
        bias_s[idx] = band_bias(ATT_QB, ATT_KB, delta)
    for d in dense:
        biasd_s[...] = band_bias(geo[d][1], geo[d][2], 0)

    def rows_of(start, size, d):
        return pl.ds(start, size) if d == 1 else pl.ds(start, size, stride=d)

    def score_block(d, r, a0, bias, p_ref, m_ref, dst):
        _, qb, kb = geo[d]
        head_a, qk_head_a = lanes(qb)
        ka0 = 0 if geo[d][0] else jnp.clip(a0 - HALF, 0, seq // d - kb)
        q2 = q_ref[rows_of(r + d * a0, qb, d), :]
        kk = k_ref[rows_of(r + d * ka0, kb, d), :].astype(BF16)
        qs = jnp.concatenate([jnp.where(qk_head_a, q2, 0.0), jnp.where(qk_head_a, 0.0, q2)], axis=0)
        s = _bdot_nt(qs, kk) + bias
        m = jnp.max(s, axis=-1, keepdims=True)
        p_ref[...] = jnp.exp2(s - m).astype(BF16)
        mb = jnp.broadcast_to(m, (2 * qb, LANES))
        m_ref[dst, :] = jnp.where(head_a, mb[:qb], mb[qb:])

    def value_block(d, r, a0, p_ref, l_ref, n_ref, dst):
        _, qb, kb = geo[d]
        head_a, _ = lanes(qb)
        ka0 = 0 if geo[d][0] else jnp.clip(a0 - HALF, 0, seq // d - kb)
        vv = v_ref[rows_of(r + d * ka0, kb, d), :].astype(BF16)
        nl = jnp.dot(p_ref[...], jnp.concatenate([vv, ones_kv[kb]], axis=1), preferred_element_type=F32)
        l_ref[dst, :] = jnp.where(head_a, nl[:qb, LANES:], nl[qb:, LANES:])
        n_ref[dst, :] = jnp.where(head_a, nl[:qb, :LANES], nl[qb:, :LANES])

    for gi, d in enumerate(dense):
        qb = geo[d][1]

        def dense_scores(r, c, gi=gi, d=d, qb=qb):
            score_block(d, r, 0, biasd_s[...], pd_s.at[r], md_s.at[gi], rows_of(r, qb, d))
            return c

        def dense_values(r, c, gi=gi, d=d, qb=qb):
            value_block(d, r, 0, pd_s.at[r], ld_s.at[gi], nd_s.at[gi], rows_of(r, qb, d))
            return c
        lax.fori_loop(0, d, dense_scores, 0, unroll=ATT_DENSE_UNROLL)
        lax.fori_loop(0, d, dense_values, 0, unroll=ATT_DENSE_UNROLL)

    def tile_body(tj, carry):
        tile0 = tj * ATT_TILE
        for gi, d in enumerate(tiled):
            n_cls = seq // d
            per_cls = ATT_TILE // d // ATT_QB
            n_blk = ATT_TILE // ATT_QB
            for bi in range(n_blk):
                r = bi // per_cls
                a0 = tile0 // d + (bi % per_cls) * ATT_QB
                which = jnp.where(a0 == 0, 1, jnp.where(a0 == n_cls - ATT_QB, 2, 0))
                score_block(d, r, a0, bias_s[which], p_s.at[bi], mt_s.at[gi],
                            rows_of(r + d * a0 - tile0, ATT_QB, d))
            for bi in range(n_blk):
                r = bi // per_cls
                a0 = tile0 // d + (bi % per_cls) * ATT_QB
                value_block(d, r, a0, p_s.at[bi], lt_s.at[gi], nt_s.at[gi],
                            rows_of(r + d * a0 - tile0, ATT_QB, d))

        cb = 256

        def comb_body(ci, c):
            rows = pl.ds(pl.multiple_of(ci * cb, cb), cb)
            orow = pl.ds(pl.multiple_of(tile0 + ci * cb, cb), cb)
            parts = ([(mt_s[gi, rows, :], lt_s, nt_s, gi, rows) for gi in range(len(tiled))]
                     + [(md_s[gi, orow, :], ld_s, nd_s, gi, orow) for gi in range(len(dense))])
            mx = functools.reduce(jnp.maximum, [part[0] for part in parts])
            num = jnp.zeros((cb, LANES), F32)
            den = jnp.zeros((cb, LANES), F32)
            for m, l_ref, n_ref, gi, rr in parts:
                w = jnp.exp2(m - mx)
                num = num + w * n_ref[gi, rr, :]
                den = den + w * l_ref[gi, rr, :]
            y_ref[orow, :] = ((num / den) * _silu(g_ref[orow, :])).astype(y_ref.dtype)
            return c
        lax.fori_loop(0, ATT_TILE // cb, comb_body, 0)
        return carry

    lax.fori_loop(0, n_tiles, tile_body, 0)


def _attention(z, batch, seq):
    t = batch * seq
    n_pairs = ATT_W // LANES

    def col(c0):
        return pl.BlockSpec((seq, LANES), lambda b, h, c0=c0: (b, c0 + h))

    geo = _att_geometry(seq)
    dense_geo = sorted({geo[d][1:] for d in DILATIONS if geo[d][0]})
    assert len(dense_geo) <= 1, "one band-mask table serves the dense groups"
    dqb, dkb = dense_geo[0] if dense_geo else (8, LANES)
    n_dense = sum(geo[d][0] for d in DILATIONS)
    tiled = pltpu.VMEM((max(len(DILATIONS) - n_dense, 1), ATT_TILE, LANES), F32)
    whole = pltpu.VMEM((max(n_dense, 1), seq, LANES), F32)
    return pl.pallas_call(
        functools.partial(_att_kernel, seq=seq),
        grid=(batch, n_pairs),
        in_specs=[col(COL_AQ), col(COL_AK), col(COL_AV), col(COL_AG)],
        out_specs=pl.BlockSpec((seq, LANES), lambda b, h: (b, h)),
        out_shape=jax.ShapeDtypeStruct((t, ATT_W), BF16),
        scratch_shapes=[tiled, tiled, tiled, whole, whole, whole,
                        pltpu.VMEM((3, 2 * ATT_QB, ATT_KB), F32),
                        pltpu.VMEM((2 * dqb, dkb), F32),
                        pltpu.VMEM((ATT_TILE // ATT_QB, 2 * ATT_QB, ATT_KB), BF16),
                        pltpu.VMEM((max(DILATIONS), 2 * dqb, dkb), BF16)],
        compiler_params=pltpu.CompilerParams(
            dimension_semantics=("parallel", "parallel"), vmem_limit_bytes=VMEM_LIMIT),
        name="band_attention",
    )(z, z, z, z)


def _chunk_cumsum(g, tri):
    c = HG_CHUNK
    g1 = g.astype(BF16)
    g2 = (g - g1.astype(F32)).astype(BF16)
    terms = jnp.concatenate([g1, g2], axis=1)
    k = g.shape[1]
    outs = []
    for i in range(g.shape[0] // c):
        part = jnp.dot(tri, terms[i * c:(i + 1) * c, :], preferred_element_type=F32)
        outs.append(part[:, k:] + part[:, :k])
    return jnp.concatenate(outs, axis=0)


def _hgrn_kernel(hq_ref, ff_ref, fb_ref, hv_ref, hg_ref, lbf_ref, lbb_ref, ghg_ref, y_ref,
                 q_s, kf_s, kb_s, bf_s, bb_s, o_s,
                 qtf_s, ktf_s, qtb_s, ktb_s, v16_s,
                 emf_s, emb_s, stf_s, stb_s, *, seq, layer):
    C = HG_CHUNK
    n_chunks = seq // C

    def lower_bound(ref):
        a = ref[...]
        e = jnp.exp(a - jnp.max(a, axis=0, keepdims=True))
        return jnp.sum(e[0:layer + 1, :], axis=0, keepdims=True) / jnp.sum(e, axis=0, keepdims=True)

    lbf = lower_bound(lbf_ref)
    lbb = lower_bound(lbb_ref)

    ti = lax.broadcasted_iota(jnp.int32, (C, C), 0)
    si = lax.broadcasted_iota(jnp.int32, (C, C), 1)
    tri_f = si <= ti
    tri_b = si >= ti

    pr = HG_PRE_ROWS

    cpb = pr // C
    fwd = (ff_ref, lbf, kf_s, bf_s, qtf_s, ktf_s, emf_s, False)
    bwd = (fb_ref, lbb, kb_s, bb_s, qtb_s, ktb_s, emb_s, True)

    def per_chunk_rows(x3):
        return jnp.broadcast_to(x3, (cpb, C, HG_DK)).reshape(pr, HG_DK)

    def pre_body(i, lo):
        rows = pl.ds(pl.multiple_of(i * pr, pr), pr)
        q = _silu(hq_ref[rows, :])
        q_s[rows, :] = q
        o_s[rows, :] = jnp.zeros((pr, HG_DK), F32)
        v16_s[rows, :] = hv_ref[rows, :].astype(BF16)
        for f_ref, lb, k_s, b_s, qt_s, kt_s, em_s, reverse in (fwd, bwd):
            f = lb + (1.0 - lb) * _sigmoid(f_ref[rows, :])
            kk = 1.0 - f
            tri = tri_b if reverse else tri_f
            b = _chunk_cumsum(jnp.log(f), tri.astype(BF16))
            last = 0 if reverse else C - 1
            btot = b.reshape(cpb, C, HG_DK)[:, last:last + 1, :]
            mid = 0.5 * btot
            midr = per_chunk_rows(mid)
            k_s[rows, :] = kk
            b_s[rows, :] = b
            qt_s[rows, :] = (q * jnp.exp(b - midr)).astype(BF16)
            kt_s[rows, :] = (kk * jnp.exp(midr - b)).astype(BF16)
            em_s[pl.ds(pl.multiple_of(i * cpb, cpb), cpb), :] = jnp.exp(mid).reshape(cpb, HG_DK)
            lo = jnp.minimum(lo, btot.reshape(cpb, HG_DK))
        return lo

    lo = lax.fori_loop(0, seq // pr, pre_body, jnp.zeros((cpb, HG_DK), F32))
    fast = jnp.min(lo) >= HG_FAST_MIN_LOGDECAY

    stf_s[...] = jnp.zeros_like(stf_s)
    stb_s[...] = jnp.zeros_like(stb_s)

    def chunk(ci, tri, d, st_ref, exact):
        _, _, k_s, b_s, qt_s, kt_s, em_s, reverse = d
        r0 = pl.multiple_of(ci * C, C)
        rows = pl.ds(r0, C)
        v = v16_s[rows, :]
        st = st_ref[...]
        if exact:
            q = q_s[rows, :]
            b = b_s[rows, :]
            kk = k_s[rows, :]
            btot = b_s[pl.ds(r0 + (0 if reverse else C - 1), 1), :]

            def col_body(s, att):
                w = q * k_s[pl.ds(r0 + s, 1), :] * jnp.exp(jnp.minimum(b - b_s[pl.ds(r0 + s, 1), :], 0.0))
                return att + jnp.where(si == s, jnp.sum(w, axis=-1, keepdims=True), 0.0)
            att = lax.fori_loop(0, C, col_body, jnp.zeros((C, C), F32))
            o = _bdot(jnp.where(tri, att, 0.0), v) + _bdot_nt(q * jnp.exp(b), st)
            st_ref[...] = st * jnp.exp(btot) + _bdot_tn(v, kk * jnp.exp(btot - b))
        else:
            em = em_s[pl.ds(ci, 1), :]
            qt = qt_s[rows, :]
            kt = kt_s[rows, :]
            att = jnp.where(tri, _bdot_nt(qt, kt), 0.0)
            o = _bdot(att, v) + _bdot_nt(qt, st * em)
            st_ref[...] = st * (em * em) + _bdot_tn(v, kt) * em
        o_s[rows, :] = o_s[rows, :] + o

    def scan(exact):
        def body(n, carry):
            chunk(n, tri_f, fwd, stf_s, exact)
            chunk(n_chunks - 1 - n, tri_b, bwd, stb_s, exact)
            return carry
        lax.fori_loop(0, n_chunks, body, 0, unroll=1 if exact else HG_UNROLL)

    lax.cond(fast, lambda: scan(False), lambda: scan(True))

    def post_body(i, carry):
        rows = pl.ds(pl.multiple_of(i * pr, pr), pr)
        o = o_s[rows, :]
        ms = jnp.mean(o * o, axis=-1, keepdims=True)
        y = (o * lax.rsqrt(ms + EPS)) * ghg_ref[...]
        y_ref[rows, :] = (y * _silu(hg_ref[rows, :])).astype(y_ref.dtype)
        return carry
    lax.fori_loop(0, seq // pr, post_body, 0)


def _hgrn(z, lb_fwd, lb_bwd, g_hg, batch, seq, layer):
    t = batch * seq

    def col(c0):
        return pl.BlockSpec((seq, LANES), lambda b, h, c0=c0: (b, c0 + h))

    n_lb = lb_fwd.shape[0]
    lb_spec = pl.BlockSpec((n_lb, HG_DK), lambda b, h: (0, h))
    seq_f32 = pltpu.VMEM((seq, HG_DK), F32)
    seq_bf16 = pltpu.VMEM((seq, HG_DK), BF16)
    chunk_f32 = pltpu.VMEM((seq // HG_CHUNK, HG_DK), F32)
    state = pltpu.VMEM((HG_DK, HG_DK), F32)
    return pl.pallas_call(
        functools.partial(_hgrn_kernel, seq=seq, layer=layer),
        grid=(batch, HG_HEADS),
        in_specs=[col(COL_HQ), col(COL_HFF), col(COL_HFB), col(COL_HV), col(COL_HG),
                  lb_spec, lb_spec, pl.BlockSpec((1, HG_DK), lambda b, h: (0, 0))],
        out_specs=pl.BlockSpec((seq, HG_DK), lambda b, h: (b, h)),
        out_shape=jax.ShapeDtypeStruct((t, HG_W), BF16),
        scratch_shapes=[seq_f32] * 6 + [seq_bf16] * 5 + [chunk_f32, chunk_f32, state, state],
        compiler_params=pltpu.CompilerParams(
            dimension_semantics=("parallel", "parallel"), vmem_limit_bytes=VMEM_LIMIT),
        name="hgrn2",
    )(z, z, z, z, z, lb_fwd, lb_bwd, g_hg.reshape(1, HG_DK))


def _rms(y, g):
    ms = jnp.mean(y * y, axis=-1, keepdims=True)
    return (y * lax.rsqrt(ms + EPS)) * g


def _outproj_kernel(x_ref, ya_ref, yh_ref, p_ref, wo_ref, wpg_ref, wpp_ref, gpost_ref, gple_ref, o_ref):
    sub = x_ref.shape[0] // OUT_SUBBLOCKS
    for r in range(OUT_SUBBLOCKS):
        rows = slice(r * sub, (r + 1) * sub)
        y = (jnp.dot(ya_ref[rows, :], wo_ref[0:ATT_W, :], preferred_element_type=F32)
             + jnp.dot(yh_ref[rows, :], wo_ref[ATT_W:, :], preferred_element_type=F32))
        h1 = x_ref[rows, :] + _rms(y, gpost_ref[...])
        gate = jax.nn.sigmoid(jnp.dot(h1.astype(BF16), wpg_ref[...], preferred_element_type=F32))
        e = jnp.dot(p_ref[rows, :].astype(BF16), wpp_ref[...], preferred_element_type=F32)
        o_ref[rows, :] = h1 + _rms(gate * e, gple_ref[...])


def _outproj(x2, y_att, y_hg, p2, w_out, w_pg, w_pp, g_post, g_ple):
    t = x2.shape[0]
    tm = min(OUT_TM, t)
    row = lambda i: (i, 0)
    const = lambda i: (0, 0)
    resident = dict(pipeline_mode=pl.Buffered(1))
    return pl.pallas_call(
        _outproj_kernel,
        grid=(t // tm,),
        in_specs=[pl.BlockSpec((tm, D_MODEL), row),
                  pl.BlockSpec((tm, ATT_W), row),
                  pl.BlockSpec((tm, HG_W), row),
                  pl.BlockSpec((tm, PLE_DIM), row),
                  pl.BlockSpec((ATT_W + HG_W, D_MODEL), const, **resident),
                  pl.BlockSpec((D_MODEL, D_MODEL), const, **resident),
                  pl.BlockSpec((PLE_DIM, D_MODEL), const, **resident),
                  pl.BlockSpec((1, D_MODEL), const),
                  pl.BlockSpec((1, D_MODEL), const)],
        out_specs=pl.BlockSpec((tm, D_MODEL), row),
        out_shape=jax.ShapeDtypeStruct((t, D_MODEL), F32),
        compiler_params=pltpu.CompilerParams(
            dimension_semantics=("parallel",), vmem_limit_bytes=VMEM_LIMIT),
        name="outproj_ple",
    )(x2, y_att, y_hg, p2, w_out, w_pg, w_pp, g_post.reshape(1, D_MODEL), g_ple.reshape(1, D_MODEL))


def kernel(x, p, positions, w_in, w_out, g_pre, g_post, g_hg, lb_fwd, lb_bwd, w_pg, w_pp, g_ple):
    depth = w_in.shape[0]
    batch, seq, _ = x.shape
    assert seq % ATT_TILE == 0 and seq // max(DILATIONS) >= ATT_KB
    t = batch * seq
    cos_t, sin_t = _rope_tables(positions)
    h = x.reshape(t, D_MODEL)
    for i in range(depth):
        z = _inproj(h, g_pre[i], _prep_w_in(w_in[i]), cos_t, sin_t)
        y_att = _attention(z, batch, seq)
        y_hg = _hgrn(z, lb_fwd, lb_bwd, g_hg[i], batch, seq, i)
        h = _outproj(h, y_att, y_hg, p[i].reshape(t, PLE_DIM), w_out[i].astype(BF16),
                     w_pg[i].astype(BF16), w_pp[i].astype(BF16), g_post[i], g_ple[i])
    return h.reshape(batch, seq, D_MODEL)
```

```python
import functools

import jax
import jax.numpy as jnp
from jax import lax
from jax.experimental import pallas as pl
from jax.experimental.pallas import tpu as pltpu

F32 = jnp.float32
BF16 = jnp.bfloat16

D_MODEL = 2048
PLE_DIM = 256
ATT_HEADS = 16
ATT_HD = 64
ATT_W = ATT_HEADS * ATT_HD
ROT_DIM = ATT_HD // 4
ROPE_THETA = 500000.0
DILATION_CFG = ((128, 1), (512, 4), (2048, 16))
DILATIONS = tuple(d for _, d in DILATION_CFG)
HALF = (DILATION_CFG[0][0] // 2) // DILATION_CFG[0][1]
assert all((w // 2) // d == HALF for w, d in DILATION_CFG)
HG_HEADS = 8
HG_DK = 128
HG_W = HG_HEADS * HG_DK
HG_CHUNK = 128
IN_W = 4 * ATT_W + 5 * HG_W
EPS = 1e-6
NEG = -1e30
LOG2E = 1.4426950408889634

LANES = 128
VMEM_LIMIT = 56 * 1024 * 1024

COL_AQ, COL_AK, COL_AV, COL_AG = (i * ATT_W // LANES for i in range(4))
COL_HQ, COL_HFF, COL_HFB, COL_HV, COL_HG = (4 * ATT_W // LANES + i * HG_W // LANES for i in range(5))

ROPE_TR = 1024
WPREP_TN = 512
INPROJ_TM = 1024
INPROJ_TN = 1024
INPROJ_ROPE_CHUNK = 256
ATT_TILE = 1024
ATT_QB = 2 * HALF
ATT_KB = ATT_QB + 2 * HALF
ATT_DENSE_MAX = 256
ATT_DENSE_UNROLL = 8
HG_UNROLL = 16
HG_PRE_ROWS = 512
OUT_TM = 512
OUT_SUBBLOCKS = 2
HG_FAST_MIN_LOGDECAY = -120.0


def _bdot(a, b):
    return jnp.dot(a.astype(BF16), b.astype(BF16), preferred_element_type=F32)


def _bdot_nt(a, b):
    return lax.dot_general(a.astype(BF16), b.astype(BF16), (((1,), (1,)), ((), ())),
                           preferred_element_type=F32)


def _bdot_tn(a, b):
    return lax.dot_general(a.astype(BF16), b.astype(BF16), (((0,), (0,)), ((), ())),
                           preferred_element_type=F32)


def _sigmoid(x):
    return 0.5 * jnp.tanh(0.5 * x) + 0.5


def _silu(x):
    return x * _sigmoid(x)


def _pair_swap_matrix():
    h = ROT_DIM // 2
    src = lax.broadcasted_iota(jnp.int32, (LANES, LANES), 0)
    dst = lax.broadcasted_iota(jnp.int32, (LANES, LANES), 1)
    shift = ATT_HD - h
    sigma = jnp.where((dst >= h) & (dst < ROT_DIM), dst + shift,
                      jnp.where((dst >= ATT_HD) & (dst < ATT_HD + h), dst - shift, dst))
    return (src == sigma).astype(BF16)


def _wprep_kernel(w_ref, o_ref, *, swap_tiles, rb):
    j = pl.program_id(0)
    n_rows, tn = w_ref.shape

    @pl.when(j < swap_tiles)
    def _():
        perm = _pair_swap_matrix()

        def body(i, carry):
            rows = pl.ds(pl.multiple_of(i * rb, rb), rb)
            for cb in range(tn // LANES):
                cols = slice(cb * LANES, (cb + 1) * LANES)
                o_ref[rows, cols] = jnp.dot(w_ref[rows, cols].astype(BF16), perm,
                                            preferred_element_type=F32).astype(BF16)
            return carry
        lax.fori_loop(0, n_rows // rb, body, 0)

    @pl.when(j >= swap_tiles)
    def _():
        def body(i, carry):
            rows = pl.ds(pl.multiple_of(i * rb, rb), rb)
            o_ref[rows, :] = w_ref[rows, :].astype(BF16)
            return carry
        lax.fori_loop(0, n_rows // rb, body, 0)


def _prep_w_in(w):
    k, n = w.shape
    tn = WPREP_TN
    return pl.pallas_call(
        functools.partial(_wprep_kernel, swap_tiles=2 * ATT_W // tn, rb=256),
        grid=(n // tn,),
        in_specs=[pl.BlockSpec((k, tn), lambda j: (0, j))],
        out_specs=pl.BlockSpec((k, tn), lambda j: (0, j)),
        out_shape=jax.ShapeDtypeStruct((k, n), BF16),
        compiler_params=pltpu.CompilerParams(
            dimension_semantics=("parallel",), vmem_limit_bytes=VMEM_LIMIT),
        name="w_in_prep",
    )(w)


def _head_a_lanes(lane):
    h = ROT_DIM // 2
    return (lane < h) | ((lane >= ROT_DIM) & (lane < ATT_HD + h))


ROPE_FREQS = ROT_DIM // 2
ROPE_GROUPS = LANES // ROPE_FREQS


def _split3(x):
    t1 = x.astype(BF16)
    r1 = x - t1.astype(F32)
    t2 = r1.astype(BF16)
    return t1, t2, (r1 - t2.astype(F32)).astype(BF16)


def _rope_table_kernel(pos_ref, inv_ref, cos_ref, sin_ref, c_s, s_s):
    p = pl.program_id(0)

    @pl.when(p == 0)
    def _():
        ang = pos_ref[...].astype(F32) * inv_ref[...]
        c_s[...] = jnp.concatenate(_split3(jnp.cos(ang)), axis=1)
        s_s[...] = jnp.concatenate(_split3(jnp.sin(ang)), axis=1)

    src = lax.broadcasted_iota(jnp.int32, (3 * LANES, LANES), 0) & (LANES - 1)
    dst = lax.broadcasted_iota(jnp.int32, (3 * LANES, LANES), 1)
    spread = (src == p * ROPE_FREQS + (dst & (ROPE_FREQS - 1))).astype(BF16)
    c = jnp.dot(c_s[...], spread, preferred_element_type=F32)
    s = jnp.dot(s_s[...], spread, preferred_element_type=F32)
    lane = lax.broadcasted_iota(jnp.int32, c.shape, 1)
    is_x1 = lane < ROT_DIM
    is_x2 = (lane >= ATT_HD) & (lane < ATT_HD + ROT_DIM)
    cos_ref[...] = jnp.where(is_x1 | is_x2, c, 1.0)
    sin_ref[...] = jnp.where(is_x1, -s, jnp.where(is_x2, s, 0.0))


def _rope_tables(positions):
    t = positions.size
    n = t // ROPE_GROUPS
    pos_c = jnp.repeat(positions.reshape(ROPE_GROUPS, n).T, ROPE_FREQS, axis=1)
    inv = jnp.power(ROPE_THETA, -jnp.arange(0, ROT_DIM, 2, dtype=F32) / ROT_DIM)
    inv_lane = jnp.tile(inv, ROPE_GROUPS).reshape(1, LANES)
    out = jax.ShapeDtypeStruct((t, LANES), F32)
    return pl.pallas_call(
        _rope_table_kernel,
        grid=(ROPE_GROUPS,),
        in_specs=[pl.BlockSpec((n, LANES), lambda p: (0, 0)),
                  pl.BlockSpec((1, LANES), lambda p: (0, 0))],
        out_specs=[pl.BlockSpec((n, LANES), lambda p: (p, 0))] * 2,
        out_shape=[out] * 2,
        scratch_shapes=[pltpu.VMEM((n, 3 * LANES), BF16)] * 2,
        compiler_params=pltpu.CompilerParams(dimension_semantics=("arbitrary",)),
        name="rope_tables",
    )(pos_c, inv_lane)


def _inproj_kernel(x_ref, g_ref, w_ref, cos_ref, sin_ref, z_ref, u_ref, *, tm, tn):
    j = pl.program_id(1)
    rb = 128
    n_q = ATT_W // tn
    n_rope = 2 * ATT_W // tn

    def norm_rows(r0, n):
        for i in range(n // rb):
            rows = slice(r0 + i * rb, r0 + (i + 1) * rb)
            xb = x_ref[rows, :]
            ms = jnp.mean(xb * xb, axis=-1, keepdims=True)
            u_ref[rows, :] = ((xb * lax.rsqrt(ms + EPS)) * g_ref[...]).astype(BF16)

    def project_rotate(r0, n):
        rows = slice(r0, r0 + n)
        scale = jnp.where(j < n_q, ATT_HD ** -0.5 * LOG2E, 1.0).astype(F32)
        c = cos_ref[rows, :] * scale
        s = sin_ref[rows, :] * scale
        for cb in range(tn // INPROJ_ROPE_CHUNK):
            cols = slice(cb * INPROJ_ROPE_CHUNK, (cb + 1) * INPROJ_ROPE_CHUNK)
            zc = jnp.dot(u_ref[rows, :], w_ref[:, cols], preferred_element_type=F32)
            parts = [zc[:, b * LANES:(b + 1) * LANES] for b in range(INPROJ_ROPE_CHUNK // LANES)]
            z_ref[rows, cols] = jnp.concatenate(
                [blk * c + pltpu.roll(blk, LANES // 2, 1) * s for blk in parts], axis=1)

    @pl.when(j == 0)
    def _():
        for r0 in range(0, tm, tm // 2):
            norm_rows(r0, tm // 2)
            project_rotate(r0, tm // 2)

    @pl.when((j > 0) & (j < n_rope))
    def _():
        project_rotate(0, tm)

    @pl.when(j >= n_rope)
    def _():
        z_ref[...] = jnp.dot(u_ref[...], w_ref[...], preferred_element_type=F32)


def _inproj(x2, g_pre, w_in_bf16, cos_t, sin_t):
    t = x2.shape[0]
    tm = min(INPROJ_TM, t)
    tn = INPROJ_TN
    row = lambda i, j: (i, 0)
    return pl.pallas_call(
        functools.partial(_inproj_kernel, tm=tm, tn=tn),
        grid=(t // tm, IN_W // tn),
        in_specs=[pl.BlockSpec((tm, D_MODEL), row),
                  pl.BlockSpec((1, D_MODEL), lambda i, j: (0, 0)),
                  pl.BlockSpec((D_MODEL, tn), lambda i, j: (0, j)),
                  pl.BlockSpec((tm, LANES), row),
                  pl.BlockSpec((tm, LANES), row)],
        out_specs=pl.BlockSpec((tm, tn), lambda i, j: (i, j)),
        out_shape=jax.ShapeDtypeStruct((t, IN_W), F32),
        scratch_shapes=[pltpu.VMEM((tm, D_MODEL), BF16)],
        compiler_params=pltpu.CompilerParams(
            dimension_semantics=("parallel", "arbitrary"), vmem_limit_bytes=VMEM_LIMIT),
        name="inproj",
    )(x2, g_pre.reshape(1, D_MODEL), w_in_bf16, cos_t, sin_t)


def _att_geometry(seq):
    geo = {}
    for d in DILATIONS:
        n_cls = seq // d
        dense = n_cls <= ATT_DENSE_MAX
        geo[d] = (dense, n_cls if dense else ATT_QB, n_cls if dense else ATT_KB)
    return geo


def _att_kernel(q_ref, k_ref, v_ref, g_ref, y_ref, mt_s, lt_s, nt_s, lsed_s, od_s,
                bias_s, biasd_s, p_s, pd_s, *, seq):
    geo = _att_geometry(seq)
    tiled = [d for d in DILATIONS if not geo[d][0]]
    dense = [d for d in DILATIONS if geo[d][0]]
    n_tiles = seq // ATT_TILE
    ones_kv = {kb: jnp.ones((kb, LANES), BF16) for kb in {geo[d][2] for d in DILATIONS}}

    def lanes(rows):
        lane = lax.broadcasted_iota(jnp.int32, (rows, LANES), 1)
        return lane < ATT_HD, _head_a_lanes(lane)

    def band_bias(qb, kb, delta):
        qa = lax.broadcasted_iota(jnp.int32, (2 * qb, kb), 0) & (qb - 1)
        ka = lax.broadcasted_iota(jnp.int32, (2 * qb, kb), 1)
        return jnp.where(jnp.abs(ka - qa + delta) <= HALF, 0.0, NEG)

    for idx, delta in enumerate((-HALF, 0, -2 * HALF)):
        bias_s[idx] = band_bias(ATT_QB, ATT_KB, delta)
    for d in dense:
        biasd_s[...] = band_bias(geo[d][1], geo[d][2], 0)

    def rows_of(start, size, d):
        return pl.ds(start, size) if d == 1 else pl.ds(start, size, stride=d)

    def score_block(d, r, a0, bias, p_ref):
        _, qb, kb = geo[d]
        head_a, qk_head_a = lanes(qb)
        ka0 = 0 if geo[d][0] else jnp.clip(a0 - HALF, 0, seq // d - kb)
        q2 = q_ref[rows_of(r + d * a0, qb, d), :]
        kk = k_ref[rows_of(r + d * ka0, kb, d), :].astype(BF16)
        qs = jnp.concatenate([jnp.where(qk_head_a, q2, 0.0), jnp.where(qk_head_a, 0.0, q2)], axis=0)
        s = _bdot_nt(qs, kk) + bias
        m = jnp.max(s, axis=-1, keepdims=True)
        p_ref[...] = jnp.exp2(s - m).astype(BF16)
        mb = jnp.broadcast_to(m, (2 * qb, LANES))
        return jnp.where(head_a, mb[:qb], mb[qb:])

    def value_block(d, r, a0, p_ref):
        _, qb, kb = geo[d]
        head_a, _ = lanes(qb)
        ka0 = 0 if geo[d][0] else jnp.clip(a0 - HALF, 0, seq // d - kb)
        vv = v_ref[rows_of(r + d * ka0, kb, d), :].astype(BF16)
        nl = jnp.dot(p_ref[...], jnp.concatenate([vv, ones_kv[kb]], axis=1), preferred_element_type=F32)
        return (jnp.where(head_a, nl[:qb, LANES:], nl[qb:, LANES:]),
                jnp.where(head_a, nl[:qb, :LANES], nl[qb:, :LANES]))

    for gi, d in enumerate(dense):
        def dense_scores(r, c, gi=gi, d=d):
            lsed_s[gi * d + r] = score_block(d, r, 0, biasd_s[...], pd_s.at[r])
            return c

        def dense_values(r, c, gi=gi, d=d):
            l, n = value_block(d, r, 0, pd_s.at[r])
            od_s[gi * d + r] = (n / l).astype(BF16)
            lsed_s[gi * d + r] = lsed_s[gi * d + r] + jnp.log2(l)
            return c
        lax.fori_loop(0, d, dense_scores, 0, unroll=ATT_DENSE_UNROLL)
        lax.fori_loop(0, d, dense_values, 0, unroll=ATT_DENSE_UNROLL)

    cb = 256

    def to_natural(d):
        i = lax.broadcasted_iota(jnp.int32, (cb, cb), 0)
        j = lax.broadcasted_iota(jnp.int32, (cb, cb), 1)
        return (j == (i % d) * (cb // d) + i // d).astype(BF16)

    def tile_body(tj, carry):
        tile0 = tj * ATT_TILE
        for gi, d in enumerate(tiled):
            n_cls = seq // d
            per_cls = ATT_TILE // d // ATT_QB
            n_blk = ATT_TILE // ATT_QB
            for bi in range(n_blk):
                r = bi // per_cls
                a0 = tile0 // d + (bi % per_cls) * ATT_QB
                which = jnp.where(a0 == 0, 1, jnp.where(a0 == n_cls - ATT_QB, 2, 0))
                mt_s[gi, rows_of(r + d * a0 - tile0, ATT_QB, d), :] = score_block(
                    d, r, a0, bias_s[which], p_s.at[bi])
            for bi in range(n_blk):
                r = bi // per_cls
                a0 = tile0 // d + (bi % per_cls) * ATT_QB
                dst = rows_of(r + d * a0 - tile0, ATT_QB, d)
                lt_s[gi, dst, :], nt_s[gi, dst, :] = value_block(d, r, a0, p_s.at[bi])

        def comb_body(ci, c):
            rows = pl.ds(pl.multiple_of(ci * cb, cb), cb)
            orow = pl.ds(pl.multiple_of(tile0 + ci * cb, cb), cb)
            parts = [(mt_s[gi, rows, :], lt_s[gi, rows, :], nt_s[gi, rows, :]) for gi in range(len(tiled))]
            for gi, d in enumerate(dense):
                per = cb // d
                a0 = pl.multiple_of((tile0 + ci * cb) // d, per)
                o_cls = jnp.concatenate([od_s[gi * d + r, pl.ds(a0, per), :] for r in range(d)], axis=0)
                lse_cls = jnp.concatenate([lsed_s[gi * d + r, pl.ds(a0, per), :] for r in range(d)], axis=0)
                perm = to_natural(d)
                o_nat = jnp.dot(perm, o_cls, preferred_element_type=F32)
                hi = lse_cls.astype(BF16)
                lo = (lse_cls - hi.astype(F32)).astype(BF16)
                lse2 = jnp.dot(perm, jnp.concatenate([hi, lo], axis=1), preferred_element_type=F32)
                parts.append((lse2[:, :LANES] + lse2[:, LANES:], None, o_nat))
            mx = functools.reduce(jnp.maximum, [part[0] for part in parts])
            num = jnp.zeros((cb, LANES), F32)
            den = jnp.zeros((cb, LANES), F32)
            for m, l, n in parts:
                w = jnp.exp2(m - mx)
                num = num + w * n
                den = den + (w if l is None else w * l)
            y_ref[orow, :] = ((num / den) * _silu(g_ref[orow, :])).astype(y_ref.dtype)
            return c
        lax.fori_loop(0, ATT_TILE // cb, comb_body, 0, unroll=True)
        return carry

    lax.fori_loop(0, n_tiles, tile_body, 0)


def _attention(z, batch, seq):
    t = batch * seq
    n_pairs = ATT_W // LANES

    def col(c0):
        return pl.BlockSpec((seq, LANES), lambda b, h, c0=c0: (b, c0 + h))

    geo = _att_geometry(seq)
    dense_geo = sorted({geo[d][1:] for d in DILATIONS if geo[d][0]})
    assert len(dense_geo) <= 1, "one band-mask table serves the dense groups"
    dqb, dkb = dense_geo[0] if dense_geo else (8, LANES)
    n_dense = sum(geo[d][0] for d in DILATIONS)
    tiled = pltpu.VMEM((max(len(DILATIONS) - n_dense, 1), ATT_TILE, LANES), F32)
    n_dense_cls = max(sum(d for d in DILATIONS if geo[d][0]), 1)
    return pl.pallas_call(
        functools.partial(_att_kernel, seq=seq),
        grid=(batch, n_pairs),
        in_specs=[col(COL_AQ), col(COL_AK), col(COL_AV), col(COL_AG)],
        out_specs=pl.BlockSpec((seq, LANES), lambda b, h: (b, h)),
        out_shape=jax.ShapeDtypeStruct((t, ATT_W), BF16),
        scratch_shapes=[tiled, tiled, tiled,
                        pltpu.VMEM((n_dense_cls, dqb, LANES), F32),
                        pltpu.VMEM((n_dense_cls, dqb, LANES), BF16),
                        pltpu.VMEM((3, 2 * ATT_QB, ATT_KB), F32),
                        pltpu.VMEM((2 * dqb, dkb), F32),
                        pltpu.VMEM((ATT_TILE // ATT_QB, 2 * ATT_QB, ATT_KB), BF16),
                        pltpu.VMEM((max(DILATIONS), 2 * dqb, dkb), BF16)],
        compiler_params=pltpu.CompilerParams(
            dimension_semantics=("parallel", "parallel"), vmem_limit_bytes=VMEM_LIMIT),
        name="band_attention",
    )(z, z, z, z)


def _chunk_cumsum(g, tri):
    c = HG_CHUNK
    g1 = g.astype(BF16)
    g2 = (g - g1.astype(F32)).astype(BF16)
    terms = jnp.concatenate([g1, g2], axis=1)
    k = g.shape[1]
    outs = []
    for i in range(g.shape[0] // c):
        part = jnp.dot(tri, terms[i * c:(i + 1) * c, :], preferred_element_type=F32)
        outs.append(part[:, k:] + part[:, :k])
    return jnp.concatenate(outs, axis=0)


def _hgrn_kernel(hq_ref, ff_ref, fb_ref, hv_ref, hg_ref, lbf_ref, lbb_ref, ghg_ref, y_ref,
                 q_s, kf_s, kb_s, bf_s, bb_s, o_s,
                 qtf_s, ktf_s, qtb_s, ktb_s, v16_s,
                 emf_s, emb_s, stf_s, stb_s, *, seq, layer):
    C = HG_CHUNK
    n_chunks = seq // C

    def lower_bound(ref):
        a = ref[...]
        e = jnp.exp(a - jnp.max(a, axis=0, keepdims=True))
        return jnp.sum(e[0:layer + 1, :], axis=0, keepdims=True) / jnp.sum(e, axis=0, keepdims=True)

    lbf = lower_bound(lbf_ref)
    lbb = lower_bound(lbb_ref)

    ti = lax.broadcasted_iota(jnp.int32, (C, C), 0)
    si = lax.broadcasted_iota(jnp.int32, (C, C), 1)
    tri_f = si <= ti
    tri_b = si >= ti

    pr = HG_PRE_ROWS

    cpb = pr // C
    fwd = (ff_ref, lbf, kf_s, bf_s, qtf_s, ktf_s, emf_s, False)
    bwd = (fb_ref, lbb, kb_s, bb_s, qtb_s, ktb_s, emb_s, True)

    def per_chunk_rows(x3):
        return jnp.broadcast_to(x3, (cpb, C, HG_DK)).reshape(pr, HG_DK)

    def pre_body(i, lo):
        rows = pl.ds(pl.multiple_of(i * pr, pr), pr)
        q = _silu(hq_ref[rows, :])
        q_s[rows, :] = q
        o_s[rows, :] = jnp.zeros((pr, HG_DK), F32)
        v16_s[rows, :] = hv_ref[rows, :].astype(BF16)
        for f_ref, lb, k_s, b_s, qt_s, kt_s, em_s, reverse in (fwd, bwd):
            f = lb + (1.0 - lb) * _sigmoid(f_ref[rows, :])
            kk = 1.0 - f
            tri = tri_b if reverse else tri_f
            b = _chunk_cumsum(jnp.log(f), tri.astype(BF16))
            last = 0 if reverse else C - 1
            btot = b.reshape(cpb, C, HG_DK)[:, last:last + 1, :]
            mid = 0.5 * btot
            midr = per_chunk_rows(mid)
            k_s[rows, :] = kk
            b_s[rows, :] = b
            qt_s[rows, :] = (q * jnp.exp(b - midr)).astype(BF16)
            kt_s[rows, :] = (kk * jnp.exp(midr - b)).astype(BF16)
            em_s[pl.ds(pl.multiple_of(i * cpb, cpb), cpb), :] = jnp.exp(mid).reshape(cpb, HG_DK)
            lo = jnp.minimum(lo, btot.reshape(cpb, HG_DK))
        return lo

    lo = lax.fori_loop(0, seq // pr, pre_body, jnp.zeros((cpb, HG_DK), F32))
    fast = jnp.min(lo) >= HG_FAST_MIN_LOGDECAY

    stf_s[...] = jnp.zeros_like(stf_s)
    stb_s[...] = jnp.zeros_like(stb_s)

    def chunk(ci, tri, d, st_ref, exact):
        _, _, k_s, b_s, qt_s, kt_s, em_s, reverse = d
        r0 = pl.multiple_of(ci * C, C)
        rows = pl.ds(r0, C)
        v = v16_s[rows, :]
        st = st_ref[...]
        if exact:
            q = q_s[rows, :]
            b = b_s[rows, :]
            kk = k_s[rows, :]
            btot = b_s[pl.ds(r0 + (0 if reverse else C - 1), 1), :]

            def col_body(s, att):
                w = q * k_s[pl.ds(r0 + s, 1), :] * jnp.exp(jnp.minimum(b - b_s[pl.ds(r0 + s, 1), :], 0.0))
                return att + jnp.where(si == s, jnp.sum(w, axis=-1, keepdims=True), 0.0)
            att = lax.fori_loop(0, C, col_body, jnp.zeros((C, C), F32))
            o = _bdot(jnp.where(tri, att, 0.0), v) + _bdot_nt(q * jnp.exp(b), st)
            st_ref[...] = st * jnp.exp(btot) + _bdot_tn(v, kk * jnp.exp(btot - b))
        else:
            em = em_s[pl.ds(ci, 1), :]
            qt = qt_s[rows, :]
            kt = kt_s[rows, :]
            att = jnp.where(tri, _bdot_nt(qt, kt), 0.0)
            o = _bdot(att, v) + _bdot_nt(qt, st * em)
            st_ref[...] = st * (em * em) + _bdot_tn(v, kt) * em
        o_s[rows, :] = o_s[rows, :] + o

    def scan(exact):
        def body(n, carry):
            chunk(n, tri_f, fwd, stf_s, exact)
            chunk(n_chunks - 1 - n, tri_b, bwd, stb_s, exact)
            return carry
        lax.fori_loop(0, n_chunks, body, 0, unroll=1 if exact else HG_UNROLL)

    lax.cond(fast, lambda: scan(False), lambda: scan(True))

    def post_body(i, carry):
        rows = pl.ds(pl.multiple_of(i * pr, pr), pr)
        o = o_s[rows, :]
        ms = jnp.mean(o * o, axis=-1, keepdims=True)
        y = (o * lax.rsqrt(ms + EPS)) * ghg_ref[...]
        y_ref[rows, :] = (y * _silu(hg_ref[rows, :])).astype(y_ref.dtype)
        return carry
    lax.fori_loop(0, seq // pr, post_body, 0)


def _hgrn(z, lb_fwd, lb_bwd, g_hg, batch, seq, layer):
    t = batch * seq

    def col(c0):
        return pl.BlockSpec((seq, LANES), lambda b, h, c0=c0: (b, c0 + h))

    n_lb = lb_fwd.shape[0]
    lb_spec = pl.BlockSpec((n_lb, HG_DK), lambda b, h: (0, h))
    seq_f32 = pltpu.VMEM((seq, HG_DK), F32)
    seq_bf16 = pltpu.VMEM((seq, HG_DK), BF16)
    chunk_f32 = pltpu.VMEM((seq // HG_CHUNK, HG_DK), F32)
    state = pltpu.VMEM((HG_DK, HG_DK), F32)
    return pl.pallas_call(
        functools.partial(_hgrn_kernel, seq=seq, layer=layer),
        grid=(batch, HG_HEADS),
        in_specs=[col(COL_HQ), col(COL_HFF), col(COL_HFB), col(COL_HV), col(COL_HG),
                  lb_spec, lb_spec, pl.BlockSpec((1, HG_DK), lambda b, h: (0, 0))],
        out_specs=pl.BlockSpec((seq, HG_DK), lambda b, h: (b, h)),
        out_shape=jax.ShapeDtypeStruct((t, HG_W), BF16),
        scratch_shapes=[seq_f32] * 6 + [seq_bf16] * 5 + [chunk_f32, chunk_f32, state, state],
        compiler_params=pltpu.CompilerParams(
            dimension_semantics=("parallel", "parallel"), vmem_limit_bytes=VMEM_LIMIT),
        name="hgrn2",
    )(z, z, z, z, z, lb_fwd, lb_bwd, g_hg.reshape(1, HG_DK))


def _rms(y, g):
    ms = jnp.mean(y * y, axis=-1, keepdims=True)
    return (y * lax.rsqrt(ms + EPS)) * g


def _outproj_kernel(x_ref, ya_ref, yh_ref, p_ref, wo_ref, wpg_ref, wpp_ref, gpost_ref, gple_ref, o_ref):
    sub = x_ref.shape[0] // OUT_SUBBLOCKS
    for r in range(OUT_SUBBLOCKS):
        rows = slice(r * sub, (r + 1) * sub)
        y = (jnp.dot(ya_ref[rows, :], wo_ref[0:ATT_W, :], preferred_element_type=F32)
             + jnp.dot(yh_ref[rows, :], wo_ref[ATT_W:, :], preferred_element_type=F32))
        h1 = x_ref[rows, :] + _rms(y, gpost_ref[...])
        gate = jax.nn.sigmoid(jnp.dot(h1.astype(BF16), wpg_ref[...], preferred_element_type=F32))
        e = jnp.dot(p_ref[rows, :].astype(BF16), wpp_ref[...], preferred_element_type=F32)
        o_ref[rows, :] = h1 + _rms(gate * e, gple_ref[...])


def _outproj(x2, y_att, y_hg, p2, w_out, w_pg, w_pp, g_post, g_ple):
    t = x2.shape[0]
    tm = min(OUT_TM, t)
    row = lambda i: (i, 0)
    const = lambda i: (0, 0)
    resident = dict(pipeline_mode=pl.Buffered(1))
    return pl.pallas_call(
        _outproj_kernel,
        grid=(t // tm,),
        in_specs=[pl.BlockSpec((tm, D_MODEL), row),
                  pl.BlockSpec((tm, ATT_W), row),
                  pl.BlockSpec((tm, HG_W), row),
                  pl.BlockSpec((tm, PLE_DIM), row),
                  pl.BlockSpec((ATT_W + HG_W, D_MODEL), const, **resident),
                  pl.BlockSpec((D_MODEL, D_MODEL), const, **resident),
                  pl.BlockSpec((PLE_DIM, D_MODEL), const, **resident),
                  pl.BlockSpec((1, D_MODEL), const),
                  pl.BlockSpec((1, D_MODEL), const)],
        out_specs=pl.BlockSpec((tm, D_MODEL), row),
        out_shape=jax.ShapeDtypeStruct((t, D_MODEL), F32),
        compiler_params=pltpu.CompilerParams(
            dimension_semantics=("parallel",), vmem_limit_bytes=VMEM_LIMIT),
        name="outproj_ple",
    )(x2, y_att, y_hg, p2, w_out, w_pg, w_pp, g_post.reshape(1, D_MODEL), g_ple.reshape(1, D_MODEL))


def kernel(x, p, positions, w_in, w_out, g_pre, g_post, g_hg, lb_fwd, lb_bwd, w_pg, w_pp, g_ple):
    depth = w_in.shape[0]
    batch, seq, _ = x.shape
    assert seq % ATT_TILE == 0 and seq // max(DILATIONS) >= ATT_KB
    t = batch * seq
    cos_t, sin_t = _rope_tables(positions)
    h = x.reshape(t, D_MODEL)
    for i in range(depth):
        z = _inproj(h, g_pre[i], _prep_w_in(w_in[i]), cos_t, sin_t)
        y_att = _attention(z, batch, seq)
        y_hg = _hgrn(z, lb_fwd, lb_bwd, g_hg[i], batch, seq, i)
        h = _outproj(h, y_att, y_hg, p[i].reshape(t, PLE_DIM), w_out[i].astype(BF16),
                     w_pg[i].astype(BF16), w_pp[i].astype(BF16), g_post[i], g_ple[i])
    return h.reshape(batch, seq, D_MODEL)
```

```python
import functools

import jax
import jax.numpy as jnp
from jax import lax
from jax.experimental import pallas as pl
from jax.experimental.pallas import tpu as pltpu

F32 = jnp.float32
BF16 = jnp.bfloat16

D_MODEL = 2048
PLE_DIM = 256
ATT_HEADS = 16
ATT_HD = 64
ATT_W = ATT_HEADS * ATT_HD
ROT_DIM = ATT_HD // 4
ROPE_THETA = 500000.0
DILATION_CFG = ((128, 1), (512, 4), (2048, 16))
DILATIONS = tuple(d for _, d in DILATION_CFG)
HALF = (DILATION_CFG[0][0] // 2) // DILATION_CFG[0][1]
assert all((w // 2) // d == HALF for w, d in DILATION_CFG)
HG_HEADS = 8
HG_DK = 128
HG_W = HG_HEADS * HG_DK
HG_CHUNK = 128
IN_W = 4 * ATT_W + 5 * HG_W
EPS = 1e-6
NEG = -1e30
LOG2E = 1.4426950408889634

LANES = 128
VMEM_LIMIT = 56 * 1024 * 1024

COL_AQ, COL_AK, COL_AV, COL_AG = (i * ATT_W // LANES for i in range(4))
COL_HQ, COL_HFF, COL_HFB, COL_HV, COL_HG = (4 * ATT_W // LANES + i * HG_W // LANES for i in range(5))

INPROJ_PREP_TN = 512
INPROJ_TM = 1024
INPROJ_TN = 1024
INPROJ_ROPE_CHUNK = 256
ATT_TILE = 1024
ATT_QB = 2 * HALF
ATT_KB = ATT_QB + 2 * HALF
ATT_DENSE_MAX = 256
ATT_DENSE_UNROLL = 8
HG_UNROLL = 16
HG_PRE_ROWS = 512
OUT_TM = 512
OUT_SUBBLOCKS = 2
HG_FAST_MIN_LOGDECAY = -120.0


def _bdot(a, b):
    return jnp.dot(a.astype(BF16), b.astype(BF16), preferred_element_type=F32)


def _bdot_nt(a, b):
    return lax.dot_general(a.astype(BF16), b.astype(BF16), (((1,), (1,)), ((), ())),
                           preferred_element_type=F32)


def _bdot_tn(a, b):
    return lax.dot_general(a.astype(BF16), b.astype(BF16), (((0,), (0,)), ((), ())),
                           preferred_element_type=F32)


def _sigmoid(x):
    return 0.5 * jnp.tanh(0.5 * x) + 0.5


def _silu(x):
    return x * _sigmoid(x)


def _pair_swap_matrix():
    h = ROT_DIM // 2
    src = lax.broadcasted_iota(jnp.int32, (LANES, LANES), 0)
    dst = lax.broadcasted_iota(jnp.int32, (LANES, LANES), 1)
    shift = ATT_HD - h
    sigma = jnp.where((dst >= h) & (dst < ROT_DIM), dst + shift,
                      jnp.where((dst >= ATT_HD) & (dst < ATT_HD + h), dst - shift, dst))
    return (src == sigma).astype(BF16)


def _cast_weight_tile(w_ref, o_refs, j, swap_tiles, rb=256):
    n_rows, tn = w_ref.shape

    @pl.when(j < swap_tiles)
    def _():
        perm = _pair_swap_matrix()

        def body(i, carry):
            rows = pl.ds(pl.multiple_of(i * rb, rb), rb)
            for cb in range(tn // LANES):
                cols = slice(cb * LANES, (cb + 1) * LANES)
                wb = jnp.dot(w_ref[rows, cols].astype(BF16), perm, preferred_element_type=F32).astype(BF16)
                for o_ref in o_refs:
                    o_ref[rows, cols] = wb
            return carry
        lax.fori_loop(0, n_rows // rb, body, 0)

    @pl.when(j >= swap_tiles)
    def _():
        def body(i, carry):
            rows = pl.ds(pl.multiple_of(i * rb, rb), rb)
            wb = w_ref[rows, :].astype(BF16)
            for o_ref in o_refs:
                o_ref[rows, :] = wb
            return carry
        lax.fori_loop(0, n_rows // rb, body, 0)


def _head_a_lanes(lane):
    h = ROT_DIM // 2
    return (lane < h) | ((lane >= ROT_DIM) & (lane < ATT_HD + h))


ROPE_FREQS = ROT_DIM // 2
ROPE_GROUPS = LANES // ROPE_FREQS


def _split3(x):
    t1 = x.astype(BF16)
    r1 = x - t1.astype(F32)
    t2 = r1.astype(BF16)
    return t1, t2, (r1 - t2.astype(F32)).astype(BF16)


def _rope_table_kernel(pos_ref, inv_ref, cos_ref, sin_ref, c_s, s_s):
    p = pl.program_id(0)

    @pl.when(p == 0)
    def _():
        ang = pos_ref[...].astype(F32) * inv_ref[...]
        c_s[...] = jnp.concatenate(_split3(jnp.cos(ang)), axis=1)
        s_s[...] = jnp.concatenate(_split3(jnp.sin(ang)), axis=1)

    src = lax.broadcasted_iota(jnp.int32, (3 * LANES, LANES), 0) & (LANES - 1)
    dst = lax.broadcasted_iota(jnp.int32, (3 * LANES, LANES), 1)
    spread = (src == p * ROPE_FREQS + (dst & (ROPE_FREQS - 1))).astype(BF16)
    c = jnp.dot(c_s[...], spread, preferred_element_type=F32)
    s = jnp.dot(s_s[...], spread, preferred_element_type=F32)
    lane = lax.broadcasted_iota(jnp.int32, c.shape, 1)
    is_x1 = lane < ROT_DIM
    is_x2 = (lane >= ATT_HD) & (lane < ATT_HD + ROT_DIM)
    cos_ref[...] = jnp.where(is_x1 | is_x2, c, 1.0)
    sin_ref[...] = jnp.where(is_x1, -s, jnp.where(is_x2, s, 0.0))


def _rope_tables(positions):
    t = positions.size
    n = t // ROPE_GROUPS
    pos_c = jnp.repeat(positions.reshape(ROPE_GROUPS, n).T, ROPE_FREQS, axis=1)
    inv = jnp.power(ROPE_THETA, -jnp.arange(0, ROT_DIM, 2, dtype=F32) / ROT_DIM)
    inv_lane = jnp.tile(inv, ROPE_GROUPS).reshape(1, LANES)
    out = jax.ShapeDtypeStruct((t, LANES), F32)
    return pl.pallas_call(
        _rope_table_kernel,
        grid=(ROPE_GROUPS,),
        in_specs=[pl.BlockSpec((n, LANES), lambda p: (0, 0)),
                  pl.BlockSpec((1, LANES), lambda p: (0, 0))],
        out_specs=[pl.BlockSpec((n, LANES), lambda p: (p, 0))] * 2,
        out_shape=[out] * 2,
        scratch_shapes=[pltpu.VMEM((n, 3 * LANES), BF16)] * 2,
        compiler_params=pltpu.CompilerParams(dimension_semantics=("arbitrary",)),
        name="rope_tables",
    )(pos_c, inv_lane)


def _inproj_kernel(*refs, tm, tn, prepare_weights):
    j = pl.program_id(1)
    rb = 128
    n_q = ATT_W // tn
    n_rope = 2 * ATT_W // tn
    if prepare_weights:
        x_ref, g_ref, w32_ref, cos_ref, sin_ref, z_ref, wout_ref, u_ref, w_ref = refs
        _cast_weight_tile(w32_ref, (w_ref, wout_ref), j, n_rope)
    else:
        x_ref, g_ref, w_ref, cos_ref, sin_ref, _, z_ref, u_ref = refs

    def norm_rows(r0, n):
        for i in range(n // rb):
            rows = slice(r0 + i * rb, r0 + (i + 1) * rb)
            xb = x_ref[rows, :]
            ms = jnp.mean(xb * xb, axis=-1, keepdims=True)
            u_ref[rows, :] = ((xb * lax.rsqrt(ms + EPS)) * g_ref[...]).astype(BF16)

    def project_rotate(r0, n):
        rows = slice(r0, r0 + n)
        scale = jnp.where(j < n_q, ATT_HD ** -0.5 * LOG2E, 1.0).astype(F32)
        c = cos_ref[rows, :] * scale
        s = sin_ref[rows, :] * scale
        for cb in range(tn // INPROJ_ROPE_CHUNK):
            cols = slice(cb * INPROJ_ROPE_CHUNK, (cb + 1) * INPROJ_ROPE_CHUNK)
            zc = jnp.dot(u_ref[rows, :], w_ref[:, cols], preferred_element_type=F32)
            parts = [zc[:, b * LANES:(b + 1) * LANES] for b in range(INPROJ_ROPE_CHUNK // LANES)]
            z_ref[rows, cols] = jnp.concatenate(
                [blk * c + pltpu.roll(blk, LANES // 2, 1) * s for blk in parts], axis=1)

    @pl.when(j == 0)
    def _():
        for r0 in range(0, tm, tm // 2):
            norm_rows(r0, tm // 2)
            project_rotate(r0, tm // 2)

    @pl.when((j > 0) & (j < n_rope))
    def _():
        project_rotate(0, tm)

    @pl.when(j >= n_rope)
    def _():
        z_ref[...] = jnp.dot(u_ref[...], w_ref[...], preferred_element_type=F32)


def _inproj(x2, g_pre, w_in, cos_t, sin_t):
    t = x2.shape[0]
    tm = min(INPROJ_TM, t)
    g2 = g_pre.reshape(1, D_MODEL)
    z_shape = jax.ShapeDtypeStruct((t, IN_W), F32)
    params = pltpu.CompilerParams(dimension_semantics=("parallel", "arbitrary"), vmem_limit_bytes=VMEM_LIMIT)
    u_scratch = pltpu.VMEM((tm, D_MODEL), BF16)

    def specs(tn, first):
        row = lambda i, j: (i + first, 0)
        return ([pl.BlockSpec((tm, D_MODEL), row),
                 pl.BlockSpec((1, D_MODEL), lambda i, j: (0, 0)),
                 pl.BlockSpec((D_MODEL, tn), lambda i, j: (0, j)),
                 pl.BlockSpec((tm, LANES), row),
                 pl.BlockSpec((tm, LANES), row)],
                pl.BlockSpec((tm, tn), lambda i, j: (i + first, j)))

    tn = INPROJ_PREP_TN
    in_specs, z_spec = specs(tn, 0)
    z, w_bf16 = pl.pallas_call(
        functools.partial(_inproj_kernel, tm=tm, tn=tn, prepare_weights=True),
        grid=(1, IN_W // tn),
        in_specs=in_specs,
        out_specs=[z_spec, pl.BlockSpec((D_MODEL, tn), lambda i, j: (0, j))],
        out_shape=[z_shape, jax.ShapeDtypeStruct((D_MODEL, IN_W), BF16)],
        scratch_shapes=[u_scratch, pltpu.VMEM((D_MODEL, tn), BF16)],
        compiler_params=params,
        name="inproj_first",
    )(x2, g2, w_in, cos_t, sin_t)
    if t == tm:
        return z
    tn = INPROJ_TN
    in_specs, z_spec = specs(tn, 1)
    return pl.pallas_call(
        functools.partial(_inproj_kernel, tm=tm, tn=tn, prepare_weights=False),
        grid=(t // tm - 1, IN_W // tn),
        in_specs=in_specs + [pl.BlockSpec(memory_space=pl.ANY)],
        out_specs=z_spec,
        out_shape=z_shape,
        input_output_aliases={len(in_specs): 0},
        scratch_shapes=[u_scratch],
        compiler_params=params,
        name="inproj",
    )(x2, g2, w_bf16, cos_t, sin_t, z)


def _att_geometry(seq):
    geo = {}
    for d in DILATIONS:
        n_cls = seq // d
        dense = n_cls <= ATT_DENSE_MAX
        geo[d] = (dense, n_cls if dense else ATT_QB, n_cls if dense else ATT_KB)
    return geo


def _att_kernel(q_ref, k_ref, v_ref, g_ref, y_ref, mt_s, lt_s, nt_s, lsed_s, od_s,
                bias_s, biasd_s, p_s, pd_s, *, seq):
    geo = _att_geometry(seq)
    tiled = [d for d in DILATIONS if not geo[d][0]]
    dense = [d for d in DILATIONS if geo[d][0]]
    n_tiles = seq // ATT_TILE
    ones_kv = {kb: jnp.ones((kb, LANES), BF16) for kb in {geo[d][2] for d in DILATIONS}}

    def lanes(rows):
        lane = lax.broadcasted_iota(jnp.int32, (rows, LANES), 1)
        return lane < ATT_HD, _head_a_lanes(lane)

    def band_bias(qb, kb, delta):
        qa = lax.broadcasted_iota(jnp.int32, (2 * qb, kb), 0) & (qb - 1)
        ka = lax.broadcasted_iota(jnp.int32, (2 * qb, kb), 1)
        return jnp.where(jnp.abs(ka - qa + delta) <= HALF, 0.0, NEG)

    for idx, delta in enumerate((-HALF, 0, -2 * HALF)):
        bias_s[idx] = band_bias(ATT_QB, ATT_KB, delta)
    for d in dense:
        biasd_s[...] = band_bias(geo[d][1], geo[d][2], 0)

    def rows_of(start, size, d):
        return pl.ds(start, size) if d == 1 else pl.ds(start, size, stride=d)

    def score_block(d, r, a0, bias, p_ref):
        _, qb, kb = geo[d]
        head_a, qk_head_a = lanes(qb)
        ka0 = 0 if geo[d][0] else jnp.clip(a0 - HALF, 0, seq // d - kb)
        q2 = q_ref[rows_of(r + d * a0, qb, d), :]
        kk = k_ref[rows_of(r + d * ka0, kb, d), :].astype(BF16)
        qs = jnp.concatenate([jnp.where(qk_head_a, q2, 0.0), jnp.where(qk_head_a, 0.0, q2)], axis=0)
        s = _bdot_nt(qs, kk) + bias
        m = jnp.max(s, axis=-1, keepdims=True)
        p_ref[...] = jnp.exp2(s - m).astype(BF16)
        mb = jnp.broadcast_to(m, (2 * qb, LANES))
        return jnp.where(head_a, mb[:qb], mb[qb:])

    def value_block(d, r, a0, p_ref):
        _, qb, kb = geo[d]
        head_a, _ = lanes(qb)
        ka0 = 0 if geo[d][0] else jnp.clip(a0 - HALF, 0, seq // d - kb)
        vv = v_ref[rows_of(r + d * ka0, kb, d), :].astype(BF16)
        nl = jnp.dot(p_ref[...], jnp.concatenate([vv, ones_kv[kb]], axis=1), preferred_element_type=F32)
        return (jnp.where(head_a, nl[:qb, LANES:], nl[qb:, LANES:]),
                jnp.where(head_a, nl[:qb, :LANES], nl[qb:, :LANES]))

    for gi, d in enumerate(dense):
        def dense_scores(r, c, gi=gi, d=d):
            lsed_s[gi * d + r] = score_block(d, r, 0, biasd_s[...], pd_s.at[r])
            return c

        def dense_values(r, c, gi=gi, d=d):
            l, n = value_block(d, r, 0, pd_s.at[r])
            od_s[gi * d + r] = (n / l).astype(BF16)
            lsed_s[gi * d + r] = lsed_s[gi * d + r] + jnp.log2(l)
            return c
        lax.fori_loop(0, d, dense_scores, 0, unroll=ATT_DENSE_UNROLL)
        lax.fori_loop(0, d, dense_values, 0, unroll=ATT_DENSE_UNROLL)

    cb = 256

    def to_natural(d):
        i = lax.broadcasted_iota(jnp.int32, (cb, cb), 0)
        j = lax.broadcasted_iota(jnp.int32, (cb, cb), 1)
        return (j == (i % d) * (cb // d) + i // d).astype(BF16)

    def tile_body(tj, carry):
        tile0 = tj * ATT_TILE
        for gi, d in enumerate(tiled):
            n_cls = seq // d
            per_cls = ATT_TILE // d // ATT_QB
            n_blk = ATT_TILE // ATT_QB
            for bi in range(n_blk):
                r = bi // per_cls
                a0 = tile0 // d + (bi % per_cls) * ATT_QB
                which = jnp.where(a0 == 0, 1, jnp.where(a0 == n_cls - ATT_QB, 2, 0))
                mt_s[gi, rows_of(r + d * a0 - tile0, ATT_QB, d), :] = score_block(
                    d, r, a0, bias_s[which], p_s.at[bi])
            for bi in range(n_blk):
                r = bi // per_cls
                a0 = tile0 // d + (bi % per_cls) * ATT_QB
                dst = rows_of(r + d * a0 - tile0, ATT_QB, d)
                lt_s[gi, dst, :], nt_s[gi, dst, :] = value_block(d, r, a0, p_s.at[bi])

        def comb_body(ci, c):
            rows = pl.ds(pl.multiple_of(ci * cb, cb), cb)
            orow = pl.ds(pl.multiple_of(tile0 + ci * cb, cb), cb)
            parts = [(mt_s[gi, rows, :], lt_s[gi, rows, :], nt_s[gi, rows, :]) for gi in range(len(tiled))]
            for gi, d in enumerate(dense):
                per = cb // d
                a0 = pl.multiple_of((tile0 + ci * cb) // d, per)
                o_cls = jnp.concatenate([od_s[gi * d + r, pl.ds(a0, per), :] for r in range(d)], axis=0)
                lse_cls = jnp.concatenate([lsed_s[gi * d + r, pl.ds(a0, per), :] for r in range(d)], axis=0)
                perm = to_natural(d)
                o_nat = jnp.dot(perm, o_cls, preferred_element_type=F32)
                hi = lse_cls.astype(BF16)
                lo = (lse_cls - hi.astype(F32)).astype(BF16)
                lse2 = jnp.dot(perm, jnp.concatenate([hi, lo], axis=1), preferred_element_type=F32)
                parts.append((lse2[:, :LANES] + lse2[:, LANES:], None, o_nat))
            mx = functools.reduce(jnp.maximum, [part[0] for part in parts])
            num = jnp.zeros((cb, LANES), F32)
            den = jnp.zeros((cb, LANES), F32)
            for m, l, n in parts:
                w = jnp.exp2(m - mx)
                num = num + w * n
                den = den + (w if l is None else w * l)
            y_ref[orow, :] = ((num / den) * _silu(g_ref[orow, :])).astype(y_ref.dtype)
            return c
        lax.fori_loop(0, ATT_TILE // cb, comb_body, 0, unroll=True)
        return carry

    lax.fori_loop(0, n_tiles, tile_body, 0)


def _attention(z, batch, seq):
    t = batch * seq
    n_pairs = ATT_W // LANES

    def col(c0):
        return pl.BlockSpec((seq, LANES), lambda b, h, c0=c0: (b, c0 + h))

    geo = _att_geometry(seq)
    dense_geo = sorted({geo[d][1:] for d in DILATIONS if geo[d][0]})
    assert len(dense_geo) <= 1, "one band-mask table serves the dense groups"
    dqb, dkb = dense_geo[0] if dense_geo else (8, LANES)
    n_dense = sum(geo[d][0] for d in DILATIONS)
    tiled = pltpu.VMEM((max(len(DILATIONS) - n_dense, 1), ATT_TILE, LANES), F32)
    n_dense_cls = max(sum(d for d in DILATIONS if geo[d][0]), 1)
    return pl.pallas_call(
        functools.partial(_att_kernel, seq=seq),
        grid=(batch, n_pairs),
        in_specs=[col(COL_AQ), col(COL_AK), col(COL_AV), col(COL_AG)],
        out_specs=pl.BlockSpec((seq, LANES), lambda b, h: (b, h)),
        out_shape=jax.ShapeDtypeStruct((t, ATT_W), BF16),
        scratch_shapes=[tiled, tiled, tiled,
                        pltpu.VMEM((n_dense_cls, dqb, LANES), F32),
                        pltpu.VMEM((n_dense_cls, dqb, LANES), BF16),
                        pltpu.VMEM((3, 2 * ATT_QB, ATT_KB), F32),
                        pltpu.VMEM((2 * dqb, dkb), F32),
                        pltpu.VMEM((ATT_TILE // ATT_QB, 2 * ATT_QB, ATT_KB), BF16),
                        pltpu.VMEM((max(DILATIONS), 2 * dqb, dkb), BF16)],
        compiler_params=pltpu.CompilerParams(
            dimension_semantics=("parallel", "parallel"), vmem_limit_bytes=VMEM_LIMIT),
        name="band_attention",
    )(z, z, z, z)


def _chunk_cumsum(g, tri):
    c = HG_CHUNK
    g1 = g.astype(BF16)
    g2 = (g - g1.astype(F32)).astype(BF16)
    terms = jnp.concatenate([g1, g2], axis=1)
    k = g.shape[1]
    outs = []
    for i in range(g.shape[0] // c):
        part = jnp.dot(tri, terms[i * c:(i + 1) * c, :], preferred_element_type=F32)
        outs.append(part[:, k:] + part[:, :k])
    return jnp.concatenate(outs, axis=0)


def _hgrn_kernel(hq_ref, ff_ref, fb_ref, hv_ref, hg_ref, lbf_ref, lbb_ref, ghg_ref, y_ref,
                 q_s, kf_s, kb_s, bf_s, bb_s, o_s,
                 qtf_s, ktf_s, qtb_s, ktb_s, v16_s,
                 emf_s, emb_s, stf_s, stb_s, *, seq, layer):
    C = HG_CHUNK
    n_chunks = seq // C

    def lower_bound(ref):
        a = ref[...]
        e = jnp.exp(a - jnp.max(a, axis=0, keepdims=True))
        return jnp.sum(e[0:layer + 1, :], axis=0, keepdims=True) / jnp.sum(e, axis=0, keepdims=True)

    lbf = lower_bound(lbf_ref)
    lbb = lower_bound(lbb_ref)

    ti = lax.broadcasted_iota(jnp.int32, (C, C), 0)
    si = lax.broadcasted_iota(jnp.int32, (C, C), 1)
    tri_f = si <= ti
    tri_b = si >= ti

    pr = HG_PRE_ROWS

    cpb = pr // C
    fwd = (ff_ref, lbf, kf_s, bf_s, qtf_s, ktf_s, emf_s, False)
    bwd = (fb_ref, lbb, kb_s, bb_s, qtb_s, ktb_s, emb_s, True)

    def per_chunk_rows(x3):
        return jnp.broadcast_to(x3, (cpb, C, HG_DK)).reshape(pr, HG_DK)

    def pre_body(i, lo):
        rows = pl.ds(pl.multiple_of(i * pr, pr), pr)
        q = _silu(hq_ref[rows, :])
        q_s[rows, :] = q
        o_s[rows, :] = jnp.zeros((pr, HG_DK), F32)
        v16_s[rows, :] = hv_ref[rows, :].astype(BF16)
        for f_ref, lb, k_s, b_s, qt_s, kt_s, em_s, reverse in (fwd, bwd):
            f = lb + (1.0 - lb) * _sigmoid(f_ref[rows, :])
            kk = 1.0 - f
            tri = tri_b if reverse else tri_f
            b = _chunk_cumsum(jnp.log(f), tri.astype(BF16))
            last = 0 if reverse else C - 1
            btot = b.reshape(cpb, C, HG_DK)[:, last:last + 1, :]
            mid = 0.5 * btot
            midr = per_chunk_rows(mid)
            k_s[rows, :] = kk
            b_s[rows, :] = b
            qt_s[rows, :] = (q * jnp.exp(b - midr)).astype(BF16)
            kt_s[rows, :] = (kk * jnp.exp(midr - b)).astype(BF16)
            em_s[pl.ds(pl.multiple_of(i * cpb, cpb), cpb), :] = jnp.exp(mid).reshape(cpb, HG_DK)
            lo = jnp.minimum(lo, btot.reshape(cpb, HG_DK))
        return lo

    lo = lax.fori_loop(0, seq // pr, pre_body, jnp.zeros((cpb, HG_DK), F32))
    fast = jnp.min(lo) >= HG_FAST_MIN_LOGDECAY

    stf_s[...] = jnp.zeros_like(stf_s)
    stb_s[...] = jnp.zeros_like(stb_s)

    def chunk(ci, tri, d, st_ref, exact):
        _, _, k_s, b_s, qt_s, kt_s, em_s, reverse = d
        r0 = pl.multiple_of(ci * C, C)
        rows = pl.ds(r0, C)
        v = v16_s[rows, :]
        st = st_ref[...]
        if exact:
            q = q_s[rows, :]
            b = b_s[rows, :]
            kk = k_s[rows, :]
            btot = b_s[pl.ds(r0 + (0 if reverse else C - 1), 1), :]

            def col_body(s, att):
                w = q * k_s[pl.ds(r0 + s, 1), :] * jnp.exp(jnp.minimum(b - b_s[pl.ds(r0 + s, 1), :], 0.0))
                return att + jnp.where(si == s, jnp.sum(w, axis=-1, keepdims=True), 0.0)
            att = lax.fori_loop(0, C, col_body, jnp.zeros((C, C), F32))
            o = _bdot(jnp.where(tri, att, 0.0), v) + _bdot_nt(q * jnp.exp(b), st)
            st_ref[...] = st * jnp.exp(btot) + _bdot_tn(v, kk * jnp.exp(btot - b))
        else:
            em = em_s[pl.ds(ci, 1), :]
            qt = qt_s[rows, :]
            kt = kt_s[rows, :]
            att = jnp.where(tri, _bdot_nt(qt, kt), 0.0)
            o = _bdot(att, v) + _bdot_nt(qt, st * em)
            st_ref[...] = st * (em * em) + _bdot_tn(v, kt) * em
        o_s[rows, :] = o_s[rows, :] + o

    def scan(exact):
        def body(n, carry):
            chunk(n, tri_f, fwd, stf_s, exact)
            chunk(n_chunks - 1 - n, tri_b, bwd, stb_s, exact)
            return carry
        lax.fori_loop(0, n_chunks, body, 0, unroll=1 if exact else HG_UNROLL)

    lax.cond(fast, lambda: scan(False), lambda: scan(True))

    def post_body(i, carry):
        rows = pl.ds(pl.multiple_of(i * pr, pr), pr)
        o = o_s[rows, :]
        ms = jnp.mean(o * o, axis=-1, keepdims=True)
        y = (o * lax.rsqrt(ms + EPS)) * ghg_ref[...]
        y_ref[rows, :] = (y * _silu(hg_ref[rows, :])).astype(y_ref.dtype)
        return carry
    lax.fori_loop(0, seq // pr, post_body, 0)


def _hgrn(z, lb_fwd, lb_bwd, g_hg, batch, seq, layer):
    t = batch * seq

    def col(c0):
        return pl.BlockSpec((seq, LANES), lambda b, h, c0=c0: (b, c0 + h))

    n_lb = lb_fwd.shape[0]
    lb_spec = pl.BlockSpec((n_lb, HG_DK), lambda b, h: (0, h))
    seq_f32 = pltpu.VMEM((seq, HG_DK), F32)
    seq_bf16 = pltpu.VMEM((seq, HG_DK), BF16)
    chunk_f32 = pltpu.VMEM((seq // HG_CHUNK, HG_DK), F32)
    state = pltpu.VMEM((HG_DK, HG_DK), F32)
    return pl.pallas_call(
        functools.partial(_hgrn_kernel, seq=seq, layer=layer),
        grid=(batch, HG_HEADS),
        in_specs=[col(COL_HQ), col(COL_HFF), col(COL_HFB), col(COL_HV), col(COL_HG),
                  lb_spec, lb_spec, pl.BlockSpec((1, HG_DK), lambda b, h: (0, 0))],
        out_specs=pl.BlockSpec((seq, HG_DK), lambda b, h: (b, h)),
        out_shape=jax.ShapeDtypeStruct((t, HG_W), BF16),
        scratch_shapes=[seq_f32] * 6 + [seq_bf16] * 5 + [chunk_f32, chunk_f32, state, state],
        compiler_params=pltpu.CompilerParams(
            dimension_semantics=("parallel", "parallel"), vmem_limit_bytes=VMEM_LIMIT),
        name="hgrn2",
    )(z, z, z, z, z, lb_fwd, lb_bwd, g_hg.reshape(1, HG_DK))


def _rms(y, g):
    ms = jnp.mean(y * y, axis=-1, keepdims=True)
    return (y * lax.rsqrt(ms + EPS)) * g


def _outproj_kernel(x_ref, ya_ref, yh_ref, p_ref, wo_ref, wpg_ref, wpp_ref, gpost_ref, gple_ref, o_ref):
    sub = x_ref.shape[0] // OUT_SUBBLOCKS
    for r in range(OUT_SUBBLOCKS):
        rows = slice(r * sub, (r + 1) * sub)
        y = (jnp.dot(ya_ref[rows, :], wo_ref[0:ATT_W, :], preferred_element_type=F32)
             + jnp.dot(yh_ref[rows, :], wo_ref[ATT_W:, :], preferred_element_type=F32))
        h1 = x_ref[rows, :] + _rms(y, gpost_ref[...])
        gate = jax.nn.sigmoid(jnp.dot(h1.astype(BF16), wpg_ref[...], preferred_element_type=F32))
        e = jnp.dot(p_ref[rows, :].astype(BF16), wpp_ref[...], preferred_element_type=F32)
        o_ref[rows, :] = h1 + _rms(gate * e, gple_ref[...])


def _outproj(x2, y_att, y_hg, p2, w_out, w_pg, w_pp, g_post, g_ple):
    t = x2.shape[0]
    tm = min(OUT_TM, t)
    row = lambda i: (i, 0)
    const = lambda i: (0, 0)
    resident = dict(pipeline_mode=pl.Buffered(1))
    return pl.pallas_call(
        _outproj_kernel,
        grid=(t // tm,),
        in_specs=[pl.BlockSpec((tm, D_MODEL), row),
                  pl.BlockSpec((tm, ATT_W), row),
                  pl.BlockSpec((tm, HG_W), row),
                  pl.BlockSpec((tm, PLE_DIM), row),
                  pl.BlockSpec((ATT_W + HG_W, D_MODEL), const, **resident),
                  pl.BlockSpec((D_MODEL, D_MODEL), const, **resident),
                  pl.BlockSpec((PLE_DIM, D_MODEL), const, **resident),
                  pl.BlockSpec((1, D_MODEL), const),
                  pl.BlockSpec((1, D_MODEL), const)],
        out_specs=pl.BlockSpec((tm, D_MODEL), row),
        out_shape=jax.ShapeDtypeStruct((t, D_MODEL), F32),
        compiler_params=pltpu.CompilerParams(
            dimension_semantics=("parallel",), vmem_limit_bytes=VMEM_LIMIT),
        name="outproj_ple",
    )(x2, y_att, y_hg, p2, w_out, w_pg, w_pp, g_post.reshape(1, D_MODEL), g_ple.reshape(1, D_MODEL))


def kernel(x, p, positions, w_in, w_out, g_pre, g_post, g_hg, lb_fwd, lb_bwd, w_pg, w_pp, g_ple):
    depth = w_in.shape[0]
    batch, seq, _ = x.shape
    assert seq % ATT_TILE == 0 and seq // max(DILATIONS) >= ATT_KB
    t = batch * seq
    cos_t, sin_t = _rope_tables(positions)
    h = x.reshape(t, D_MODEL)
    for i in range(depth):
        z = _inproj(h, g_pre[i], w_in[i], cos_t, sin_t)
        y_att = _attention(z, batch, seq)
        y_hg = _hgrn(z, lb_fwd, lb_bwd, g_hg[i], batch, seq, i)
        h = _outproj(h, y_att, y_hg, p[i].reshape(t, PLE_DIM), w_out[i].astype(BF16),
                     w_pg[i].astype(BF16), w_pp[i].astype(BF16), g_post[i], g_ple[i])
    return h.reshape(batch, seq, D_MODEL)
```

```python
import functools

import jax
import jax.numpy as jnp
from jax import lax
from jax.experimental import pallas as pl
from jax.experimental.pallas import tpu as pltpu

F32 = jnp.float32
BF16 = jnp.bfloat16

D_MODEL = 2048
PLE_DIM = 256
ATT_HEADS = 16
ATT_HD = 64
ATT_W = ATT_HEADS * ATT_HD
ROT_DIM = ATT_HD // 4
ROPE_THETA = 500000.0
DILATION_CFG = ((128, 1), (512, 4), (2048, 16))
DILATIONS = tuple(d for _, d in DILATION_CFG)
HALF = (DILATION_CFG[0][0] // 2) // DILATION_CFG[0][1]
assert all((w // 2) // d == HALF for w, d in DILATION_CFG)
HG_HEADS = 8
HG_DK = 128
HG_W = HG_HEADS * HG_DK
HG_CHUNK = 128
IN_W = 4 * ATT_W + 5 * HG_W
EPS = 1e-6
NEG = -1e30
LOG2E = 1.4426950408889634

LANES = 128
VMEM_LIMIT = 56 * 1024 * 1024

COL_AQ, COL_AK, COL_AV, COL_AG = (i * ATT_W // LANES for i in range(4))
COL_HQ, COL_HFF, COL_HFB, COL_HV, COL_HG = (4 * ATT_W // LANES + i * HG_W // LANES for i in range(5))

INPROJ_PREP_TN = 512
N_LATE_WEIGHTS = 3
LATE_WEIGHT_BLOCKS = 16
INPROJ_TM = 1024
INPROJ_TN = 1024
INPROJ_ROPE_CHUNK = 256
ATT_TILE = 1024
ATT_QB = 2 * HALF
ATT_KB = ATT_QB + 2 * HALF
ATT_DENSE_MAX = 256
ATT_DENSE_UNROLL = 8
HG_UNROLL = 16
HG_PRE_ROWS = 512
OUT_TM = 512
OUT_SUBBLOCKS = 2
HG_FAST_MIN_LOGDECAY = -120.0


def _bdot(a, b):
    return jnp.dot(a.astype(BF16), b.astype(BF16), preferred_element_type=F32)


def _bdot_nt(a, b):
    return lax.dot_general(a.astype(BF16), b.astype(BF16), (((1,), (1,)), ((), ())),
                           preferred_element_type=F32)


def _bdot_tn(a, b):
    return lax.dot_general(a.astype(BF16), b.astype(BF16), (((0,), (0,)), ((), ())),
                           preferred_element_type=F32)


def _sigmoid(x):
    return 0.5 * jnp.tanh(0.5 * x) + 0.5


def _silu(x):
    return x * _sigmoid(x)


def _pair_swap_matrix():
    h = ROT_DIM // 2
    src = lax.broadcasted_iota(jnp.int32, (LANES, LANES), 0)
    dst = lax.broadcasted_iota(jnp.int32, (LANES, LANES), 1)
    shift = ATT_HD - h
    sigma = jnp.where((dst >= h) & (dst < ROT_DIM), dst + shift,
                      jnp.where((dst >= ATT_HD) & (dst < ATT_HD + h), dst - shift, dst))
    return (src == sigma).astype(BF16)


def _cast_weight_tile(w_ref, o_refs, j, swap_tiles, rb=256):
    n_rows, tn = w_ref.shape

    @pl.when(j < swap_tiles)
    def _():
        perm = _pair_swap_matrix()

        def body(i, carry):
            rows = pl.ds(pl.multiple_of(i * rb, rb), rb)
            for cb in range(tn // LANES):
                cols = slice(cb * LANES, (cb + 1) * LANES)
                wb = jnp.dot(w_ref[rows, cols].astype(BF16), perm, preferred_element_type=F32).astype(BF16)
                for o_ref in o_refs:
                    o_ref[rows, cols] = wb
            return carry
        lax.fori_loop(0, n_rows // rb, body, 0, unroll=True)

    @pl.when(j >= swap_tiles)
    def _():
        def body(i, carry):
            rows = pl.ds(pl.multiple_of(i * rb, rb), rb)
            wb = w_ref[rows, :].astype(BF16)
            for o_ref in o_refs:
                o_ref[rows, :] = wb
            return carry
        lax.fori_loop(0, n_rows // rb, body, 0)


def _head_a_lanes(lane):
    h = ROT_DIM // 2
    return (lane < h) | ((lane >= ROT_DIM) & (lane < ATT_HD + h))


ROPE_FREQS = ROT_DIM // 2
ROPE_GROUPS = LANES // ROPE_FREQS


def _split3(x):
    t1 = x.astype(BF16)
    r1 = x - t1.astype(F32)
    t2 = r1.astype(BF16)
    return t1, t2, (r1 - t2.astype(F32)).astype(BF16)


def _rope_table_kernel(pos_ref, inv_ref, cos_ref, sin_ref, c_s, s_s):
    p = pl.program_id(0)

    @pl.when(p == 0)
    def _():
        ang = pos_ref[...].astype(F32) * inv_ref[...]
        c_s[...] = jnp.concatenate(_split3(jnp.cos(ang)), axis=1)
        s_s[...] = jnp.concatenate(_split3(jnp.sin(ang)), axis=1)

    src = lax.broadcasted_iota(jnp.int32, (3 * LANES, LANES), 0) & (LANES - 1)
    dst = lax.broadcasted_iota(jnp.int32, (3 * LANES, LANES), 1)
    spread = (src == p * ROPE_FREQS + (dst & (ROPE_FREQS - 1))).astype(BF16)
    c = jnp.dot(c_s[...], spread, preferred_element_type=F32)
    s = jnp.dot(s_s[...], spread, preferred_element_type=F32)
    lane = lax.broadcasted_iota(jnp.int32, c.shape, 1)
    is_x1 = lane < ROT_DIM
    is_x2 = (lane >= ATT_HD) & (lane < ATT_HD + ROT_DIM)
    cos_ref[...] = jnp.where(is_x1 | is_x2, c, 1.0)
    sin_ref[...] = jnp.where(is_x1, -s, jnp.where(is_x2, s, 0.0))


def _rope_tables(positions):
    t = positions.size
    n = t // ROPE_GROUPS
    pos_c = jnp.repeat(positions.reshape(ROPE_GROUPS, n).T, ROPE_FREQS, axis=1)
    inv = jnp.power(ROPE_THETA, -jnp.arange(0, ROT_DIM, 2, dtype=F32) / ROT_DIM)
    inv_lane = jnp.tile(inv, ROPE_GROUPS).reshape(1, LANES)
    out = jax.ShapeDtypeStruct((t, LANES), F32)
    return pl.pallas_call(
        _rope_table_kernel,
        grid=(ROPE_GROUPS,),
        in_specs=[pl.BlockSpec((n, LANES), lambda p: (0, 0)),
                  pl.BlockSpec((1, LANES), lambda p: (0, 0))],
        out_specs=[pl.BlockSpec((n, LANES), lambda p: (p, 0))] * 2,
        out_shape=[out] * 2,
        scratch_shapes=[pltpu.VMEM((n, 3 * LANES), BF16)] * 2,
        compiler_params=pltpu.CompilerParams(dimension_semantics=("arbitrary",)),
        name="rope_tables",
    )(pos_c, inv_lane)


def _inproj_kernel(*refs, tm, tn, prepare_weights):
    j = pl.program_id(1)
    rb = 128
    n_q = ATT_W // tn
    n_rope = 2 * ATT_W // tn
    if prepare_weights:
        x_ref, g_ref, w32_ref, cos_ref, sin_ref = refs[:5]
        late32 = refs[5:5 + N_LATE_WEIGHTS]
        z_ref, wout_ref = refs[5 + N_LATE_WEIGHTS:7 + N_LATE_WEIGHTS]
        late16 = refs[7 + N_LATE_WEIGHTS:7 + 2 * N_LATE_WEIGHTS]
        u_ref, w_ref = refs[7 + 2 * N_LATE_WEIGHTS:]
        _cast_weight_tile(w32_ref, (w_ref, wout_ref), j, n_rope)
        for src, dst in zip(late32, late16):
            dst[...] = src[...].astype(BF16)
    else:
        x_ref, g_ref, w_ref, cos_ref, sin_ref, _, z_ref, u_ref = refs

    def norm_rows(r0, n):
        for i in range(n // rb):
            rows = slice(r0 + i * rb, r0 + (i + 1) * rb)
            xb = x_ref[rows, :]
            ms = jnp.mean(xb * xb, axis=-1, keepdims=True)
            u_ref[rows, :] = ((xb * lax.rsqrt(ms + EPS)) * g_ref[...]).astype(BF16)

    def project_rotate(r0, n):
        rows = slice(r0, r0 + n)
        scale = jnp.where(j < n_q, ATT_HD ** -0.5 * LOG2E, 1.0).astype(F32)
        c = cos_ref[rows, :] * scale
        s = sin_ref[rows, :] * scale
        for cb in range(tn // INPROJ_ROPE_CHUNK):
            cols = slice(cb * INPROJ_ROPE_CHUNK, (cb + 1) * INPROJ_ROPE_CHUNK)
            zc = jnp.dot(u_ref[rows, :], w_ref[:, cols], preferred_element_type=F32)
            parts = [zc[:, b * LANES:(b + 1) * LANES] for b in range(INPROJ_ROPE_CHUNK // LANES)]
            z_ref[rows, cols] = jnp.concatenate(
                [blk * c + pltpu.roll(blk, LANES // 2, 1) * s for blk in parts], axis=1)

    @pl.when(j == 0)
    def _():
        for r0 in range(0, tm, tm // 2):
            norm_rows(r0, tm // 2)
            project_rotate(r0, tm // 2)

    @pl.when((j > 0) & (j < n_rope))
    def _():
        project_rotate(0, tm)

    @pl.when(j >= n_rope)
    def _():
        z_ref[...] = jnp.dot(u_ref[...], w_ref[...], preferred_element_type=F32)


def _inproj(x2, g_pre, w_in, cos_t, sin_t, late_weights):
    assert len(late_weights) == N_LATE_WEIGHTS
    t = x2.shape[0]
    tm = min(INPROJ_TM, t)
    g2 = g_pre.reshape(1, D_MODEL)
    z_shape = jax.ShapeDtypeStruct((t, IN_W), F32)
    params = pltpu.CompilerParams(dimension_semantics=("parallel", "arbitrary"), vmem_limit_bytes=VMEM_LIMIT)
    u_scratch = pltpu.VMEM((tm, D_MODEL), BF16)

    def specs(tn, first):
        row = lambda i, j: (i + first, 0)
        return ([pl.BlockSpec((tm, D_MODEL), row),
                 pl.BlockSpec((1, D_MODEL), lambda i, j: (0, 0)),
                 pl.BlockSpec((D_MODEL, tn), lambda i, j: (0, j)),
                 pl.BlockSpec((tm, LANES), row),
                 pl.BlockSpec((tm, LANES), row)],
                pl.BlockSpec((tm, tn), lambda i, j: (i + first, j)))

    tn = INPROJ_PREP_TN
    n_steps = IN_W // tn
    in_specs, z_spec = specs(tn, 0)
    assert n_steps >= LATE_WEIGHT_BLOCKS
    late_specs = [pl.BlockSpec((w.shape[0] // LATE_WEIGHT_BLOCKS, w.shape[1]),
                               lambda i, j: (jnp.minimum(j, LATE_WEIGHT_BLOCKS - 1), 0)) for w in late_weights]
    outs = pl.pallas_call(
        functools.partial(_inproj_kernel, tm=tm, tn=tn, prepare_weights=True),
        grid=(1, n_steps),
        in_specs=in_specs + late_specs,
        out_specs=[z_spec, pl.BlockSpec((D_MODEL, tn), lambda i, j: (0, j))] + late_specs,
        out_shape=[z_shape, jax.ShapeDtypeStruct((D_MODEL, IN_W), BF16)]
        + [jax.ShapeDtypeStruct(w.shape, BF16) for w in late_weights],
        scratch_shapes=[u_scratch, pltpu.VMEM((D_MODEL, tn), BF16)],
        compiler_params=params,
        name="inproj_first",
    )(x2, g2, w_in, cos_t, sin_t, *late_weights)
    z, w_bf16, late_bf16 = outs[0], outs[1], outs[2:]
    if t == tm:
        return z, late_bf16
    tn = INPROJ_TN
    in_specs, z_spec = specs(tn, 1)
    z = pl.pallas_call(
        functools.partial(_inproj_kernel, tm=tm, tn=tn, prepare_weights=False),
        grid=(t // tm - 1, IN_W // tn),
        in_specs=in_specs + [pl.BlockSpec(memory_space=pl.ANY)],
        out_specs=z_spec,
        out_shape=z_shape,
        input_output_aliases={len(in_specs): 0},
        scratch_shapes=[u_scratch],
        compiler_params=params,
        name="inproj",
    )(x2, g2, w_bf16, cos_t, sin_t, z)
    return z, late_bf16


def _att_geometry(seq):
    geo = {}
    for d in DILATIONS:
        n_cls = seq // d
        dense = n_cls <= ATT_DENSE_MAX
        geo[d] = (dense, n_cls if dense else ATT_QB, n_cls if dense else ATT_KB)
    return geo


def _att_kernel(q_ref, k_ref, v_ref, g_ref, y_ref, mt_s, lt_s, nt_s, lsed_s, od_s,
                bias_s, biasd_s, p_s, pd_s, *, seq):
    geo = _att_geometry(seq)
    tiled = [d for d in DILATIONS if not geo[d][0]]
    dense = [d for d in DILATIONS if geo[d][0]]
    n_tiles = seq // ATT_TILE
    ones_kv = {kb: jnp.ones((kb, LANES), BF16) for kb in {geo[d][2] for d in DILATIONS}}

    def lanes(rows):
        lane = lax.broadcasted_iota(jnp.int32, (rows, LANES), 1)
        return lane < ATT_HD, _head_a_lanes(lane)

    def band_bias(qb, kb, delta):
        qa = lax.broadcasted_iota(jnp.int32, (2 * qb, kb), 0) & (qb - 1)
        ka = lax.broadcasted_iota(jnp.int32, (2 * qb, kb), 1)
        return jnp.where(jnp.abs(ka - qa + delta) <= HALF, 0.0, NEG)

    for idx, delta in enumerate((-HALF, 0, -2 * HALF)):
        bias_s[idx] = band_bias(ATT_QB, ATT_KB, delta)
    for d in dense:
        biasd_s[...] = band_bias(geo[d][1], geo[d][2], 0)

    def rows_of(start, size, d):
        return pl.ds(start, size) if d == 1 else pl.ds(start, size, stride=d)

    def score_block(d, r, a0, bias, p_ref):
        _, qb, kb = geo[d]
        head_a, qk_head_a = lanes(qb)
        ka0 = 0 if geo[d][0] else jnp.clip(a0 - HALF, 0, seq // d - kb)
        q2 = q_ref[rows_of(r + d * a0, qb, d), :]
        kk = k_ref[rows_of(r + d * ka0, kb, d), :].astype(BF16)
        qs = jnp.concatenate([jnp.where(qk_head_a, q2, 0.0), jnp.where(qk_head_a, 0.0, q2)], axis=0)
        s = _bdot_nt(qs, kk) + bias
        m = jnp.max(s, axis=-1, keepdims=True)
        p_ref[...] = jnp.exp2(s - m).astype(BF16)
        mb = jnp.broadcast_to(m, (2 * qb, LANES))
        return jnp.where(head_a, mb[:qb], mb[qb:])

    def value_block(d, r, a0, p_ref):
        _, qb, kb = geo[d]
        head_a, _ = lanes(qb)
        ka0 = 0 if geo[d][0] else jnp.clip(a0 - HALF, 0, seq // d - kb)
        vv = v_ref[rows_of(r + d * ka0, kb, d), :].astype(BF16)
        nl = jnp.dot(p_ref[...], jnp.concatenate([vv, ones_kv[kb]], axis=1), preferred_element_type=F32)
        return (jnp.where(head_a, nl[:qb, LANES:], nl[qb:, LANES:]),
                jnp.where(head_a, nl[:qb, :LANES], nl[qb:, :LANES]))

    for gi, d in enumerate(dense):
        def dense_scores(r, c, gi=gi, d=d):
            lsed_s[gi * d + r] = score_block(d, r, 0, biasd_s[...], pd_s.at[r])
            return c

        def dense_values(r, c, gi=gi, d=d):
            l, n = value_block(d, r, 0, pd_s.at[r])
            od_s[gi * d + r] = (n / l).astype(BF16)
            lsed_s[gi * d + r] = lsed_s[gi * d + r] + jnp.log2(l)
            return c
        lax.fori_loop(0, d, dense_scores, 0, unroll=ATT_DENSE_UNROLL)
        lax.fori_loop(0, d, dense_values, 0, unroll=ATT_DENSE_UNROLL)

    cb = 256

    def to_natural(d):
        i = lax.broadcasted_iota(jnp.int32, (cb, cb), 0)
        j = lax.broadcasted_iota(jnp.int32, (cb, cb), 1)
        return (j == (i % d) * (cb // d) + i // d).astype(BF16)

    def tile_body(tj, carry):
        tile0 = tj * ATT_TILE
        for gi, d in enumerate(tiled):
            n_cls = seq // d
            per_cls = ATT_TILE // d // ATT_QB
            n_blk = ATT_TILE // ATT_QB
            for bi in range(n_blk):
                r = bi // per_cls
                a0 = tile0 // d + (bi % per_cls) * ATT_QB
                which = jnp.where(a0 == 0, 1, jnp.where(a0 == n_cls - ATT_QB, 2, 0))
                mt_s[gi, rows_of(r + d * a0 - tile0, ATT_QB, d), :] = score_block(
                    d, r, a0, bias_s[which], p_s.at[bi])
            for bi in range(n_blk):
                r = bi // per_cls
                a0 = tile0 // d + (bi % per_cls) * ATT_QB
                dst = rows_of(r + d * a0 - tile0, ATT_QB, d)
                lt_s[gi, dst, :], nt_s[gi, dst, :] = value_block(d, r, a0, p_s.at[bi])

        def comb_body(ci, c):
            rows = pl.ds(pl.multiple_of(ci * cb, cb), cb)
            orow = pl.ds(pl.multiple_of(tile0 + ci * cb, cb), cb)
            parts = [(mt_s[gi, rows, :], lt_s[gi, rows, :], nt_s[gi, rows, :]) for gi in range(len(tiled))]
            for gi, d in enumerate(dense):
                per = cb // d
                a0 = pl.multiple_of((tile0 + ci * cb) // d, per)
                o_cls = jnp.concatenate([od_s[gi * d + r, pl.ds(a0, per), :] for r in range(d)], axis=0)
                lse_cls = jnp.concatenate([lsed_s[gi * d + r, pl.ds(a0, per), :] for r in range(d)], axis=0)
                perm = to_natural(d)
                o_nat = jnp.dot(perm, o_cls, preferred_element_type=F32)
                hi = lse_cls.astype(BF16)
                lo = (lse_cls - hi.astype(F32)).astype(BF16)
                lse2 = jnp.dot(perm, jnp.concatenate([hi, lo], axis=1), preferred_element_type=F32)
                parts.append((lse2[:, :LANES] + lse2[:, LANES:], None, o_nat))
            mx = functools.reduce(jnp.maximum, [part[0] for part in parts])
            num = jnp.zeros((cb, LANES), F32)
            den = jnp.zeros((cb, LANES), F32)
            for m, l, n in parts:
                w = jnp.exp2(m - mx)
                num = num + w * n
                den = den + (w if l is None else w * l)
            y_ref[orow, :] = ((num / den) * _silu(g_ref[orow, :])).astype(y_ref.dtype)
            return c
        lax.fori_loop(0, ATT_TILE // cb, comb_body, 0, unroll=True)
        return carry

    lax.fori_loop(0, n_tiles, tile_body, 0)


def _attention(z, batch, seq):
    t = batch * seq
    n_pairs = ATT_W // LANES

    def col(c0):
        return pl.BlockSpec((seq, LANES), lambda b, h, c0=c0: (b, c0 + h))

    geo = _att_geometry(seq)
    dense_geo = sorted({geo[d][1:] for d in DILATIONS if geo[d][0]})
    assert len(dense_geo) <= 1, "one band-mask table serves the dense groups"
    dqb, dkb = dense_geo[0] if dense_geo else (8, LANES)
    n_dense = sum(geo[d][0] for d in DILATIONS)
    tiled = pltpu.VMEM((max(len(DILATIONS) - n_dense, 1), ATT_TILE, LANES), F32)
    n_dense_cls = max(sum(d for d in DILATIONS if geo[d][0]), 1)
    return pl.pallas_call(
        functools.partial(_att_kernel, seq=seq),
        grid=(batch, n_pairs),
        in_specs=[col(COL_AQ), col(COL_AK), col(COL_AV), col(COL_AG)],
        out_specs=pl.BlockSpec((seq, LANES), lambda b, h: (b, h)),
        out_shape=jax.ShapeDtypeStruct((t, ATT_W), BF16),
        scratch_shapes=[tiled, tiled, tiled,
                        pltpu.VMEM((n_dense_cls, dqb, LANES), F32),
                        pltpu.VMEM((n_dense_cls, dqb, LANES), BF16),
                        pltpu.VMEM((3, 2 * ATT_QB, ATT_KB), F32),
                        pltpu.VMEM((2 * dqb, dkb), F32),
                        pltpu.VMEM((ATT_TILE // ATT_QB, 2 * ATT_QB, ATT_KB), BF16),
                        pltpu.VMEM((max(DILATIONS), 2 * dqb, dkb), BF16)],
        compiler_params=pltpu.CompilerParams(
            dimension_semantics=("parallel", "parallel"), vmem_limit_bytes=VMEM_LIMIT),
        name="band_attention",
    )(z, z, z, z)


def _chunk_cumsum(g, tri):
    c = HG_CHUNK
    g1 = g.astype(BF16)
    g2 = (g - g1.astype(F32)).astype(BF16)
    terms = jnp.concatenate([g1, g2], axis=1)
    k = g.shape[1]
    outs = []
    for i in range(g.shape[0] // c):
        part = jnp.dot(tri, terms[i * c:(i + 1) * c, :], preferred_element_type=F32)
        outs.append(part[:, k:] + part[:, :k])
    return jnp.concatenate(outs, axis=0)


def _hgrn_kernel(hq_ref, ff_ref, fb_ref, hv_ref, hg_ref, lbf_ref, lbb_ref, ghg_ref, y_ref,
                 q_s, kf_s, kb_s, bf_s, bb_s, o_s,
                 qtf_s, ktf_s, qtb_s, ktb_s, v16_s,
                 emf_s, emb_s, stf_s, stb_s, *, seq, layer):
    C = HG_CHUNK
    n_chunks = seq // C

    def lower_bound(ref):
        a = ref[...]
        e = jnp.exp(a - jnp.max(a, axis=0, keepdims=True))
        return jnp.sum(e[0:layer + 1, :], axis=0, keepdims=True) / jnp.sum(e, axis=0, keepdims=True)

    lbf = lower_bound(lbf_ref)
    lbb = lower_bound(lbb_ref)

    ti = lax.broadcasted_iota(jnp.int32, (C, C), 0)
    si = lax.broadcasted_iota(jnp.int32, (C, C), 1)
    tri_f = si <= ti
    tri_b = si >= ti

    pr = HG_PRE_ROWS

    cpb = pr // C
    fwd = (ff_ref, lbf, kf_s, bf_s, qtf_s, ktf_s, emf_s, False)
    bwd = (fb_ref, lbb, kb_s, bb_s, qtb_s, ktb_s, emb_s, True)

    def per_chunk_rows(x3):
        return jnp.broadcast_to(x3, (cpb, C, HG_DK)).reshape(pr, HG_DK)

    def pre_body(i, lo):
        rows = pl.ds(pl.multiple_of(i * pr, pr), pr)
        q = _silu(hq_ref[rows, :])
        q_s[rows, :] = q
        o_s[rows, :] = jnp.zeros((pr, HG_DK), F32)
        v16_s[rows, :] = hv_ref[rows, :].astype(BF16)
        for f_ref, lb, k_s, b_s, qt_s, kt_s, em_s, reverse in (fwd, bwd):
            f = lb + (1.0 - lb) * _sigmoid(f_ref[rows, :])
            kk = 1.0 - f
            tri = tri_b if reverse else tri_f
            b = _chunk_cumsum(jnp.log(f), tri.astype(BF16))
            last = 0 if reverse else C - 1
            btot = b.reshape(cpb, C, HG_DK)[:, last:last + 1, :]
            mid = 0.5 * btot
            midr = per_chunk_rows(mid)
            k_s[rows, :] = kk
            b_s[rows, :] = b
            qt_s[rows, :] = (q * jnp.exp(b - midr)).astype(BF16)
            kt_s[rows, :] = (kk * jnp.exp(midr - b)).astype(BF16)
            em_s[pl.ds(pl.multiple_of(i * cpb, cpb), cpb), :] = jnp.exp(mid).reshape(cpb, HG_DK)
            lo = jnp.minimum(lo, btot.reshape(cpb, HG_DK))
        return lo

    lo = lax.fori_loop(0, seq // pr, pre_body, jnp.zeros((cpb, HG_DK), F32))
    fast = jnp.min(lo) >= HG_FAST_MIN_LOGDECAY

    stf_s[...] = jnp.zeros_like(stf_s)
    stb_s[...] = jnp.zeros_like(stb_s)

    def chunk(ci, tri, d, st_ref, exact):
        _, _, k_s, b_s, qt_s, kt_s, em_s, reverse = d
        r0 = pl.multiple_of(ci * C, C)
        rows = pl.ds(r0, C)
        v = v16_s[rows, :]
        st = st_ref[...]
        if exact:
            q = q_s[rows, :]
            b = b_s[rows, :]
            kk = k_s[rows, :]
            btot = b_s[pl.ds(r0 + (0 if reverse else C - 1), 1), :]

            def col_body(s, att):
                w = q * k_s[pl.ds(r0 + s, 1), :] * jnp.exp(jnp.minimum(b - b_s[pl.ds(r0 + s, 1), :], 0.0))
                return att + jnp.where(si == s, jnp.sum(w, axis=-1, keepdims=True), 0.0)
            att = lax.fori_loop(0, C, col_body, jnp.zeros((C, C), F32))
            o = _bdot(jnp.where(tri, att, 0.0), v) + _bdot_nt(q * jnp.exp(b), st)
            st_ref[...] = st * jnp.exp(btot) + _bdot_tn(v, kk * jnp.exp(btot - b))
        else:
            em = em_s[pl.ds(ci, 1), :]
            qt = qt_s[rows, :]
            kt = kt_s[rows, :]
            att = jnp.where(tri, _bdot_nt(qt, kt), 0.0)
            o = _bdot(att, v) + _bdot_nt(qt, st * em)
            st_ref[...] = st * (em * em) + _bdot_tn(v, kt) * em
        o_s[rows, :] = o_s[rows, :] + o

    def scan(exact):
        def body(n, carry):
            chunk(n, tri_f, fwd, stf_s, exact)
            chunk(n_chunks - 1 - n, tri_b, bwd, stb_s, exact)
            return carry
        lax.fori_loop(0, n_chunks, body, 0, unroll=1 if exact else HG_UNROLL)

    lax.cond(fast, lambda: scan(False), lambda: scan(True))

    def post_body(i, carry):
        rows = pl.ds(pl.multiple_of(i * pr, pr), pr)
        o = o_s[rows, :]
        ms = jnp.mean(o * o, axis=-1, keepdims=True)
        y = (o * lax.rsqrt(ms + EPS)) * ghg_ref[...]
        y_ref[rows, :] = (y * _silu(hg_ref[rows, :])).astype(y_ref.dtype)
        return carry
    lax.fori_loop(0, seq // pr, post_body, 0)


def _hgrn(z, lb_fwd, lb_bwd, g_hg, batch, seq, layer):
    t = batch * seq

    def col(c0):
        return pl.BlockSpec((seq, LANES), lambda b, h, c0=c0: (b, c0 + h))

    n_lb = lb_fwd.shape[0]
    lb_spec = pl.BlockSpec((n_lb, HG_DK), lambda b, h: (0, h))
    seq_f32 = pltpu.VMEM((seq, HG_DK), F32)
    seq_bf16 = pltpu.VMEM((seq, HG_DK), BF16)
    chunk_f32 = pltpu.VMEM((seq // HG_CHUNK, HG_DK), F32)
    state = pltpu.VMEM((HG_DK, HG_DK), F32)
    return pl.pallas_call(
        functools.partial(_hgrn_kernel, seq=seq, layer=layer),
        grid=(batch, HG_HEADS),
        in_specs=[col(COL_HQ), col(COL_HFF), col(COL_HFB), col(COL_HV), col(COL_HG),
                  lb_spec, lb_spec, pl.BlockSpec((1, HG_DK), lambda b, h: (0, 0))],
        out_specs=pl.BlockSpec((seq, HG_DK), lambda b, h: (b, h)),
        out_shape=jax.ShapeDtypeStruct((t, HG_W), BF16),
        scratch_shapes=[seq_f32] * 6 + [seq_bf16] * 5 + [chunk_f32, chunk_f32, state, state],
        compiler_params=pltpu.CompilerParams(
            dimension_semantics=("parallel", "parallel"), vmem_limit_bytes=VMEM_LIMIT),
        name="hgrn2",
    )(z, z, z, z, z, lb_fwd, lb_bwd, g_hg.reshape(1, HG_DK))


def _rms(y, g):
    ms = jnp.mean(y * y, axis=-1, keepdims=True)
    return (y * lax.rsqrt(ms + EPS)) * g


def _outproj_kernel(x_ref, ya_ref, yh_ref, p_ref, wo_ref, wpg_ref, wpp_ref, gpost_ref, gple_ref, o_ref):
    sub = x_ref.shape[0] // OUT_SUBBLOCKS
    for r in range(OUT_SUBBLOCKS):
        rows = slice(r * sub, (r + 1) * sub)
        y = (jnp.dot(ya_ref[rows, :], wo_ref[0:ATT_W, :], preferred_element_type=F32)
             + jnp.dot(yh_ref[rows, :], wo_ref[ATT_W:, :], preferred_element_type=F32))
        h1 = x_ref[rows, :] + _rms(y, gpost_ref[...])
        gate = jax.nn.sigmoid(jnp.dot(h1.astype(BF16), wpg_ref[...], preferred_element_type=F32))
        e = jnp.dot(p_ref[rows, :].astype(BF16), wpp_ref[...], preferred_element_type=F32)
        o_ref[rows, :] = h1 + _rms(gate * e, gple_ref[...])


def _outproj(x2, y_att, y_hg, p2, w_out, w_pg, w_pp, g_post, g_ple):
    t = x2.shape[0]
    tm = min(OUT_TM, t)
    row = lambda i: (i, 0)
    const = lambda i: (0, 0)
    resident = dict(pipeline_mode=pl.Buffered(1))
    return pl.pallas_call(
        _outproj_kernel,
        grid=(t // tm,),
        in_specs=[pl.BlockSpec((tm, D_MODEL), row),
                  pl.BlockSpec((tm, ATT_W), row),
                  pl.BlockSpec((tm, HG_W), row),
                  pl.BlockSpec((tm, PLE_DIM), row),
                  pl.BlockSpec((ATT_W + HG_W, D_MODEL), const, **resident),
                  pl.BlockSpec((D_MODEL, D_MODEL), const, **resident),
                  pl.BlockSpec((PLE_DIM, D_MODEL), const, **resident),
                  pl.BlockSpec((1, D_MODEL), const),
                  pl.BlockSpec((1, D_MODEL), const)],
        out_specs=pl.BlockSpec((tm, D_MODEL), row),
        out_shape=jax.ShapeDtypeStruct((t, D_MODEL), F32),
        compiler_params=pltpu.CompilerParams(
            dimension_semantics=("parallel",), vmem_limit_bytes=VMEM_LIMIT),
        name="outproj_ple",
    )(x2, y_att, y_hg, p2, w_out, w_pg, w_pp, g_post.reshape(1, D_MODEL), g_ple.reshape(1, D_MODEL))


def kernel(x, p, positions, w_in, w_out, g_pre, g_post, g_hg, lb_fwd, lb_bwd, w_pg, w_pp, g_ple):
    depth = w_in.shape[0]
    batch, seq, _ = x.shape
    assert seq % ATT_TILE == 0 and seq // max(DILATIONS) >= ATT_KB
    t = batch * seq
    cos_t, sin_t = _rope_tables(positions)
    h = x.reshape(t, D_MODEL)
    for i in range(depth):
        z, (w_out16, w_pg16, w_pp16) = _inproj(h, g_pre[i], w_in[i], cos_t, sin_t,
                                               (w_out[i], w_pg[i], w_pp[i]))
        y_att = _attention(z, batch, seq)
        y_hg = _hgrn(z, lb_fwd, lb_bwd, g_hg[i], batch, seq, i)
        h = _outproj(h, y_att, y_hg, p[i].reshape(t, PLE_DIM), w_out16, w_pg16, w_pp16,
                     g_post[i], g_ple[i])
    return h.reshape(batch, seq, D_MODEL)
```

```python
import functools

import jax
import jax.numpy as jnp
from jax import lax
from jax.experimental import pallas as pl
from jax.experimental.pallas import tpu as pltpu

F32 = jnp.float32
BF16 = jnp.bfloat16

D_MODEL = 2048
PLE_DIM = 256
ATT_HEADS = 16
ATT_HD = 64
ATT_W = ATT_HEADS * ATT_HD
ROT_DIM = ATT_HD // 4
ROPE_THETA = 500000.0
DILATION_CFG = ((128, 1), (512, 4), (2048, 16))
DILATIONS = tuple(d for _, d in DILATION_CFG)
HALF = (DILATION_CFG[0][0] // 2) // DILATION_CFG[0][1]
assert all((w // 2) // d == HALF for w, d in DILATION_CFG)
HG_HEADS = 8
HG_DK = 128
HG_W = HG_HEADS * HG_DK
HG_CHUNK = 128
IN_W = 4 * ATT_W + 5 * HG_W
EPS = 1e-6
NEG = -1e30
LOG2E = 1.4426950408889634

LANES = 128
VMEM_LIMIT = 56 * 1024 * 1024

COL_AQ, COL_AK, COL_AV, COL_AG = (i * ATT_W // LANES for i in range(4))
COL_HQ, COL_HFF, COL_HFB, COL_HV, COL_HG = (4 * ATT_W // LANES + i * HG_W // LANES for i in range(5))

INPROJ_PREP_TN = 512
N_LATE_WEIGHTS = 3
LATE_WEIGHT_BLOCKS = 16
INPROJ_TM = 1024
INPROJ_TN = 1024
INPROJ_ROPE_CHUNK = 256
ATT_TILE = 1024
ATT_QB = 2 * HALF
ATT_KB = ATT_QB + 2 * HALF
ATT_DENSE_MAX = 256
ATT_DENSE_UNROLL = 8
HG_BLOCK_UNROLL = 8
HG_PRE_ROWS = 512
OUT_TM = 512
OUT_SUBBLOCKS = 2
HG_FAST_MIN_LOGDECAY = -120.0


def _bdot(a, b):
    return jnp.dot(a.astype(BF16), b.astype(BF16), preferred_element_type=F32)


def _bdot_nt(a, b):
    return lax.dot_general(a.astype(BF16), b.astype(BF16), (((1,), (1,)), ((), ())),
                           preferred_element_type=F32)


def _bdot_tn(a, b):
    return lax.dot_general(a.astype(BF16), b.astype(BF16), (((0,), (0,)), ((), ())),
                           preferred_element_type=F32)


def _sigmoid(x):
    return 0.5 * jnp.tanh(0.5 * x) + 0.5


def _silu(x):
    return x * _sigmoid(x)


def _pair_swap_matrix():
    h = ROT_DIM // 2
    src = lax.broadcasted_iota(jnp.int32, (LANES, LANES), 0)
    dst = lax.broadcasted_iota(jnp.int32, (LANES, LANES), 1)
    shift = ATT_HD - h
    sigma = jnp.where((dst >= h) & (dst < ROT_DIM), dst + shift,
                      jnp.where((dst >= ATT_HD) & (dst < ATT_HD + h), dst - shift, dst))
    return (src == sigma).astype(BF16)


def _cast_weight_tile(w_ref, o_refs, j, swap_tiles, rb=256):
    n_rows, tn = w_ref.shape

    @pl.when(j < swap_tiles)
    def _():
        perm = _pair_swap_matrix()

        def body(i, carry):
            rows = pl.ds(pl.multiple_of(i * rb, rb), rb)
            for cb in range(tn // LANES):
                cols = slice(cb * LANES, (cb + 1) * LANES)
                wb = jnp.dot(w_ref[rows, cols].astype(BF16), perm, preferred_element_type=F32).astype(BF16)
                for o_ref in o_refs:
                    o_ref[rows, cols] = wb
            return carry
        lax.fori_loop(0, n_rows // rb, body, 0, unroll=True)

    @pl.when(j >= swap_tiles)
    def _():
        def body(i, carry):
            rows = pl.ds(pl.multiple_of(i * rb, rb), rb)
            wb = w_ref[rows, :].astype(BF16)
            for o_ref in o_refs:
                o_ref[rows, :] = wb
            return carry
        lax.fori_loop(0, n_rows // rb, body, 0)


def _head_a_lanes(lane):
    h = ROT_DIM // 2
    return (lane < h) | ((lane >= ROT_DIM) & (lane < ATT_HD + h))


ROPE_FREQS = ROT_DIM // 2
ROPE_GROUPS = LANES // ROPE_FREQS


def _split3(x):
    t1 = x.astype(BF16)
    r1 = x - t1.astype(F32)
    t2 = r1.astype(BF16)
    return t1, t2, (r1 - t2.astype(F32)).astype(BF16)


def _rope_table_kernel(pos_ref, inv_ref, cos_ref, sin_ref, c_s, s_s):
    p = pl.program_id(0)

    @pl.when(p == 0)
    def _():
        ang = pos_ref[...].astype(F32) * inv_ref[...]
        c_s[...] = jnp.concatenate(_split3(jnp.cos(ang)), axis=1)
        s_s[...] = jnp.concatenate(_split3(jnp.sin(ang)), axis=1)

    src = lax.broadcasted_iota(jnp.int32, (3 * LANES, LANES), 0) & (LANES - 1)
    dst = lax.broadcasted_iota(jnp.int32, (3 * LANES, LANES), 1)
    spread = (src == p * ROPE_FREQS + (dst & (ROPE_FREQS - 1))).astype(BF16)
    c = jnp.dot(c_s[...], spread, preferred_element_type=F32)
    s = jnp.dot(s_s[...], spread, preferred_element_type=F32)
    lane = lax.broadcasted_iota(jnp.int32, c.shape, 1)
    is_x1 = lane < ROT_DIM
    is_x2 = (lane >= ATT_HD) & (lane < ATT_HD + ROT_DIM)
    cos_ref[...] = jnp.where(is_x1 | is_x2, c, 1.0)
    sin_ref[...] = jnp.where(is_x1, -s, jnp.where(is_x2, s, 0.0))


def _rope_tables(positions):
    t = positions.size
    n = t // ROPE_GROUPS
    pos_c = jnp.repeat(positions.reshape(ROPE_GROUPS, n).T, ROPE_FREQS, axis=1)
    inv = jnp.power(ROPE_THETA, -jnp.arange(0, ROT_DIM, 2, dtype=F32) / ROT_DIM)
    inv_lane = jnp.tile(inv, ROPE_GROUPS).reshape(1, LANES)
    out = jax.ShapeDtypeStruct((t, LANES), F32)
    return pl.pallas_call(
        _rope_table_kernel,
        grid=(ROPE_GROUPS,),
        in_specs=[pl.BlockSpec((n, LANES), lambda p: (0, 0)),
                  pl.BlockSpec((1, LANES), lambda p: (0, 0))],
        out_specs=[pl.BlockSpec((n, LANES), lambda p: (p, 0))] * 2,
        out_shape=[out] * 2,
        scratch_shapes=[pltpu.VMEM((n, 3 * LANES), BF16)] * 2,
        compiler_params=pltpu.CompilerParams(dimension_semantics=("arbitrary",)),
        name="rope_tables",
    )(pos_c, inv_lane)


def _inproj_kernel(*refs, tm, tn, prepare_weights):
    j = pl.program_id(1)
    rb = 128
    n_q = ATT_W // tn
    n_rope = 2 * ATT_W // tn
    if prepare_weights:
        x_ref, g_ref, w32_ref, cos_ref, sin_ref = refs[:5]
        late32 = refs[5:5 + N_LATE_WEIGHTS]
        z_ref, wout_ref = refs[5 + N_LATE_WEIGHTS:7 + N_LATE_WEIGHTS]
        late16 = refs[7 + N_LATE_WEIGHTS:7 + 2 * N_LATE_WEIGHTS]
        u_ref, w_ref = refs[7 + 2 * N_LATE_WEIGHTS:]
        _cast_weight_tile(w32_ref, (w_ref, wout_ref), j, n_rope)
        for src, dst in zip(late32, late16):
            dst[...] = src[...].astype(BF16)
    else:
        x_ref, g_ref, w_ref, cos_ref, sin_ref, _, z_ref, u_ref = refs

    def norm_rows(r0, n):
        for i in range(n // rb):
            rows = slice(r0 + i * rb, r0 + (i + 1) * rb)
            xb = x_ref[rows, :]
            ms = jnp.mean(xb * xb, axis=-1, keepdims=True)
            u_ref[rows, :] = ((xb * lax.rsqrt(ms + EPS)) * g_ref[...]).astype(BF16)

    def project_rotate(r0, n):
        rows = slice(r0, r0 + n)
        scale = jnp.where(j < n_q, ATT_HD ** -0.5 * LOG2E, 1.0).astype(F32)
        c = cos_ref[rows, :] * scale
        s = sin_ref[rows, :] * scale
        for cb in range(tn // INPROJ_ROPE_CHUNK):
            cols = slice(cb * INPROJ_ROPE_CHUNK, (cb + 1) * INPROJ_ROPE_CHUNK)
            zc = jnp.dot(u_ref[rows, :], w_ref[:, cols], preferred_element_type=F32)
            parts = [zc[:, b * LANES:(b + 1) * LANES] for b in range(INPROJ_ROPE_CHUNK // LANES)]
            z_ref[rows, cols] = jnp.concatenate(
                [blk * c + pltpu.roll(blk, LANES // 2, 1) * s for blk in parts], axis=1)

    @pl.when(j == 0)
    def _():
        for r0 in range(0, tm, tm // 2):
            norm_rows(r0, tm // 2)
            project_rotate(r0, tm // 2)

    @pl.when((j > 0) & (j < n_rope))
    def _():
        project_rotate(0, tm)

    @pl.when(j >= n_rope)
    def _():
        z_ref[...] = jnp.dot(u_ref[...], w_ref[...], preferred_element_type=F32)


def _inproj(x2, g_pre, w_in, cos_t, sin_t, late_weights):
    assert len(late_weights) == N_LATE_WEIGHTS
    t = x2.shape[0]
    tm = min(INPROJ_TM, t)
    g2 = g_pre.reshape(1, D_MODEL)
    z_shape = jax.ShapeDtypeStruct((t, IN_W), F32)
    params = pltpu.CompilerParams(dimension_semantics=("parallel", "arbitrary"), vmem_limit_bytes=VMEM_LIMIT)
    u_scratch = pltpu.VMEM((tm, D_MODEL), BF16)

    def specs(tn, first):
        row = lambda i, j: (i + first, 0)
        return ([pl.BlockSpec((tm, D_MODEL), row),
                 pl.BlockSpec((1, D_MODEL), lambda i, j: (0, 0)),
                 pl.BlockSpec((D_MODEL, tn), lambda i, j: (0, j)),
                 pl.BlockSpec((tm, LANES), row),
                 pl.BlockSpec((tm, LANES), row)],
                pl.BlockSpec((tm, tn), lambda i, j: (i + first, j)))

    tn = INPROJ_PREP_TN
    n_steps = IN_W // tn
    in_specs, z_spec = specs(tn, 0)
    assert n_steps >= LATE_WEIGHT_BLOCKS
    late_specs = [pl.BlockSpec((w.shape[0] // LATE_WEIGHT_BLOCKS, w.shape[1]),
                               lambda i, j: (jnp.minimum(j, LATE_WEIGHT_BLOCKS - 1), 0)) for w in late_weights]
    outs = pl.pallas_call(
        functools.partial(_inproj_kernel, tm=tm, tn=tn, prepare_weights=True),
        grid=(1, n_steps),
        in_specs=in_specs + late_specs,
        out_specs=[z_spec, pl.BlockSpec((D_MODEL, tn), lambda i, j: (0, j))] + late_specs,
        out_shape=[z_shape, jax.ShapeDtypeStruct((D_MODEL, IN_W), BF16)]
        + [jax.ShapeDtypeStruct(w.shape, BF16) for w in late_weights],
        scratch_shapes=[u_scratch, pltpu.VMEM((D_MODEL, tn), BF16)],
        compiler_params=params,
        name="inproj_first",
    )(x2, g2, w_in, cos_t, sin_t, *late_weights)
    z, w_bf16, late_bf16 = outs[0], outs[1], outs[2:]
    if t == tm:
        return z, late_bf16
    tn = INPROJ_TN
    in_specs, z_spec = specs(tn, 1)
    z = pl.pallas_call(
        functools.partial(_inproj_kernel, tm=tm, tn=tn, prepare_weights=False),
        grid=(t // tm - 1, IN_W // tn),
        in_specs=in_specs + [pl.BlockSpec(memory_space=pl.ANY)],
        out_specs=z_spec,
        out_shape=z_shape,
        input_output_aliases={len(in_specs): 0},
        scratch_shapes=[u_scratch],
        compiler_params=params,
        name="inproj",
    )(x2, g2, w_bf16, cos_t, sin_t, z)
    return z, late_bf16


def _att_geometry(seq):
    geo = {}
    for d in DILATIONS:
        n_cls = seq // d
        dense = n_cls <= ATT_DENSE_MAX
        geo[d] = (dense, n_cls if dense else ATT_QB, n_cls if dense else ATT_KB)
    return geo


def _att_kernel(q_ref, k_ref, v_ref, g_ref, y_ref, mt_s, lt_s, nt_s, lsed_s, od_s,
                bias_s, biasd_s, p_s, pd_s, *, seq):
    geo = _att_geometry(seq)
    tiled = [d for d in DILATIONS if not geo[d][0]]
    dense = [d for d in DILATIONS if geo[d][0]]
    n_tiles = seq // ATT_TILE
    ones_kv = {kb: jnp.ones((kb, LANES), BF16) for kb in {geo[d][2] for d in DILATIONS}}

    def lanes(rows):
        lane = lax.broadcasted_iota(jnp.int32, (rows, LANES), 1)
        return lane < ATT_HD, _head_a_lanes(lane)

    def band_bias(qb, kb, delta):
        qa = lax.broadcasted_iota(jnp.int32, (2 * qb, kb), 0) & (qb - 1)
        ka = lax.broadcasted_iota(jnp.int32, (2 * qb, kb), 1)
        return jnp.where(jnp.abs(ka - qa + delta) <= HALF, 0.0, NEG)

    for idx, delta in enumerate((-HALF, 0, -2 * HALF)):
        bias_s[idx] = band_bias(ATT_QB, ATT_KB, delta)
    for d in dense:
        biasd_s[...] = band_bias(geo[d][1], geo[d][2], 0)

    def rows_of(start, size, d):
        return pl.ds(start, size) if d == 1 else pl.ds(start, size, stride=d)

    def score_block(d, r, a0, bias, p_ref):
        _, qb, kb = geo[d]
        head_a, qk_head_a = lanes(qb)
        ka0 = 0 if geo[d][0] else jnp.clip(a0 - HALF, 0, seq // d - kb)
        q2 = q_ref[rows_of(r + d * a0, qb, d), :]
        kk = k_ref[rows_of(r + d * ka0, kb, d), :].astype(BF16)
        qs = jnp.concatenate([jnp.where(qk_head_a, q2, 0.0), jnp.where(qk_head_a, 0.0, q2)], axis=0)
        s = _bdot_nt(qs, kk) + bias
        m = jnp.max(s, axis=-1, keepdims=True)
        p_ref[...] = jnp.exp2(s - m).astype(BF16)
        mb = jnp.broadcast_to(m, (2 * qb, LANES))
        return jnp.where(head_a, mb[:qb], mb[qb:])

    def value_block(d, r, a0, p_ref):
        _, qb, kb = geo[d]
        head_a, _ = lanes(qb)
        ka0 = 0 if geo[d][0] else jnp.clip(a0 - HALF, 0, seq // d - kb)
        vv = v_ref[rows_of(r + d * ka0, kb, d), :].astype(BF16)
        nl = jnp.dot(p_ref[...], jnp.concatenate([vv, ones_kv[kb]], axis=1), preferred_element_type=F32)
        return (jnp.where(head_a, nl[:qb, LANES:], nl[qb:, LANES:]),
                jnp.where(head_a, nl[:qb, :LANES], nl[qb:, :LANES]))

    for gi, d in enumerate(dense):
        def dense_scores(r, c, gi=gi, d=d):
            lsed_s[gi * d + r] = score_block(d, r, 0, biasd_s[...], pd_s.at[r])
            return c

        def dense_values(r, c, gi=gi, d=d):
            l, n = value_block(d, r, 0, pd_s.at[r])
            od_s[gi * d + r] = (n / l).astype(BF16)
            lsed_s[gi * d + r] = lsed_s[gi * d + r] + jnp.log2(l)
            return c
        lax.fori_loop(0, d, dense_scores, 0, unroll=ATT_DENSE_UNROLL)
        lax.fori_loop(0, d, dense_values, 0, unroll=ATT_DENSE_UNROLL)

    cb = 256

    def to_natural(d):
        i = lax.broadcasted_iota(jnp.int32, (cb, cb), 0)
        j = lax.broadcasted_iota(jnp.int32, (cb, cb), 1)
        return (j == (i % d) * (cb // d) + i // d).astype(BF16)

    def tile_body(tj, carry):
        tile0 = tj * ATT_TILE
        for gi, d in enumerate(tiled):
            n_cls = seq // d
            per_cls = ATT_TILE // d // ATT_QB
            n_blk = ATT_TILE // ATT_QB
            for bi in range(n_blk):
                r = bi // per_cls
                a0 = tile0 // d + (bi % per_cls) * ATT_QB
                which = jnp.where(a0 == 0, 1, jnp.where(a0 == n_cls - ATT_QB, 2, 0))
                mt_s[gi, rows_of(r + d * a0 - tile0, ATT_QB, d), :] = score_block(
                    d, r, a0, bias_s[which], p_s.at[bi])
            for bi in range(n_blk):
                r = bi // per_cls
                a0 = tile0 // d + (bi % per_cls) * ATT_QB
                dst = rows_of(r + d * a0 - tile0, ATT_QB, d)
                lt_s[gi, dst, :], nt_s[gi, dst, :] = value_block(d, r, a0, p_s.at[bi])

        def comb_body(ci, c):
            rows = pl.ds(pl.multiple_of(ci * cb, cb), cb)
            orow = pl.ds(pl.multiple_of(tile0 + ci * cb, cb), cb)
            parts = [(mt_s[gi, rows, :], lt_s[gi, rows, :], nt_s[gi, rows, :]) for gi in range(len(tiled))]
            for gi, d in enumerate(dense):
                per = cb // d
                a0 = pl.multiple_of((tile0 + ci * cb) // d, per)
                o_cls = jnp.concatenate([od_s[gi * d + r, pl.ds(a0, per), :] for r in range(d)], axis=0)
                lse_cls = jnp.concatenate([lsed_s[gi * d + r, pl.ds(a0, per), :] for r in range(d)], axis=0)
                perm = to_natural(d)
                o_nat = jnp.dot(perm, o_cls, preferred_element_type=F32)
                hi = lse_cls.astype(BF16)
                lo = (lse_cls - hi.astype(F32)).astype(BF16)
                lse2 = jnp.dot(perm, jnp.concatenate([hi, lo], axis=1), preferred_element_type=F32)
                parts.append((lse2[:, :LANES] + lse2[:, LANES:], None, o_nat))
            mx = functools.reduce(jnp.maximum, [part[0] for part in parts])
            num = jnp.zeros((cb, LANES), F32)
            den = jnp.zeros((cb, LANES), F32)
            for m, l, n in parts:
                w = jnp.exp2(m - mx)
                num = num + w * n
                den = den + (w if l is None else w * l)
            y_ref[orow, :] = ((num / den) * _silu(g_ref[orow, :])).astype(y_ref.dtype)
            return c
        lax.fori_loop(0, ATT_TILE // cb, comb_body, 0, unroll=True)
        return carry

    lax.fori_loop(0, n_tiles, tile_body, 0)


def _attention(z, batch, seq):
    t = batch * seq
    n_pairs = ATT_W // LANES

    def col(c0):
        return pl.BlockSpec((seq, LANES), lambda b, h, c0=c0: (b, c0 + h))

    geo = _att_geometry(seq)
    dense_geo = sorted({geo[d][1:] for d in DILATIONS if geo[d][0]})
    assert len(dense_geo) <= 1, "one band-mask table serves the dense groups"
    dqb, dkb = dense_geo[0] if dense_geo else (8, LANES)
    n_dense = sum(geo[d][0] for d in DILATIONS)
    tiled = pltpu.VMEM((max(len(DILATIONS) - n_dense, 1), ATT_TILE, LANES), F32)
    n_dense_cls = max(sum(d for d in DILATIONS if geo[d][0]), 1)
    return pl.pallas_call(
        functools.partial(_att_kernel, seq=seq),
        grid=(batch, n_pairs),
        in_specs=[col(COL_AQ), col(COL_AK), col(COL_AV), col(COL_AG)],
        out_specs=pl.BlockSpec((seq, LANES), lambda b, h: (b, h)),
        out_shape=jax.ShapeDtypeStruct((t, ATT_W), BF16),
        scratch_shapes=[tiled, tiled, tiled,
                        pltpu.VMEM((n_dense_cls, dqb, LANES), F32),
                        pltpu.VMEM((n_dense_cls, dqb, LANES), BF16),
                        pltpu.VMEM((3, 2 * ATT_QB, ATT_KB), F32),
                        pltpu.VMEM((2 * dqb, dkb), F32),
                        pltpu.VMEM((ATT_TILE // ATT_QB, 2 * ATT_QB, ATT_KB), BF16),
                        pltpu.VMEM((max(DILATIONS), 2 * dqb, dkb), BF16)],
        compiler_params=pltpu.CompilerParams(
            dimension_semantics=("parallel", "parallel"), vmem_limit_bytes=VMEM_LIMIT),
        name="band_attention",
    )(z, z, z, z)


def _chunk_cumsum(g, tri):
    c = HG_CHUNK
    g1 = g.astype(BF16)
    g2 = (g - g1.astype(F32)).astype(BF16)
    terms = jnp.concatenate([g1, g2], axis=1)
    k = g.shape[1]
    outs = []
    for i in range(g.shape[0] // c):
        part = jnp.dot(tri, terms[i * c:(i + 1) * c, :], preferred_element_type=F32)
        outs.append(part[:, k:] + part[:, :k])
    return jnp.concatenate(outs, axis=0)


def _hgrn_kernel(hq_ref, ff_ref, fb_ref, hv_ref, hg_ref, lbf_ref, lbb_ref, ghg_ref, y_ref,
                 q_s, kf_s, kb_s, bf_s, bb_s, o_s,
                 qtf_s, ktf_s, qtb_s, ktb_s, v16_s,
                 emf_s, emb_s, stf_s, stb_s, *, seq, layer):
    C = HG_CHUNK
    n_chunks = seq // C

    def lower_bound(ref):
        a = ref[...]
        e = jnp.exp(a - jnp.max(a, axis=0, keepdims=True))
        return jnp.sum(e[0:layer + 1, :], axis=0, keepdims=True) / jnp.sum(e, axis=0, keepdims=True)

    lbf = lower_bound(lbf_ref)
    lbb = lower_bound(lbb_ref)

    ti = lax.broadcasted_iota(jnp.int32, (C, C), 0)
    si = lax.broadcasted_iota(jnp.int32, (C, C), 1)
    tri_f = si <= ti
    tri_b = si >= ti

    pr = HG_PRE_ROWS
    n_blocks = seq // pr
    cpb = pr // C
    fwd = (ff_ref, lbf, kf_s, bf_s, qtf_s, ktf_s, emf_s, False)
    bwd = (fb_ref, lbb, kb_s, bb_s, qtb_s, ktb_s, emb_s, True)

    def per_chunk_rows(x3):
        return jnp.broadcast_to(x3, (cpb, C, HG_DK)).reshape(pr, HG_DK)

    def block_rows(i):
        return pl.ds(pl.multiple_of(i * pr, pr), pr)

    def prepare_common(i):
        rows = block_rows(i)
        q_s[rows, :] = _silu(hq_ref[rows, :])
        o_s[rows, :] = jnp.zeros((pr, HG_DK), F32)
        v16_s[rows, :] = hv_ref[rows, :].astype(BF16)

    def prepare_direction(i, d, lo):
        f_ref, lb, k_s, b_s, qt_s, kt_s, em_s, reverse = d
        rows = block_rows(i)
        q = q_s[rows, :]
        f = lb + (1.0 - lb) * _sigmoid(f_ref[rows, :])
        kk = 1.0 - f
        tri = tri_b if reverse else tri_f
        b = _chunk_cumsum(jnp.log(f), tri.astype(BF16))
        last = 0 if reverse else C - 1
        btot = b.reshape(cpb, C, HG_DK)[:, last:last + 1, :]
        mid = 0.5 * btot
        midr = per_chunk_rows(mid)
        k_s[rows, :] = kk
        b_s[rows, :] = b
        qt_s[rows, :] = (q * jnp.exp(b - midr)).astype(BF16)
        kt_s[rows, :] = (kk * jnp.exp(midr - b)).astype(BF16)
        em_s[pl.ds(pl.multiple_of(i * cpb, cpb), cpb), :] = jnp.exp(mid).reshape(cpb, HG_DK)
        return jnp.minimum(lo, btot.reshape(cpb, HG_DK))

    def chunk(ci, tri, d, st_ref, exact):
        _, _, k_s, b_s, qt_s, kt_s, em_s, reverse = d
        r0 = pl.multiple_of(ci * C, C)
        rows = pl.ds(r0, C)
        v = v16_s[rows, :]
        st = st_ref[...]
        if exact:
            q = q_s[rows, :]
            b = b_s[rows, :]
            kk = k_s[rows, :]
            btot = b_s[pl.ds(r0 + (0 if reverse else C - 1), 1), :]

            def col_body(s, att):
                w = q * k_s[pl.ds(r0 + s, 1), :] * jnp.exp(jnp.minimum(b - b_s[pl.ds(r0 + s, 1), :], 0.0))
                return att + jnp.where(si == s, jnp.sum(w, axis=-1, keepdims=True), 0.0)
            att = lax.fori_loop(0, C, col_body, jnp.zeros((C, C), F32))
            o = _bdot(jnp.where(tri, att, 0.0), v) + _bdot_nt(q * jnp.exp(b), st)
            st_ref[...] = st * jnp.exp(btot) + _bdot_tn(v, kk * jnp.exp(btot - b))
        else:
            em = em_s[pl.ds(ci, 1), :]
            qt = qt_s[rows, :]
            kt = kt_s[rows, :]
            att = jnp.where(tri, _bdot_nt(qt, kt), 0.0)
            o = _bdot(att, v) + _bdot_nt(qt, st * em)
            st_ref[...] = st * (em * em) + _bdot_tn(v, kt) * em
        o_s[rows, :] = o_s[rows, :] + o

    def finish_rows(i):
        rows = block_rows(i)
        o = o_s[rows, :]
        ms = jnp.mean(o * o, axis=-1, keepdims=True)
        y = (o * lax.rsqrt(ms + EPS)) * ghg_ref[...]
        y_ref[rows, :] = (y * _silu(hg_ref[rows, :])).astype(y_ref.dtype)

    def factorised_ok(lo):
        return jnp.min(lo) >= HG_FAST_MIN_LOGDECAY

    lo0 = jnp.zeros((cpb, HG_DK), F32)
    stf_s[...] = jnp.zeros_like(stf_s)
    stb_s[...] = jnp.zeros_like(stb_s)

    def first_pass(i, lo):
        prepare_common(i)
        return prepare_direction(i, fwd, lo)
    lo_f = lax.fori_loop(0, n_blocks, first_pass, lo0)

    def second_pass(exact):
        def body(i, lo):
            lo = prepare_direction(i, bwd, lo)
            for c in range(cpb):
                chunk(i * cpb + c, tri_f, fwd, stf_s, exact)
            return lo
        return lax.fori_loop(0, n_blocks, body, lo0, unroll=1 if exact else HG_BLOCK_UNROLL)
    lo_b = lax.cond(factorised_ok(lo_f), lambda: second_pass(False), lambda: second_pass(True))

    def third_pass(exact):
        def body(i, carry):
            ib = n_blocks - 1 - i
            for c in reversed(range(cpb)):
                chunk(ib * cpb + c, tri_b, bwd, stb_s, exact)
            finish_rows(ib)
            return carry
        lax.fori_loop(0, n_blocks, body, 0, unroll=1 if exact else HG_BLOCK_UNROLL)
    lax.cond(factorised_ok(lo_b), lambda: third_pass(False), lambda: third_pass(True))


def _hgrn(z, lb_fwd, lb_bwd, g_hg, batch, seq, layer):
    t = batch * seq

    def col(c0):
        return pl.BlockSpec((seq, LANES), lambda b, h, c0=c0: (b, c0 + h))

    n_lb = lb_fwd.shape[0]
    lb_spec = pl.BlockSpec((n_lb, HG_DK), lambda b, h: (0, h))
    seq_f32 = pltpu.VMEM((seq, HG_DK), F32)
    seq_bf16 = pltpu.VMEM((seq, HG_DK), BF16)
    chunk_f32 = pltpu.VMEM((seq // HG_CHUNK, HG_DK), F32)
    state = pltpu.VMEM((HG_DK, HG_DK), F32)
    return pl.pallas_call(
        functools.partial(_hgrn_kernel, seq=seq, layer=layer),
        grid=(batch, HG_HEADS),
        in_specs=[col(COL_HQ), col(COL_HFF), col(COL_HFB), col(COL_HV), col(COL_HG),
                  lb_spec, lb_spec, pl.BlockSpec((1, HG_DK), lambda b, h: (0, 0))],
        out_specs=pl.BlockSpec((seq, HG_DK), lambda b, h: (b, h)),
        out_shape=jax.ShapeDtypeStruct((t, HG_W), BF16),
        scratch_shapes=[seq_f32] * 6 + [seq_bf16] * 5 + [chunk_f32, chunk_f32, state, state],
        compiler_params=pltpu.CompilerParams(
            dimension_semantics=("parallel", "parallel"), vmem_limit_bytes=VMEM_LIMIT),
        name="hgrn2",
    )(z, z, z, z, z, lb_fwd, lb_bwd, g_hg.reshape(1, HG_DK))


def _rms(y, g):
    ms = jnp.mean(y * y, axis=-1, keepdims=True)
    return (y * lax.rsqrt(ms + EPS)) * g


def _outproj_kernel(x_ref, ya_ref, yh_ref, p_ref, wo_ref, wpg_ref, wpp_ref, gpost_ref, gple_ref, o_ref):
    sub = x_ref.shape[0] // OUT_SUBBLOCKS
    for r in range(OUT_SUBBLOCKS):
        rows = slice(r * sub, (r + 1) * sub)
        y = (jnp.dot(ya_ref[rows, :], wo_ref[0:ATT_W, :], preferred_element_type=F32)
             + jnp.dot(yh_ref[rows, :], wo_ref[ATT_W:, :], preferred_element_type=F32))
        h1 = x_ref[rows, :] + _rms(y, gpost_ref[...])
        gate = jax.nn.sigmoid(jnp.dot(h1.astype(BF16), wpg_ref[...], preferred_element_type=F32))
        e = jnp.dot(p_ref[rows, :].astype(BF16), wpp_ref[...], preferred_element_type=F32)
        o_ref[rows, :] = h1 + _rms(gate * e, gple_ref[...])


def _outproj(x2, y_att, y_hg, p2, w_out, w_pg, w_pp, g_post, g_ple):
    t = x2.shape[0]
    tm = min(OUT_TM, t)
    row = lambda i: (i, 0)
    const = lambda i: (0, 0)
    resident = dict(pipeline_mode=pl.Buffered(1))
    return pl.pallas_call(
        _outproj_kernel,
        grid=(t // tm,),
        in_specs=[pl.BlockSpec((tm, D_MODEL), row),
                  pl.BlockSpec((tm, ATT_W), row),
                  pl.BlockSpec((tm, HG_W), row),
                  pl.BlockSpec((tm, PLE_DIM), row),
                  pl.BlockSpec((ATT_W + HG_W, D_MODEL), const, **resident),
                  pl.BlockSpec((D_MODEL, D_MODEL), const, **resident),
                  pl.BlockSpec((PLE_DIM, D_MODEL), const, **resident),
                  pl.BlockSpec((1, D_MODEL), const),
                  pl.BlockSpec((1, D_MODEL), const)],
        out_specs=pl.BlockSpec((tm, D_MODEL), row),
        out_shape=jax.ShapeDtypeStruct((t, D_MODEL), F32),
        compiler_params=pltpu.CompilerParams(
            dimension_semantics=("parallel",), vmem_limit_bytes=VMEM_LIMIT),
        name="outproj_ple",
    )(x2, y_att, y_hg, p2, w_out, w_pg, w_pp, g_post.reshape(1, D_MODEL), g_ple.reshape(1, D_MODEL))


def kernel(x, p, positions, w_in, w_out, g_pre, g_post, g_hg, lb_fwd, lb_bwd, w_pg, w_pp, g_ple):
    depth = w_in.shape[0]
    batch, seq, _ = x.shape
    assert seq % ATT_TILE == 0 and seq // max(DILATIONS) >= ATT_KB
    t = batch * seq
    cos_t, sin_t = _rope_tables(positions)
    h = x.reshape(t, D_MODEL)
    for i in range(depth):
        z, (w_out16, w_pg16, w_pp16) = _inproj(h, g_pre[i], w_in[i], cos_t, sin_t,
                                               (w_out[i], w_pg[i], w_pp[i]))
        y_att = _attention(z, batch, seq)
        y_hg = _hgrn(z, lb_fwd, lb_bwd, g_hg[i], batch, seq, i)
        h = _outproj(h, y_att, y_hg, p[i].reshape(t, PLE_DIM), w_out16, w_pg16, w_pp16,
                     g_post[i], g_ple[i])
    return h.reshape(batch, seq, D_MODEL)
```

```python
import functools

import jax
import jax.numpy as jnp
from jax import lax
from jax.experimental import pallas as pl
from jax.experimental.pallas import tpu as pltpu

F32 = jnp.float32
BF16 = jnp.bfloat16

D_MODEL = 2048
PLE_DIM = 256
ATT_HEADS = 16
ATT_HD = 64
ATT_W = ATT_HEADS * ATT_HD
ROT_DIM = ATT_HD // 4
ROPE_THETA = 500000.0
DILATION_CFG = ((128, 1), (512, 4), (2048, 16))
DILATIONS = tuple(d for _, d in DILATION_CFG)
HALF = (DILATION_CFG[0][0] // 2) // DILATION_CFG[0][1]
assert all((w // 2) // d == HALF for w, d in DILATION_CFG)
HG_HEADS = 8
HG_DK = 128
HG_W = HG_HEADS * HG_DK
HG_CHUNK = 128
IN_W = 4 * ATT_W + 5 * HG_W
EPS = 1e-6
NEG = -1e30
LOG2E = 1.4426950408889634

LANES = 128
VMEM_LIMIT = 56 * 1024 * 1024

COL_AQ, COL_AK, COL_AV, COL_AG = (i * ATT_W // LANES for i in range(4))
COL_HQ, COL_HFF, COL_HFB, COL_HV, COL_HG = (4 * ATT_W // LANES + i * HG_W // LANES for i in range(5))

INPROJ_PREP_TN = 512
N_LATE_WEIGHTS = 3
LATE_WEIGHT_BLOCKS = 16
INPROJ_TM = 1024
INPROJ_TN = 1024
INPROJ_ROPE_CHUNK = 256
ATT_TILE = 1024
ATT_QB = 2 * HALF
ATT_KB = ATT_QB + 2 * HALF
ATT_DENSE_MAX = 256
ATT_DENSE_UNROLL = 8
HG_PRE_ROWS = 512
OUT_TM = 512
OUT_SUBBLOCKS = 2
HG_FAST_MIN_LOGDECAY = -120.0


def _bdot(a, b):
    return jnp.dot(a.astype(BF16), b.astype(BF16), preferred_element_type=F32)


def _bdot_nt(a, b):
    return lax.dot_general(a.astype(BF16), b.astype(BF16), (((1,), (1,)), ((), ())),
                           preferred_element_type=F32)


def _bdot_tn(a, b):
    return lax.dot_general(a.astype(BF16), b.astype(BF16), (((0,), (0,)), ((), ())),
                           preferred_element_type=F32)


def _sigmoid(x):
    return 0.5 * jnp.tanh(0.5 * x) + 0.5


def _silu(x):
    return x * _sigmoid(x)


def _aligned(x, m):
    return x if isinstance(x, int) else pl.multiple_of(x, m)


def _pair_swap_matrix():
    h = ROT_DIM // 2
    src = lax.broadcasted_iota(jnp.int32, (LANES, LANES), 0)
    dst = lax.broadcasted_iota(jnp.int32, (LANES, LANES), 1)
    shift = ATT_HD - h
    sigma = jnp.where((dst >= h) & (dst < ROT_DIM), dst + shift,
                      jnp.where((dst >= ATT_HD) & (dst < ATT_HD + h), dst - shift, dst))
    return (src == sigma).astype(BF16)


def _cast_weight_tile(w_ref, o_refs, j, swap_tiles, rb=256):
    n_rows, tn = w_ref.shape

    @pl.when(j < swap_tiles)
    def _():
        perm = _pair_swap_matrix()

        def body(i, carry):
            rows = pl.ds(pl.multiple_of(i * rb, rb), rb)
            for cb in range(tn // LANES):
                cols = slice(cb * LANES, (cb + 1) * LANES)
                wb = jnp.dot(w_ref[rows, cols].astype(BF16), perm, preferred_element_type=F32).astype(BF16)
                for o_ref in o_refs:
                    o_ref[rows, cols] = wb
            return carry
        lax.fori_loop(0, n_rows // rb, body, 0, unroll=True)

    @pl.when(j >= swap_tiles)
    def _():
        def body(i, carry):
            rows = pl.ds(pl.multiple_of(i * rb, rb), rb)
            wb = w_ref[rows, :].astype(BF16)
            for o_ref in o_refs:
                o_ref[rows, :] = wb
            return carry
        lax.fori_loop(0, n_rows // rb, body, 0)


def _head_a_lanes(lane):
    h = ROT_DIM // 2
    return (lane < h) | ((lane >= ROT_DIM) & (lane < ATT_HD + h))


ROPE_FREQS = ROT_DIM // 2
ROPE_GROUPS = LANES // ROPE_FREQS


def _split3(x):
    t1 = x.astype(BF16)
    r1 = x - t1.astype(F32)
    t2 = r1.astype(BF16)
    return t1, t2, (r1 - t2.astype(F32)).astype(BF16)


def _rope_table_kernel(pos_ref, inv_ref, cos_ref, sin_ref, c_s, s_s):
    p = pl.program_id(0)

    @pl.when(p == 0)
    def _():
        ang = pos_ref[...].astype(F32) * inv_ref[...]
        c_s[...] = jnp.concatenate(_split3(jnp.cos(ang)), axis=1)
        s_s[...] = jnp.concatenate(_split3(jnp.sin(ang)), axis=1)

    src = lax.broadcasted_iota(jnp.int32, (3 * LANES, LANES), 0) & (LANES - 1)
    dst = lax.broadcasted_iota(jnp.int32, (3 * LANES, LANES), 1)
    spread = (src == p * ROPE_FREQS + (dst & (ROPE_FREQS - 1))).astype(BF16)
    c = jnp.dot(c_s[...], spread, preferred_element_type=F32)
    s = jnp.dot(s_s[...], spread, preferred_element_type=F32)
    lane = lax.broadcasted_iota(jnp.int32, c.shape, 1)
    is_x1 = lane < ROT_DIM
    is_x2 = (lane >= ATT_HD) & (lane < ATT_HD + ROT_DIM)
    cos_ref[...] = jnp.where(is_x1 | is_x2, c, 1.0)
    sin_ref[...] = jnp.where(is_x1, -s, jnp.where(is_x2, s, 0.0))


def _rope_tables(positions):
    t = positions.size
    n = t // ROPE_GROUPS
    pos_c = jnp.repeat(positions.reshape(ROPE_GROUPS, n).T, ROPE_FREQS, axis=1)
    inv = jnp.power(ROPE_THETA, -jnp.arange(0, ROT_DIM, 2, dtype=F32) / ROT_DIM)
    inv_lane = jnp.tile(inv, ROPE_GROUPS).reshape(1, LANES)
    out = jax.ShapeDtypeStruct((t, LANES), F32)
    return pl.pallas_call(
        _rope_table_kernel,
        grid=(ROPE_GROUPS,),
        in_specs=[pl.BlockSpec((n, LANES), lambda p: (0, 0)),
                  pl.BlockSpec((1, LANES), lambda p: (0, 0))],
        out_specs=[pl.BlockSpec((n, LANES), lambda p: (p, 0))] * 2,
        out_shape=[out] * 2,
        scratch_shapes=[pltpu.VMEM((n, 3 * LANES), BF16)] * 2,
        compiler_params=pltpu.CompilerParams(dimension_semantics=("arbitrary",)),
        name="rope_tables",
    )(pos_c, inv_lane)


def _inproj_kernel(*refs, tm, tn, prepare_weights):
    j = pl.program_id(1)
    rb = 128
    n_q = ATT_W // tn
    n_rope = 2 * ATT_W // tn
    if prepare_weights:
        x_ref, g_ref, w32_ref, cos_ref, sin_ref = refs[:5]
        late32 = refs[5:5 + N_LATE_WEIGHTS]
        z_ref, wout_ref = refs[5 + N_LATE_WEIGHTS:7 + N_LATE_WEIGHTS]
        late16 = refs[7 + N_LATE_WEIGHTS:7 + 2 * N_LATE_WEIGHTS]
        u_ref, w_ref = refs[7 + 2 * N_LATE_WEIGHTS:]
        _cast_weight_tile(w32_ref, (w_ref, wout_ref), j, n_rope)
        for src, dst in zip(late32, late16):
            dst[...] = src[...].astype(BF16)
    else:
        x_ref, g_ref, w_ref, cos_ref, sin_ref, _, z_ref, u_ref = refs

    def norm_rows(r0, n):
        for i in range(n // rb):
            rows = slice(r0 + i * rb, r0 + (i + 1) * rb)
            xb = x_ref[rows, :]
            ms = jnp.mean(xb * xb, axis=-1, keepdims=True)
            u_ref[rows, :] = ((xb * lax.rsqrt(ms + EPS)) * g_ref[...]).astype(BF16)

    def project_rotate(r0, n):
        rows = slice(r0, r0 + n)
        scale = jnp.where(j < n_q, ATT_HD ** -0.5 * LOG2E, 1.0).astype(F32)
        c = cos_ref[rows, :] * scale
        s = sin_ref[rows, :] * scale
        for cb in range(tn // INPROJ_ROPE_CHUNK):
            cols = slice(cb * INPROJ_ROPE_CHUNK, (cb + 1) * INPROJ_ROPE_CHUNK)
            zc = jnp.dot(u_ref[rows, :], w_ref[:, cols], preferred_element_type=F32)
            parts = [zc[:, b * LANES:(b + 1) * LANES] for b in range(INPROJ_ROPE_CHUNK // LANES)]
            z_ref[rows, cols] = jnp.concatenate(
                [blk * c + pltpu.roll(blk, LANES // 2, 1) * s for blk in parts], axis=1)

    @pl.when(j == 0)
    def _():
        for r0 in range(0, tm, tm // 2):
            norm_rows(r0, tm // 2)
            project_rotate(r0, tm // 2)

    @pl.when((j > 0) & (j < n_rope))
    def _():
        project_rotate(0, tm)

    @pl.when(j >= n_rope)
    def _():
        z_ref[...] = jnp.dot(u_ref[...], w_ref[...], preferred_element_type=F32)


def _inproj(x2, g_pre, w_in, cos_t, sin_t, late_weights):
    assert len(late_weights) == N_LATE_WEIGHTS
    t = x2.shape[0]
    tm = min(INPROJ_TM, t)
    g2 = g_pre.reshape(1, D_MODEL)
    z_shape = jax.ShapeDtypeStruct((t, IN_W), F32)
    params = pltpu.CompilerParams(dimension_semantics=("parallel", "arbitrary"), vmem_limit_bytes=VMEM_LIMIT)
    u_scratch = pltpu.VMEM((tm, D_MODEL), BF16)

    def specs(tn, first):
        row = lambda i, j: (i + first, 0)
        return ([pl.BlockSpec((tm, D_MODEL), row),
                 pl.BlockSpec((1, D_MODEL), lambda i, j: (0, 0)),
                 pl.BlockSpec((D_MODEL, tn), lambda i, j: (0, j)),
                 pl.BlockSpec((tm, LANES), row),
                 pl.BlockSpec((tm, LANES), row)],
                pl.BlockSpec((tm, tn), lambda i, j: (i + first, j)))

    tn = INPROJ_PREP_TN
    n_steps = IN_W // tn
    in_specs, z_spec = specs(tn, 0)
    assert n_steps >= LATE_WEIGHT_BLOCKS
    late_specs = [pl.BlockSpec((w.shape[0] // LATE_WEIGHT_BLOCKS, w.shape[1]),
                               lambda i, j: (jnp.minimum(j, LATE_WEIGHT_BLOCKS - 1), 0)) for w in late_weights]
    outs = pl.pallas_call(
        functools.partial(_inproj_kernel, tm=tm, tn=tn, prepare_weights=True),
        grid=(1, n_steps),
        in_specs=in_specs + late_specs,
        out_specs=[z_spec, pl.BlockSpec((D_MODEL, tn), lambda i, j: (0, j))] + late_specs,
        out_shape=[z_shape, jax.ShapeDtypeStruct((D_MODEL, IN_W), BF16)]
        + [jax.ShapeDtypeStruct(w.shape, BF16) for w in late_weights],
        scratch_shapes=[u_scratch, pltpu.VMEM((D_MODEL, tn), BF16)],
        compiler_params=params,
        name="inproj_first",
    )(x2, g2, w_in, cos_t, sin_t, *late_weights)
    z, w_bf16, late_bf16 = outs[0], outs[1], outs[2:]
    if t == tm:
        return z, late_bf16
    tn = INPROJ_TN
    in_specs, z_spec = specs(tn, 1)
    z = pl.pallas_call(
        functools.partial(_inproj_kernel, tm=tm, tn=tn, prepare_weights=False),
        grid=(t // tm - 1, IN_W // tn),
        in_specs=in_specs + [pl.BlockSpec(memory_space=pl.ANY)],
        out_specs=z_spec,
        out_shape=z_shape,
        input_output_aliases={len(in_specs): 0},
        scratch_shapes=[u_scratch],
        compiler_params=params,
        name="inproj",
    )(x2, g2, w_bf16, cos_t, sin_t, z)
    return z, late_bf16


def _att_geometry(seq):
    geo = {}
    for d in DILATIONS:
        n_cls = seq // d
        dense = n_cls <= ATT_DENSE_MAX
        geo[d] = (dense, n_cls if dense else ATT_QB, n_cls if dense else ATT_KB)
    return geo


def _att_kernel(q_ref, k_ref, v_ref, g_ref, y_ref, mt_s, lt_s, nt_s, lsed_s, od_s,
                bias_s, biasd_s, p_s, pd_s, *, seq):
    geo = _att_geometry(seq)
    tiled = [d for d in DILATIONS if not geo[d][0]]
    dense = [d for d in DILATIONS if geo[d][0]]
    n_tiles = seq // ATT_TILE
    ones_kv = {kb: jnp.ones((kb, LANES), BF16) for kb in {geo[d][2] for d in DILATIONS}}

    def lanes(rows):
        lane = lax.broadcasted_iota(jnp.int32, (rows, LANES), 1)
        return lane < ATT_HD, _head_a_lanes(lane)

    def band_bias(qb, kb, delta):
        qa = lax.broadcasted_iota(jnp.int32, (2 * qb, kb), 0) & (qb - 1)
        ka = lax.broadcasted_iota(jnp.int32, (2 * qb, kb), 1)
        return jnp.where(jnp.abs(ka - qa + delta) <= HALF, 0.0, NEG)

    for idx, delta in enumerate((-HALF, 0, -2 * HALF)):
        bias_s[idx] = band_bias(ATT_QB, ATT_KB, delta)
    for d in dense:
        biasd_s[...] = band_bias(geo[d][1], geo[d][2], 0)

    def rows_of(start, size, d):
        return pl.ds(start, size) if d == 1 else pl.ds(start, size, stride=d)

    def score_block(d, r, a0, bias, p_ref):
        _, qb, kb = geo[d]
        head_a, qk_head_a = lanes(qb)
        ka0 = 0 if geo[d][0] else jnp.clip(a0 - HALF, 0, seq // d - kb)
        q2 = q_ref[rows_of(r + d * a0, qb, d), :]
        kk = k_ref[rows_of(r + d * ka0, kb, d), :].astype(BF16)
        qs = jnp.concatenate([jnp.where(qk_head_a, q2, 0.0), jnp.where(qk_head_a, 0.0, q2)], axis=0)
        s = _bdot_nt(qs, kk) + bias
        m = jnp.max(s, axis=-1, keepdims=True)
        p_ref[...] = jnp.exp2(s - m).astype(BF16)
        mb = jnp.broadcast_to(m, (2 * qb, LANES))
        return jnp.where(head_a, mb[:qb], mb[qb:])

    def value_block(d, r, a0, p_ref):
        _, qb, kb = geo[d]
        head_a, _ = lanes(qb)
        ka0 = 0 if geo[d][0] else jnp.clip(a0 - HALF, 0, seq // d - kb)
        vv = v_ref[rows_of(r + d * ka0, kb, d), :].astype(BF16)
        nl = jnp.dot(p_ref[...], jnp.concatenate([vv, ones_kv[kb]], axis=1), preferred_element_type=F32)
        return (jnp.where(head_a, nl[:qb, LANES:], nl[qb:, LANES:]),
                jnp.where(head_a, nl[:qb, :LANES], nl[qb:, :LANES]))

    for gi, d in enumerate(dense):
        def dense_scores(r, c, gi=gi, d=d):
            lsed_s[gi * d + r] = score_block(d, r, 0, biasd_s[...], pd_s.at[r])
            return c

        def dense_values(r, c, gi=gi, d=d):
            l, n = value_block(d, r, 0, pd_s.at[r])
            od_s[gi * d + r] = (n / l).astype(BF16)
            lsed_s[gi * d + r] = lsed_s[gi * d + r] + jnp.log2(l)
            return c
        lax.fori_loop(0, d, dense_scores, 0, unroll=ATT_DENSE_UNROLL)
        lax.fori_loop(0, d, dense_values, 0, unroll=ATT_DENSE_UNROLL)

    cb = 256

    def to_natural(d):
        i = lax.broadcasted_iota(jnp.int32, (cb, cb), 0)
        j = lax.broadcasted_iota(jnp.int32, (cb, cb), 1)
        return (j == (i % d) * (cb // d) + i // d).astype(BF16)

    def tile_body(tj, carry):
        tile0 = tj * ATT_TILE
        for gi, d in enumerate(tiled):
            n_cls = seq // d
            per_cls = ATT_TILE // d // ATT_QB
            n_blk = ATT_TILE // ATT_QB
            for bi in range(n_blk):
                r = bi // per_cls
                a0 = tile0 // d + (bi % per_cls) * ATT_QB
                which = jnp.where(a0 == 0, 1, jnp.where(a0 == n_cls - ATT_QB, 2, 0))
                mt_s[gi, rows_of(r + d * a0 - tile0, ATT_QB, d), :] = score_block(
                    d, r, a0, bias_s[which], p_s.at[bi])
            for bi in range(n_blk):
                r = bi // per_cls
                a0 = tile0 // d + (bi % per_cls) * ATT_QB
                dst = rows_of(r + d * a0 - tile0, ATT_QB, d)
                lt_s[gi, dst, :], nt_s[gi, dst, :] = value_block(d, r, a0, p_s.at[bi])

        def comb_body(ci, c):
            rows = pl.ds(pl.multiple_of(ci * cb, cb), cb)
            orow = pl.ds(pl.multiple_of(tile0 + ci * cb, cb), cb)
            parts = [(mt_s[gi, rows, :], lt_s[gi, rows, :], nt_s[gi, rows, :]) for gi in range(len(tiled))]
            for gi, d in enumerate(dense):
                per = cb // d
                a0 = pl.multiple_of((tile0 + ci * cb) // d, per)
                o_cls = jnp.concatenate([od_s[gi * d + r, pl.ds(a0, per), :] for r in range(d)], axis=0)
                lse_cls = jnp.concatenate([lsed_s[gi * d + r, pl.ds(a0, per), :] for r in range(d)], axis=0)
                perm = to_natural(d)
                o_nat = jnp.dot(perm, o_cls, preferred_element_type=F32)
                hi = lse_cls.astype(BF16)
                lo = (lse_cls - hi.astype(F32)).astype(BF16)
                lse2 = jnp.dot(perm, jnp.concatenate([hi, lo], axis=1), preferred_element_type=F32)
                parts.append((lse2[:, :LANES] + lse2[:, LANES:], None, o_nat))
            mx = functools.reduce(jnp.maximum, [part[0] for part in parts])
            num = jnp.zeros((cb, LANES), F32)
            den = jnp.zeros((cb, LANES), F32)
            for m, l, n in parts:
                w = jnp.exp2(m - mx)
                num = num + w * n
                den = den + (w if l is None else w * l)
            y_ref[orow, :] = ((num / den) * _silu(g_ref[orow, :])).astype(y_ref.dtype)
            return c
        lax.fori_loop(0, ATT_TILE // cb, comb_body, 0, unroll=True)
        return carry

    lax.fori_loop(0, n_tiles, tile_body, 0)


def _attention(z, batch, seq):
    t = batch * seq
    n_pairs = ATT_W // LANES

    def col(c0):
        return pl.BlockSpec((seq, LANES), lambda b, h, c0=c0: (b, c0 + h))

    geo = _att_geometry(seq)
    dense_geo = sorted({geo[d][1:] for d in DILATIONS if geo[d][0]})
    assert len(dense_geo) <= 1, "one band-mask table serves the dense groups"
    dqb, dkb = dense_geo[0] if dense_geo else (8, LANES)
    n_dense = sum(geo[d][0] for d in DILATIONS)
    tiled = pltpu.VMEM((max(len(DILATIONS) - n_dense, 1), ATT_TILE, LANES), F32)
    n_dense_cls = max(sum(d for d in DILATIONS if geo[d][0]), 1)
    return pl.pallas_call(
        functools.partial(_att_kernel, seq=seq),
        grid=(batch, n_pairs),
        in_specs=[col(COL_AQ), col(COL_AK), col(COL_AV), col(COL_AG)],
        out_specs=pl.BlockSpec((seq, LANES), lambda b, h: (b, h)),
        out_shape=jax.ShapeDtypeStruct((t, ATT_W), BF16),
        scratch_shapes=[tiled, tiled, tiled,
                        pltpu.VMEM((n_dense_cls, dqb, LANES), F32),
                        pltpu.VMEM((n_dense_cls, dqb, LANES), BF16),
                        pltpu.VMEM((3, 2 * ATT_QB, ATT_KB), F32),
                        pltpu.VMEM((2 * dqb, dkb), F32),
                        pltpu.VMEM((ATT_TILE // ATT_QB, 2 * ATT_QB, ATT_KB), BF16),
                        pltpu.VMEM((max(DILATIONS), 2 * dqb, dkb), BF16)],
        compiler_params=pltpu.CompilerParams(
            dimension_semantics=("parallel", "parallel"), vmem_limit_bytes=VMEM_LIMIT),
        name="band_attention",
    )(z, z, z, z)


def _chunk_cumsum(g, tri):
    c = HG_CHUNK
    g1 = g.astype(BF16)
    g2 = (g - g1.astype(F32)).astype(BF16)
    terms = jnp.concatenate([g1, g2], axis=1)
    k = g.shape[1]
    outs = []
    for i in range(g.shape[0] // c):
        part = jnp.dot(tri, terms[i * c:(i + 1) * c, :], preferred_element_type=F32)
        outs.append(part[:, k:] + part[:, :k])
    return jnp.concatenate(outs, axis=0)


def _hgrn_kernel(hq_ref, ff_ref, fb_ref, hv_ref, hg_ref, lbf_ref, lbb_ref, ghg_ref, y_ref,
                 q_s, kf_s, kb_s, bf_s, bb_s, o_s,
                 qtf_s, ktf_s, qtb_s, ktb_s, v16_s,
                 emf_s, emb_s, stf_s, stb_s, *, seq, layer):
    C = HG_CHUNK
    n_chunks = seq // C

    def lower_bound(ref):
        a = ref[...]
        e = jnp.exp(a - jnp.max(a, axis=0, keepdims=True))
        return jnp.sum(e[0:layer + 1, :], axis=0, keepdims=True) / jnp.sum(e, axis=0, keepdims=True)

    lbf = lower_bound(lbf_ref)
    lbb = lower_bound(lbb_ref)

    ti = lax.broadcasted_iota(jnp.int32, (C, C), 0)
    si = lax.broadcasted_iota(jnp.int32, (C, C), 1)
    tri_f = si <= ti
    tri_b = si >= ti

    pr = HG_PRE_ROWS
    n_blocks = seq // pr
    cpb = pr // C
    fwd = (ff_ref, lbf, kf_s, bf_s, qtf_s, ktf_s, emf_s, False)
    bwd = (fb_ref, lbb, kb_s, bb_s, qtb_s, ktb_s, emb_s, True)

    def per_chunk_rows(x3):
        return jnp.broadcast_to(x3, (cpb, C, HG_DK)).reshape(pr, HG_DK)

    def block_rows(i):
        return pl.ds(_aligned(i * pr, pr), pr)

    def prepare_common(i):
        rows = block_rows(i)
        q_s[rows, :] = _silu(hq_ref[rows, :])
        o_s[rows, :] = jnp.zeros((pr, HG_DK), F32)
        v16_s[rows, :] = hv_ref[rows, :].astype(BF16)

    def prepare_direction(i, d, lo):
        f_ref, lb, k_s, b_s, qt_s, kt_s, em_s, reverse = d
        rows = block_rows(i)
        q = q_s[rows, :]
        f = lb + (1.0 - lb) * _sigmoid(f_ref[rows, :])
        kk = 1.0 - f
        tri = tri_b if reverse else tri_f
        b = _chunk_cumsum(jnp.log(f), tri.astype(BF16))
        last = 0 if reverse else C - 1
        btot = b.reshape(cpb, C, HG_DK)[:, last:last + 1, :]
        mid = 0.5 * btot
        midr = per_chunk_rows(mid)
        k_s[rows, :] = kk
        b_s[rows, :] = b
        qt_s[rows, :] = (q * jnp.exp(b - midr)).astype(BF16)
        kt_s[rows, :] = (kk * jnp.exp(midr - b)).astype(BF16)
        em_s[pl.ds(_aligned(i * cpb, cpb), cpb), :] = jnp.exp(mid).reshape(cpb, HG_DK)
        return jnp.minimum(lo, btot.reshape(cpb, HG_DK))

    def chunk(ci, tri, d, st_ref, exact):
        _, _, k_s, b_s, qt_s, kt_s, em_s, reverse = d
        r0 = _aligned(ci * C, C)
        rows = pl.ds(r0, C)
        v = v16_s[rows, :]
        st = st_ref[...]
        if exact:
            q = q_s[rows, :]
            b = b_s[rows, :]
            kk = k_s[rows, :]
            btot = b_s[pl.ds(r0 + (0 if reverse else C - 1), 1), :]

            def col_body(s, att):
                w = q * k_s[pl.ds(r0 + s, 1), :] * jnp.exp(jnp.minimum(b - b_s[pl.ds(r0 + s, 1), :], 0.0))
                return att + jnp.where(si == s, jnp.sum(w, axis=-1, keepdims=True), 0.0)
            att = lax.fori_loop(0, C, col_body, jnp.zeros((C, C), F32))
            o = _bdot(jnp.where(tri, att, 0.0), v) + _bdot_nt(q * jnp.exp(b), st)
            st_ref[...] = st * jnp.exp(btot) + _bdot_tn(v, kk * jnp.exp(btot - b))
        else:
            em = em_s[pl.ds(ci, 1), :]
            qt = qt_s[rows, :]
            kt = kt_s[rows, :]
            att = jnp.where(tri, _bdot_nt(qt, kt), 0.0)
            o = _bdot(att, v) + _bdot_nt(qt, st * em)
            st_ref[...] = st * (em * em) + _bdot_tn(v, kt) * em
        o_s[rows, :] = o_s[rows, :] + o

    def finish_rows(i):
        rows = block_rows(i)
        o = o_s[rows, :]
        ms = jnp.mean(o * o, axis=-1, keepdims=True)
        y = (o * lax.rsqrt(ms + EPS)) * ghg_ref[...]
        y_ref[rows, :] = (y * _silu(hg_ref[rows, :])).astype(y_ref.dtype)

    def factorised_ok(lo):
        return jnp.min(lo) >= HG_FAST_MIN_LOGDECAY

    lo0 = jnp.zeros((cpb, HG_DK), F32)
    stf_s[...] = jnp.zeros_like(stf_s)
    stb_s[...] = jnp.zeros_like(stb_s)

    lo = lo0
    prepare_common(0)
    lo = prepare_direction(0, fwd, lo)
    for i in range(n_blocks):
        if i + 1 < n_blocks:
            prepare_common(i + 1)
            lo = prepare_direction(i + 1, fwd, lo)
        for c in range(cpb):
            chunk(i * cpb + c, tri_f, fwd, stf_s, False)
    lo = prepare_direction(n_blocks - 1, bwd, lo)
    for i in reversed(range(n_blocks)):
        if i > 0:
            lo = prepare_direction(i - 1, bwd, lo)
        for c in reversed(range(cpb)):
            chunk(i * cpb + c, tri_b, bwd, stb_s, False)
        finish_rows(i)

    @pl.when(jnp.logical_not(factorised_ok(lo)))
    def _():
        stf_s[...] = jnp.zeros_like(stf_s)
        stb_s[...] = jnp.zeros_like(stb_s)

        def clear(i, carry):
            o_s[block_rows(i), :] = jnp.zeros((pr, HG_DK), F32)
            return carry
        lax.fori_loop(0, n_blocks, clear, 0)

        def both(n, carry):
            chunk(n, tri_f, fwd, stf_s, True)
            chunk(n_chunks - 1 - n, tri_b, bwd, stb_s, True)
            return carry
        lax.fori_loop(0, n_chunks, both, 0)

        def finish(i, carry):
            finish_rows(i)
            return carry
        lax.fori_loop(0, n_blocks, finish, 0)


def _hgrn(z, lb_fwd, lb_bwd, g_hg, batch, seq, layer):
    t = batch * seq

    def col(c0):
        return pl.BlockSpec((seq, LANES), lambda b, h, c0=c0: (b, c0 + h))

    n_lb = lb_fwd.shape[0]
    lb_spec = pl.BlockSpec((n_lb, HG_DK), lambda b, h: (0, h))
    seq_f32 = pltpu.VMEM((seq, HG_DK), F32)
    seq_bf16 = pltpu.VMEM((seq, HG_DK), BF16)
    chunk_f32 = pltpu.VMEM((seq // HG_CHUNK, HG_DK), F32)
    state = pltpu.VMEM((HG_DK, HG_DK), F32)
    return pl.pallas_call(
        functools.partial(_hgrn_kernel, seq=seq, layer=layer),
        grid=(batch, HG_HEADS),
        in_specs=[col(COL_HQ), col(COL_HFF), col(COL_HFB), col(COL_HV), col(COL_HG),
                  lb_spec, lb_spec, pl.BlockSpec((1, HG_DK), lambda b, h: (0, 0))],
        out_specs=pl.BlockSpec((seq, HG_DK), lambda b, h: (b, h)),
        out_shape=jax.ShapeDtypeStruct((t, HG_W), BF16),
        scratch_shapes=[seq_f32] * 6 + [seq_bf16] * 5 + [chunk_f32, chunk_f32, state, state],
        compiler_params=pltpu.CompilerParams(
            dimension_semantics=("parallel", "parallel"), vmem_limit_bytes=VMEM_LIMIT),
        name="hgrn2",
    )(z, z, z, z, z, lb_fwd, lb_bwd, g_hg.reshape(1, HG_DK))


def _rms(y, g):
    ms = jnp.mean(y * y, axis=-1, keepdims=True)
    return (y * lax.rsqrt(ms + EPS)) * g


def _outproj_kernel(x_ref, ya_ref, yh_ref, p_ref, wo_ref, wpg_ref, wpp_ref, gpost_ref, gple_ref, o_ref):
    sub = x_ref.shape[0] // OUT_SUBBLOCKS
    for r in range(OUT_SUBBLOCKS):
        rows = slice(r * sub, (r + 1) * sub)
        y = (jnp.dot(ya_ref[rows, :], wo_ref[0:ATT_W, :], preferred_element_type=F32)
             + jnp.dot(yh_ref[rows, :], wo_ref[ATT_W:, :], preferred_element_type=F32))
        h1 = x_ref[rows, :] + _rms(y, gpost_ref[...])
        gate = jax.nn.sigmoid(jnp.dot(h1.astype(BF16), wpg_ref[...], preferred_element_type=F32))
        e = jnp.dot(p_ref[rows, :].astype(BF16), wpp_ref[...], preferred_element_type=F32)
        o_ref[rows, :] = h1 + _rms(gate * e, gple_ref[...])


def _outproj(x2, y_att, y_hg, p2, w_out, w_pg, w_pp, g_post, g_ple):
    t = x2.shape[0]
    tm = min(OUT_TM, t)
    row = lambda i: (i, 0)
    const = lambda i: (0, 0)
    resident = dict(pipeline_mode=pl.Buffered(1))
    return pl.pallas_call(
        _outproj_kernel,
        grid=(t // tm,),
        in_specs=[pl.BlockSpec((tm, D_MODEL), row),
                  pl.BlockSpec((tm, ATT_W), row),
                  pl.BlockSpec((tm, HG_W), row),
                  pl.BlockSpec((tm, PLE_DIM), row),
                  pl.BlockSpec((ATT_W + HG_W, D_MODEL), const, **resident),
                  pl.BlockSpec((D_MODEL, D_MODEL), const, **resident),
                  pl.BlockSpec((PLE_DIM, D_MODEL), const, **resident),
                  pl.BlockSpec((1, D_MODEL), const),
                  pl.BlockSpec((1, D_MODEL), const)],
        out_specs=pl.BlockSpec((tm, D_MODEL), row),
        out_shape=jax.ShapeDtypeStruct((t, D_MODEL), F32),
        compiler_params=pltpu.CompilerParams(
            dimension_semantics=("parallel",), vmem_limit_bytes=VMEM_LIMIT),
        name="outproj_ple",
    )(x2, y_att, y_hg, p2, w_out, w_pg, w_pp, g_post.reshape(1, D_MODEL), g_ple.reshape(1, D_MODEL))


def kernel(x, p, positions, w_in, w_out, g_pre, g_post, g_hg, lb_fwd, lb_bwd, w_pg, w_pp, g_ple):
    depth = w_in.shape[0]
    batch, seq, _ = x.shape
    assert seq % ATT_TILE == 0 and seq // max(DILATIONS) >= ATT_KB
    t = batch * seq
    cos_t, sin_t = _rope_tables(positions)
    h = x.reshape(t, D_MODEL)
    for i in range(depth):
        z, (w_out16, w_pg16, w_pp16) = _inproj(h, g_pre[i], w_in[i], cos_t, sin_t,
                                               (w_out[i], w_pg[i], w_pp[i]))
        y_att = _attention(z, batch, seq)
        y_hg = _hgrn(z, lb_fwd, lb_bwd, g_hg[i], batch, seq, i)
        h = _outproj(h, y_att, y_hg, p[i].reshape(t, PLE_DIM), w_out16, w_pg16, w_pp16,
                     g_post[i], g_ple[i])
    return h.reshape(batch, seq, D_MODEL)
```

```python
import functools

import jax
import jax.numpy as jnp
from jax import lax
from jax.experimental import pallas as pl
from jax.experimental.pallas import tpu as pltpu

F32 = jnp.float32
BF16 = jnp.bfloat16

D_MODEL = 2048
PLE_DIM = 256
ATT_HEADS = 16
ATT_HD = 64
ATT_W = ATT_HEADS * ATT_HD
ROT_DIM = ATT_HD // 4
ROPE_THETA = 500000.0
DILATION_CFG = ((128, 1), (512, 4), (2048, 16))
DILATIONS = tuple(d for _, d in DILATION_CFG)
HALF = (DILATION_CFG[0][0] // 2) // DILATION_CFG[0][1]
assert all((w // 2) // d == HALF for w, d in DILATION_CFG)
HG_HEADS = 8
HG_DK = 128
HG_W = HG_HEADS * HG_DK
HG_CHUNK = 128
IN_W = 4 * ATT_W + 5 * HG_W
EPS = 1e-6
NEG = -1e30
LOG2E = 1.4426950408889634

LANES = 128
VMEM_LIMIT = 56 * 1024 * 1024

COL_AQ, COL_AK, COL_AV, COL_AG = (i * ATT_W // LANES for i in range(4))
COL_HQ, COL_HFF, COL_HFB, COL_HV, COL_HG = (4 * ATT_W // LANES + i * HG_W // LANES for i in range(5))

INPROJ_PREP_TN = 512
N_LATE_WEIGHTS = 3
LATE_WEIGHT_BLOCKS = 16
INPROJ_TM = 1024
INPROJ_TN = 1024
INPROJ_ROPE_CHUNK = 256
ATT_TILE = 1024
ATT_QB = 2 * HALF
ATT_KB = ATT_QB + 2 * HALF
ATT_DENSE_MAX = 256
ATT_DENSE_UNROLL = 8
HG_PRE_ROWS = 512
OUT_TM = 512
OUT_SUBBLOCKS = 2
HG_FAST_MIN_LOGDECAY = -120.0


def _bdot(a, b):
    return jnp.dot(a.astype(BF16), b.astype(BF16), preferred_element_type=F32)


def _bdot_nt(a, b):
    return lax.dot_general(a.astype(BF16), b.astype(BF16), (((1,), (1,)), ((), ())),
                           preferred_element_type=F32)


def _bdot_tn(a, b):
    return lax.dot_general(a.astype(BF16), b.astype(BF16), (((0,), (0,)), ((), ())),
                           preferred_element_type=F32)


def _sigmoid(x):
    return 0.5 * jnp.tanh(0.5 * x) + 0.5


def _silu(x):
    return x * _sigmoid(x)


def _aligned(x, m):
    return x if isinstance(x, int) else pl.multiple_of(x, m)


def _pair_swap_matrix():
    h = ROT_DIM // 2
    src = lax.broadcasted_iota(jnp.int32, (LANES, LANES), 0)
    dst = lax.broadcasted_iota(jnp.int32, (LANES, LANES), 1)
    shift = ATT_HD - h
    sigma = jnp.where((dst >= h) & (dst < ROT_DIM), dst + shift,
                      jnp.where((dst >= ATT_HD) & (dst < ATT_HD + h), dst - shift, dst))
    return (src == sigma).astype(BF16)


def _cast_weight_tile(w_ref, o_refs, j, swap_tiles, rb=256):
    n_rows, tn = w_ref.shape

    @pl.when(j < swap_tiles)
    def _():
        perm = _pair_swap_matrix()

        def body(i, carry):
            rows = pl.ds(pl.multiple_of(i * rb, rb), rb)
            for cb in range(tn // LANES):
                cols = slice(cb * LANES, (cb + 1) * LANES)
                wb = jnp.dot(w_ref[rows, cols].astype(BF16), perm, preferred_element_type=F32).astype(BF16)
                for o_ref in o_refs:
                    o_ref[rows, cols] = wb
            return carry
        lax.fori_loop(0, n_rows // rb, body, 0, unroll=True)

    @pl.when(j >= swap_tiles)
    def _():
        def body(i, carry):
            rows = pl.ds(pl.multiple_of(i * rb, rb), rb)
            wb = w_ref[rows, :].astype(BF16)
            for o_ref in o_refs:
                o_ref[rows, :] = wb
            return carry
        lax.fori_loop(0, n_rows // rb, body, 0)


def _head_a_lanes(lane):
    h = ROT_DIM // 2
    return (lane < h) | ((lane >= ROT_DIM) & (lane < ATT_HD + h))


ROPE_FREQS = ROT_DIM // 2
ROPE_GROUPS = LANES // ROPE_FREQS


def _split3(x):
    t1 = x.astype(BF16)
    r1 = x - t1.astype(F32)
    t2 = r1.astype(BF16)
    return t1, t2, (r1 - t2.astype(F32)).astype(BF16)


def _rope_table_kernel(pos_ref, inv_ref, cos_ref, sin_ref, c_s, s_s):
    p = pl.program_id(0)

    @pl.when(p == 0)
    def _():
        ang = pos_ref[...].astype(F32) * inv_ref[...]
        c_s[...] = jnp.concatenate(_split3(jnp.cos(ang)), axis=1)
        s_s[...] = jnp.concatenate(_split3(jnp.sin(ang)), axis=1)

    src = lax.broadcasted_iota(jnp.int32, (3 * LANES, LANES), 0) & (LANES - 1)
    dst = lax.broadcasted_iota(jnp.int32, (3 * LANES, LANES), 1)
    spread = (src == p * ROPE_FREQS + (dst & (ROPE_FREQS - 1))).astype(BF16)
    c = jnp.dot(c_s[...], spread, preferred_element_type=F32)
    s = jnp.dot(s_s[...], spread, preferred_element_type=F32)
    lane = lax.broadcasted_iota(jnp.int32, c.shape, 1)
    is_x1 = lane < ROT_DIM
    is_x2 = (lane >= ATT_HD) & (lane < ATT_HD + ROT_DIM)
    cos_ref[...] = jnp.where(is_x1 | is_x2, c, 1.0)
    sin_ref[...] = jnp.where(is_x1, -s, jnp.where(is_x2, s, 0.0))


def _rope_tables(positions):
    t = positions.size
    n = t // ROPE_GROUPS
    pos_c = jnp.repeat(positions.reshape(ROPE_GROUPS, n).T, ROPE_FREQS, axis=1)
    inv = jnp.power(ROPE_THETA, -jnp.arange(0, ROT_DIM, 2, dtype=F32) / ROT_DIM)
    inv_lane = jnp.tile(inv, ROPE_GROUPS).reshape(1, LANES)
    out = jax.ShapeDtypeStruct((t, LANES), F32)
    return pl.pallas_call(
        _rope_table_kernel,
        grid=(ROPE_GROUPS,),
        in_specs=[pl.BlockSpec((n, LANES), lambda p: (0, 0)),
                  pl.BlockSpec((1, LANES), lambda p: (0, 0))],
        out_specs=[pl.BlockSpec((n, LANES), lambda p: (p, 0))] * 2,
        out_shape=[out] * 2,
        scratch_shapes=[pltpu.VMEM((n, 3 * LANES), BF16)] * 2,
        compiler_params=pltpu.CompilerParams(dimension_semantics=("arbitrary",)),
        name="rope_tables",
    )(pos_c, inv_lane)


def _inproj_kernel(*refs, tm, tn, prepare_weights):
    j = pl.program_id(1)
    rb = 128
    n_q = ATT_W // tn
    n_rope = 2 * ATT_W // tn
    if prepare_weights:
        x_ref, g_ref, w32_ref, cos_ref, sin_ref = refs[:5]
        late32 = refs[5:5 + N_LATE_WEIGHTS]
        z_ref, wout_ref = refs[5 + N_LATE_WEIGHTS:7 + N_LATE_WEIGHTS]
        late16 = refs[7 + N_LATE_WEIGHTS:7 + 2 * N_LATE_WEIGHTS]
        u_ref, w_ref = refs[7 + 2 * N_LATE_WEIGHTS:]
        _cast_weight_tile(w32_ref, (w_ref, wout_ref), j, n_rope)
        for src, dst in zip(late32, late16):
            dst[...] = src[...].astype(BF16)
    else:
        x_ref, g_ref, w_ref, cos_ref, sin_ref, _, z_ref, u_ref = refs

    def norm_rows(r0, n):
        for i in range(n // rb):
            rows = slice(r0 + i * rb, r0 + (i + 1) * rb)
            xb = x_ref[rows, :]
            ms = jnp.mean(xb * xb, axis=-1, keepdims=True)
            u_ref[rows, :] = ((xb * lax.rsqrt(ms + EPS)) * g_ref[...]).astype(BF16)

    def project_rotate(r0, n):
        rows = slice(r0, r0 + n)
        scale = jnp.where(j < n_q, ATT_HD ** -0.5 * LOG2E, 1.0).astype(F32)
        c = cos_ref[rows, :] * scale
        s = sin_ref[rows, :] * scale
        for cb in range(tn // INPROJ_ROPE_CHUNK):
            cols = slice(cb * INPROJ_ROPE_CHUNK, (cb + 1) * INPROJ_ROPE_CHUNK)
            zc = jnp.dot(u_ref[rows, :], w_ref[:, cols], preferred_element_type=F32)
            parts = [zc[:, b * LANES:(b + 1) * LANES] for b in range(INPROJ_ROPE_CHUNK // LANES)]
            z_ref[rows, cols] = jnp.concatenate(
                [blk * c + pltpu.roll(blk, LANES // 2, 1) * s for blk in parts], axis=1)

    @pl.when(j == 0)
    def _():
        for r0 in range(0, tm, tm // 2):
            norm_rows(r0, tm // 2)
            project_rotate(r0, tm // 2)

    @pl.when((j > 0) & (j < n_rope))
    def _():
        project_rotate(0, tm)

    @pl.when(j >= n_rope)
    def _():
        z_ref[...] = jnp.dot(u_ref[...], w_ref[...], preferred_element_type=F32)


def _inproj(x2, g_pre, w_in, cos_t, sin_t, late_weights):
    assert len(late_weights) == N_LATE_WEIGHTS
    t = x2.shape[0]
    tm = min(INPROJ_TM, t)
    g2 = g_pre.reshape(1, D_MODEL)
    z_shape = jax.ShapeDtypeStruct((t, IN_W), F32)
    params = pltpu.CompilerParams(dimension_semantics=("parallel", "arbitrary"), vmem_limit_bytes=VMEM_LIMIT)
    u_scratch = pltpu.VMEM((tm, D_MODEL), BF16)

    def specs(tn, first):
        row = lambda i, j: (i + first, 0)
        return ([pl.BlockSpec((tm, D_MODEL), row),
                 pl.BlockSpec((1, D_MODEL), lambda i, j: (0, 0)),
                 pl.BlockSpec((D_MODEL, tn), lambda i, j: (0, j)),
                 pl.BlockSpec((tm, LANES), row),
                 pl.BlockSpec((tm, LANES), row)],
                pl.BlockSpec((tm, tn), lambda i, j: (i + first, j)))

    tn = INPROJ_PREP_TN
    n_steps = IN_W // tn
    in_specs, z_spec = specs(tn, 0)
    assert n_steps >= LATE_WEIGHT_BLOCKS
    late_specs = [pl.BlockSpec((w.shape[0] // LATE_WEIGHT_BLOCKS, w.shape[1]),
                               lambda i, j: (jnp.minimum(j, LATE_WEIGHT_BLOCKS - 1), 0)) for w in late_weights]
    outs = pl.pallas_call(
        functools.partial(_inproj_kernel, tm=tm, tn=tn, prepare_weights=True),
        grid=(1, n_steps),
        in_specs=in_specs + late_specs,
        out_specs=[z_spec, pl.BlockSpec((D_MODEL, tn), lambda i, j: (0, j))] + late_specs,
        out_shape=[z_shape, jax.ShapeDtypeStruct((D_MODEL, IN_W), BF16)]
        + [jax.ShapeDtypeStruct(w.shape, BF16) for w in late_weights],
        scratch_shapes=[u_scratch, pltpu.VMEM((D_MODEL, tn), BF16)],
        compiler_params=params,
        name="inproj_first",
    )(x2, g2, w_in, cos_t, sin_t, *late_weights)
    z, w_bf16, late_bf16 = outs[0], outs[1], outs[2:]
    if t == tm:
        return z, late_bf16
    tn = INPROJ_TN
    in_specs, z_spec = specs(tn, 1)
    z = pl.pallas_call(
        functools.partial(_inproj_kernel, tm=tm, tn=tn, prepare_weights=False),
        grid=(t // tm - 1, IN_W // tn),
        in_specs=in_specs + [pl.BlockSpec(memory_space=pl.ANY)],
        out_specs=z_spec,
        out_shape=z_shape,
        input_output_aliases={len(in_specs): 0},
        scratch_shapes=[u_scratch],
        compiler_params=params,
        name="inproj",
    )(x2, g2, w_bf16, cos_t, sin_t, z)
    return z, late_bf16


def _att_geometry(seq):
    geo = {}
    for d in DILATIONS:
        n_cls = seq // d
        dense = n_cls <= ATT_DENSE_MAX
        geo[d] = (dense, n_cls if dense else ATT_QB, n_cls if dense else ATT_KB)
    return geo


def _att_kernel(q_ref, k_ref, v_ref, g_ref, zc_ref, y_ref, mt_s, lt_s, nt_s, lsed_s, od_s,
                bias_s, biasd_s, p_s, pd_s, cls_s, cls_sem, *, seq):
    geo = _att_geometry(seq)
    tiled = [d for d in DILATIONS if not geo[d][0]]
    dense = [d for d in DILATIONS if geo[d][0]]
    n_tiles = seq // ATT_TILE

    n_pairs = pl.num_programs(1)
    step = pl.program_id(0) * n_pairs + pl.program_id(1)
    n_steps = pl.num_programs(0) * n_pairs
    slot = step % 2

    def class_copies(at_step):
        b, h, sl = at_step // n_pairs, at_step % n_pairs, at_step % 2
        copies = []
        for d in dense:
            n_cls = seq // d
            for which, c0 in enumerate((COL_AQ, COL_AK, COL_AV)):
                for r in range(d):
                    src = zc_ref.at[pl.ds(b * n_cls, n_cls), r, pl.ds((c0 + h) * LANES, LANES)]
                    copies.append(pltpu.make_async_copy(src, cls_s.at[sl, which, r], cls_sem.at[sl, which, r]))
        return copies

    @pl.when(step == 0)
    def _():
        for cp in class_copies(step):
            cp.start()

    @pl.when(step + 1 < n_steps)
    def _():
        for cp in class_copies(step + 1):
            cp.start()
    ones_kv = {kb: jnp.ones((kb, LANES), BF16) for kb in {geo[d][2] for d in DILATIONS}}

    def lanes(rows):
        lane = lax.broadcasted_iota(jnp.int32, (rows, LANES), 1)
        return lane < ATT_HD, _head_a_lanes(lane)

    def band_bias(qb, kb, delta):
        qa = lax.broadcasted_iota(jnp.int32, (2 * qb, kb), 0) & (qb - 1)
        ka = lax.broadcasted_iota(jnp.int32, (2 * qb, kb), 1)
        return jnp.where(jnp.abs(ka - qa + delta) <= HALF, 0.0, NEG)

    for idx, delta in enumerate((-HALF, 0, -2 * HALF)):
        bias_s[idx] = band_bias(ATT_QB, ATT_KB, delta)
    for d in dense:
        biasd_s[...] = band_bias(geo[d][1], geo[d][2], 0)

    def rows_of(start, size, d):
        return pl.ds(start, size) if d == 1 else pl.ds(start, size, stride=d)

    def score_block(d, r, a0, bias, p_ref):
        _, qb, kb = geo[d]
        head_a, qk_head_a = lanes(qb)
        if geo[d][0]:
            q2 = cls_s[slot, 0, r]
            kk = cls_s[slot, 1, r].astype(BF16)
        else:
            ka0 = jnp.clip(a0 - HALF, 0, seq // d - kb)
            q2 = q_ref[rows_of(r + d * a0, qb, d), :]
            kk = k_ref[rows_of(r + d * ka0, kb, d), :].astype(BF16)
        qs = jnp.concatenate([jnp.where(qk_head_a, q2, 0.0), jnp.where(qk_head_a, 0.0, q2)], axis=0)
        s = _bdot_nt(qs, kk) + bias
        m = jnp.max(s, axis=-1, keepdims=True)
        p_ref[...] = jnp.exp2(s - m).astype(BF16)
        mb = jnp.broadcast_to(m, (2 * qb, LANES))
        return jnp.where(head_a, mb[:qb], mb[qb:])

    def value_block(d, r, a0, p_ref):
        _, qb, kb = geo[d]
        head_a, _ = lanes(qb)
        if geo[d][0]:
            vv = cls_s[slot, 2, r].astype(BF16)
        else:
            ka0 = jnp.clip(a0 - HALF, 0, seq // d - kb)
            vv = v_ref[rows_of(r + d * ka0, kb, d), :].astype(BF16)
        nl = jnp.dot(p_ref[...], jnp.concatenate([vv, ones_kv[kb]], axis=1), preferred_element_type=F32)
        return (jnp.where(head_a, nl[:qb, LANES:], nl[qb:, LANES:]),
                jnp.where(head_a, nl[:qb, :LANES], nl[qb:, :LANES]))

    for cp in class_copies(step):
        cp.wait()
    for gi, d in enumerate(dense):
        def dense_scores(r, c, gi=gi, d=d):
            lsed_s[gi * d + r] = score_block(d, r, 0, biasd_s[...], pd_s.at[r])
            return c

        def dense_values(r, c, gi=gi, d=d):
            l, n = value_block(d, r, 0, pd_s.at[r])
            od_s[gi * d + r] = (n / l).astype(BF16)
            lsed_s[gi * d + r] = lsed_s[gi * d + r] + jnp.log2(l)
            return c
        lax.fori_loop(0, d, dense_scores, 0, unroll=ATT_DENSE_UNROLL)
        lax.fori_loop(0, d, dense_values, 0, unroll=ATT_DENSE_UNROLL)

    cb = 256

    def to_natural(d):
        i = lax.broadcasted_iota(jnp.int32, (cb, cb), 0)
        j = lax.broadcasted_iota(jnp.int32, (cb, cb), 1)
        return (j == (i % d) * (cb // d) + i // d).astype(BF16)

    def tile_body(tj, carry):
        tile0 = tj * ATT_TILE
        for gi, d in enumerate(tiled):
            n_cls = seq // d
            per_cls = ATT_TILE // d // ATT_QB
            n_blk = ATT_TILE // ATT_QB
            for bi in range(n_blk):
                r = bi // per_cls
                a0 = tile0 // d + (bi % per_cls) * ATT_QB
                which = jnp.where(a0 == 0, 1, jnp.where(a0 == n_cls - ATT_QB, 2, 0))
                mt_s[gi, rows_of(r + d * a0 - tile0, ATT_QB, d), :] = score_block(
                    d, r, a0, bias_s[which], p_s.at[bi])
            for bi in range(n_blk):
                r = bi // per_cls
                a0 = tile0 // d + (bi % per_cls) * ATT_QB
                dst = rows_of(r + d * a0 - tile0, ATT_QB, d)
                lt_s[gi, dst, :], nt_s[gi, dst, :] = value_block(d, r, a0, p_s.at[bi])

        def comb_body(ci, c):
            rows = pl.ds(pl.multiple_of(ci * cb, cb), cb)
            orow = pl.ds(pl.multiple_of(tile0 + ci * cb, cb), cb)
            parts = [(mt_s[gi, rows, :], lt_s[gi, rows, :], nt_s[gi, rows, :]) for gi in range(len(tiled))]
            for gi, d in enumerate(dense):
                per = cb // d
                a0 = pl.multiple_of((tile0 + ci * cb) // d, per)
                o_cls = jnp.concatenate([od_s[gi * d + r, pl.ds(a0, per), :] for r in range(d)], axis=0)
                lse_cls = jnp.concatenate([lsed_s[gi * d + r, pl.ds(a0, per), :] for r in range(d)], axis=0)
                perm = to_natural(d)
                o_nat = jnp.dot(perm, o_cls, preferred_element_type=F32)
                hi = lse_cls.astype(BF16)
                lo = (lse_cls - hi.astype(F32)).astype(BF16)
                lse2 = jnp.dot(perm, jnp.concatenate([hi, lo], axis=1), preferred_element_type=F32)
                parts.append((lse2[:, :LANES] + lse2[:, LANES:], None, o_nat))
            mx = functools.reduce(jnp.maximum, [part[0] for part in parts])
            num = jnp.zeros((cb, LANES), F32)
            den = jnp.zeros((cb, LANES), F32)
            for m, l, n in parts:
                w = jnp.exp2(m - mx)
                num = num + w * n
                den = den + (w if l is None else w * l)
            y_ref[orow, :] = ((num / den) * _silu(g_ref[orow, :])).astype(y_ref.dtype)
            return c
        lax.fori_loop(0, ATT_TILE // cb, comb_body, 0, unroll=True)
        return carry

    lax.fori_loop(0, n_tiles, tile_body, 0)


def _attention(z, batch, seq):
    t = batch * seq
    n_pairs = ATT_W // LANES

    def col(c0):
        return pl.BlockSpec((seq, LANES), lambda b, h, c0=c0: (b, c0 + h))

    geo = _att_geometry(seq)
    dense_geo = sorted({geo[d][1:] for d in DILATIONS if geo[d][0]})
    assert len(dense_geo) <= 1, "one band-mask table serves the dense groups"
    dqb, dkb = dense_geo[0] if dense_geo else (8, LANES)
    dense_d = [d for d in DILATIONS if geo[d][0]]
    assert len(dense_d) <= 1, "one class view of the projection output serves the dense group"
    n_dense = len(dense_d)
    dd = dense_d[0] if dense_d else 8
    z_cls = z.reshape(t // dd, dd, IN_W)
    tiled = pltpu.VMEM((max(len(DILATIONS) - n_dense, 1), ATT_TILE, LANES), F32)
    n_dense_cls = max(sum(dense_d), 1)
    return pl.pallas_call(
        functools.partial(_att_kernel, seq=seq),
        grid=(batch, n_pairs),
        in_specs=[col(COL_AQ), col(COL_AK), col(COL_AV), col(COL_AG), pl.BlockSpec(memory_space=pl.ANY)],
        out_specs=pl.BlockSpec((seq, LANES), lambda b, h: (b, h)),
        out_shape=jax.ShapeDtypeStruct((t, ATT_W), BF16),
        scratch_shapes=[tiled, tiled, tiled,
                        pltpu.VMEM((n_dense_cls, dqb, LANES), F32),
                        pltpu.VMEM((n_dense_cls, dqb, LANES), BF16),
                        pltpu.VMEM((3, 2 * ATT_QB, ATT_KB), F32),
                        pltpu.VMEM((2 * dqb, dkb), F32),
                        pltpu.VMEM((ATT_TILE // ATT_QB, 2 * ATT_QB, ATT_KB), BF16),
                        pltpu.VMEM((max(DILATIONS), 2 * dqb, dkb), BF16),
                        pltpu.VMEM((2, 3, n_dense_cls, dqb, LANES), F32),
                        pltpu.SemaphoreType.DMA((2, 3, n_dense_cls))],
        compiler_params=pltpu.CompilerParams(
            dimension_semantics=("arbitrary", "arbitrary"), vmem_limit_bytes=VMEM_LIMIT),
        name="band_attention",
    )(z, z, z, z, z_cls)


def _chunk_cumsum(g, tri):
    c = HG_CHUNK
    g1 = g.astype(BF16)
    g2 = (g - g1.astype(F32)).astype(BF16)
    terms = jnp.concatenate([g1, g2], axis=1)
    k = g.shape[1]
    outs = []
    for i in range(g.shape[0] // c):
        part = jnp.dot(tri, terms[i * c:(i + 1) * c, :], preferred_element_type=F32)
        outs.append(part[:, k:] + part[:, :k])
    return jnp.concatenate(outs, axis=0)


def _hgrn_kernel(hq_ref, ff_ref, fb_ref, hv_ref, hg_ref, lbf_ref, lbb_ref, ghg_ref, y_ref,
                 q_s, kf_s, kb_s, bf_s, bb_s, o_s,
                 qtf_s, ktf_s, qtb_s, ktb_s, v16_s,
                 emf_s, emb_s, stf_s, stb_s, *, seq, layer):
    C = HG_CHUNK
    n_chunks = seq // C

    def lower_bound(ref):
        a = ref[...]
        e = jnp.exp(a - jnp.max(a, axis=0, keepdims=True))
        return jnp.sum(e[0:layer + 1, :], axis=0, keepdims=True) / jnp.sum(e, axis=0, keepdims=True)

    lbf = lower_bound(lbf_ref)
    lbb = lower_bound(lbb_ref)

    ti = lax.broadcasted_iota(jnp.int32, (C, C), 0)
    si = lax.broadcasted_iota(jnp.int32, (C, C), 1)
    tri_f = si <= ti
    tri_b = si >= ti

    pr = HG_PRE_ROWS
    n_blocks = seq // pr
    cpb = pr // C
    fwd = (ff_ref, lbf, kf_s, bf_s, qtf_s, ktf_s, emf_s, False)
    bwd = (fb_ref, lbb, kb_s, bb_s, qtb_s, ktb_s, emb_s, True)

    def per_chunk_rows(x3):
        return jnp.broadcast_to(x3, (cpb, C, HG_DK)).reshape(pr, HG_DK)

    def block_rows(i):
        return pl.ds(_aligned(i * pr, pr), pr)

    def prepare_common(i):
        rows = block_rows(i)
        q_s[rows, :] = _silu(hq_ref[rows, :])
        o_s[rows, :] = jnp.zeros((pr, HG_DK), F32)
        v16_s[rows, :] = hv_ref[rows, :].astype(BF16)

    def prepare_direction(i, d, lo):
        f_ref, lb, k_s, b_s, qt_s, kt_s, em_s, reverse = d
        rows = block_rows(i)
        q = q_s[rows, :]
        f = lb + (1.0 - lb) * _sigmoid(f_ref[rows, :])
        kk = 1.0 - f
        tri = tri_b if reverse else tri_f
        b = _chunk_cumsum(jnp.log(f), tri.astype(BF16))
        last = 0 if reverse else C - 1
        btot = b.reshape(cpb, C, HG_DK)[:, last:last + 1, :]
        mid = 0.5 * btot
        midr = per_chunk_rows(mid)
        k_s[rows, :] = kk
        b_s[rows, :] = b
        qt_s[rows, :] = (q * jnp.exp(b - midr)).astype(BF16)
        kt_s[rows, :] = (kk * jnp.exp(midr - b)).astype(BF16)
        em_s[pl.ds(_aligned(i * cpb, cpb), cpb), :] = jnp.exp(mid).reshape(cpb, HG_DK)
        return jnp.minimum(lo, btot.reshape(cpb, HG_DK))

    def chunk(ci, tri, d, st_ref, exact):
        _, _, k_s, b_s, qt_s, kt_s, em_s, reverse = d
        r0 = _aligned(ci * C, C)
        rows = pl.ds(r0, C)
        v = v16_s[rows, :]
        st = st_ref[...]
        if exact:
            q = q_s[rows, :]
            b = b_s[rows, :]
            kk = k_s[rows, :]
            btot = b_s[pl.ds(r0 + (0 if reverse else C - 1), 1), :]

            def col_body(s, att):
                w = q * k_s[pl.ds(r0 + s, 1), :] * jnp.exp(jnp.minimum(b - b_s[pl.ds(r0 + s, 1), :], 0.0))
                return att + jnp.where(si == s, jnp.sum(w, axis=-1, keepdims=True), 0.0)
            att = lax.fori_loop(0, C, col_body, jnp.zeros((C, C), F32))
            o = _bdot(jnp.where(tri, att, 0.0), v) + _bdot_nt(q * jnp.exp(b), st)
            st_ref[...] = st * jnp.exp(btot) + _bdot_tn(v, kk * jnp.exp(btot - b))
        else:
            em = em_s[pl.ds(ci, 1), :]
            qt = qt_s[rows, :]
            kt = kt_s[rows, :]
            att = jnp.where(tri, _bdot_nt(qt, kt), 0.0)
            o = _bdot(att, v) + _bdot_nt(qt, st * em)
            st_ref[...] = st * (em * em) + _bdot_tn(v, kt) * em
        o_s[rows, :] = o_s[rows, :] + o

    def finish_rows(i):
        rows = block_rows(i)
        o = o_s[rows, :]
        ms = jnp.mean(o * o, axis=-1, keepdims=True)
        y = (o * lax.rsqrt(ms + EPS)) * ghg_ref[...]
        y_ref[rows, :] = (y * _silu(hg_ref[rows, :])).astype(y_ref.dtype)

    def factorised_ok(lo):
        return jnp.min(lo) >= HG_FAST_MIN_LOGDECAY

    lo0 = jnp.zeros((cpb, HG_DK), F32)
    stf_s[...] = jnp.zeros_like(stf_s)
    stb_s[...] = jnp.zeros_like(stb_s)

    lo = lo0
    prepare_common(0)
    lo = prepare_direction(0, fwd, lo)
    for i in range(n_blocks):
        if i + 1 < n_blocks:
            prepare_common(i + 1)
            lo = prepare_direction(i + 1, fwd, lo)
        for c in range(cpb):
            chunk(i * cpb + c, tri_f, fwd, stf_s, False)
    lo = prepare_direction(n_blocks - 1, bwd, lo)
    for i in reversed(range(n_blocks)):
        if i > 0:
            lo = prepare_direction(i - 1, bwd, lo)
        for c in reversed(range(cpb)):
            chunk(i * cpb + c, tri_b, bwd, stb_s, False)
        finish_rows(i)

    @pl.when(jnp.logical_not(factorised_ok(lo)))
    def _():
        stf_s[...] = jnp.zeros_like(stf_s)
        stb_s[...] = jnp.zeros_like(stb_s)

        def clear(i, carry):
            o_s[block_rows(i), :] = jnp.zeros((pr, HG_DK), F32)
            return carry
        lax.fori_loop(0, n_blocks, clear, 0)

        def both(n, carry):
            chunk(n, tri_f, fwd, stf_s, True)
            chunk(n_chunks - 1 - n, tri_b, bwd, stb_s, True)
            return carry
        lax.fori_loop(0, n_chunks, both, 0)

        def finish(i, carry):
            finish_rows(i)
            return carry
        lax.fori_loop(0, n_blocks, finish, 0)


def _hgrn(z, lb_fwd, lb_bwd, g_hg, batch, seq, layer):
    t = batch * seq

    def col(c0):
        return pl.BlockSpec((seq, LANES), lambda b, h, c0=c0: (b, c0 + h))

    n_lb = lb_fwd.shape[0]
    lb_spec = pl.BlockSpec((n_lb, HG_DK), lambda b, h: (0, h))
    seq_f32 = pltpu.VMEM((seq, HG_DK), F32)
    seq_bf16 = pltpu.VMEM((seq, HG_DK), BF16)
    chunk_f32 = pltpu.VMEM((seq // HG_CHUNK, HG_DK), F32)
    state = pltpu.VMEM((HG_DK, HG_DK), F32)
    return pl.pallas_call(
        functools.partial(_hgrn_kernel, seq=seq, layer=layer),
        grid=(batch, HG_HEADS),
        in_specs=[col(COL_HQ), col(COL_HFF), col(COL_HFB), col(COL_HV), col(COL_HG),
                  lb_spec, lb_spec, pl.BlockSpec((1, HG_DK), lambda b, h: (0, 0))],
        out_specs=pl.BlockSpec((seq, HG_DK), lambda b, h: (b, h)),
        out_shape=jax.ShapeDtypeStruct((t, HG_W), BF16),
        scratch_shapes=[seq_f32] * 6 + [seq_bf16] * 5 + [chunk_f32, chunk_f32, state, state],
        compiler_params=pltpu.CompilerParams(
            dimension_semantics=("parallel", "parallel"), vmem_limit_bytes=VMEM_LIMIT),
        name="hgrn2",
    )(z, z, z, z, z, lb_fwd, lb_bwd, g_hg.reshape(1, HG_DK))


def _rms(y, g):
    ms = jnp.mean(y * y, axis=-1, keepdims=True)
    return (y * lax.rsqrt(ms + EPS)) * g


def _outproj_kernel(x_ref, ya_ref, yh_ref, p_ref, wo_ref, wpg_ref, wpp_ref, gpost_ref, gple_ref, o_ref):
    sub = x_ref.shape[0] // OUT_SUBBLOCKS
    for r in range(OUT_SUBBLOCKS):
        rows = slice(r * sub, (r + 1) * sub)
        y = (jnp.dot(ya_ref[rows, :], wo_ref[0:ATT_W, :], preferred_element_type=F32)
             + jnp.dot(yh_ref[rows, :], wo_ref[ATT_W:, :], preferred_element_type=F32))
        h1 = x_ref[rows, :] + _rms(y, gpost_ref[...])
        gate = jax.nn.sigmoid(jnp.dot(h1.astype(BF16), wpg_ref[...], preferred_element_type=F32))
        e = jnp.dot(p_ref[rows, :].astype(BF16), wpp_ref[...], preferred_element_type=F32)
        o_ref[rows, :] = h1 + _rms(gate * e, gple_ref[...])


def _outproj(x2, y_att, y_hg, p2, w_out, w_pg, w_pp, g_post, g_ple):
    t = x2.shape[0]
    tm = min(OUT_TM, t)
    row = lambda i: (i, 0)
    const = lambda i: (0, 0)
    resident = dict(pipeline_mode=pl.Buffered(1))
    return pl.pallas_call(
        _outproj_kernel,
        grid=(t // tm,),
        in_specs=[pl.BlockSpec((tm, D_MODEL), row),
                  pl.BlockSpec((tm, ATT_W), row),
                  pl.BlockSpec((tm, HG_W), row),
                  pl.BlockSpec((tm, PLE_DIM), row),
                  pl.BlockSpec((ATT_W + HG_W, D_MODEL), const, **resident),
                  pl.BlockSpec((D_MODEL, D_MODEL), const, **resident),
                  pl.BlockSpec((PLE_DIM, D_MODEL), const, **resident),
                  pl.BlockSpec((1, D_MODEL), const),
                  pl.BlockSpec((1, D_MODEL), const)],
        out_specs=pl.BlockSpec((tm, D_MODEL), row),
        out_shape=jax.ShapeDtypeStruct((t, D_MODEL), F32),
        compiler_params=pltpu.CompilerParams(
            dimension_semantics=("parallel",), vmem_limit_bytes=VMEM_LIMIT),
        name="outproj_ple",
    )(x2, y_att, y_hg, p2, w_out, w_pg, w_pp, g_post.reshape(1, D_MODEL), g_ple.reshape(1, D_MODEL))


def kernel(x, p, positions, w_in, w_out, g_pre, g_post, g_hg, lb_fwd, lb_bwd, w_pg, w_pp, g_ple):
    depth = w_in.shape[0]
    batch, seq, _ = x.shape
    assert seq % ATT_TILE == 0 and seq // max(DILATIONS) >= ATT_KB
    t = batch * seq
    cos_t, sin_t = _rope_tables(positions)
    h = x.reshape(t, D_MODEL)
    for i in range(depth):
        z, (w_out16, w_pg16, w_pp16) = _inproj(h, g_pre[i], w_in[i], cos_t, sin_t,
                                               (w_out[i], w_pg[i], w_pp[i]))
        y_att = _attention(z, batch, seq)
        y_hg = _hgrn(z, lb_fwd, lb_bwd, g_hg[i], batch, seq, i)
        h = _outproj(h, y_att, y_hg, p[i].reshape(t, PLE_DIM), w_out16, w_pg16, w_pp16,
                     g_post[i], g_ple[i])
    return h.reshape(batch, seq, D_MODEL)
```

```python
import functools

import jax
import jax.numpy as jnp
from jax import lax
from jax.experimental import pallas as pl
from jax.experimental.pallas import tpu as pltpu

F32 = jnp.float32
BF16 = jnp.bfloat16

D_MODEL = 2048
PLE_DIM = 256
ATT_HEADS = 16
ATT_HD = 64
ATT_W = ATT_HEADS * ATT_HD
ROT_DIM = ATT_HD // 4
ROPE_THETA = 500000.0
DILATION_CFG = ((128, 1), (512, 4), (2048, 16))
DILATIONS = tuple(d for _, d in DILATION_CFG)
HALF = (DILATION_CFG[0][0] // 2) // DILATION_CFG[0][1]
assert all((w // 2) // d == HALF for w, d in DILATION_CFG)
HG_HEADS = 8
HG_DK = 128
HG_W = HG_HEADS * HG_DK
HG_CHUNK = 128
IN_W = 4 * ATT_W + 5 * HG_W
EPS = 1e-6
NEG = -1e30
LOG2E = 1.4426950408889634

LANES = 128
VMEM_LIMIT = 56 * 1024 * 1024

COL_AQ, COL_AK, COL_AV, COL_AG = (i * ATT_W // LANES for i in range(4))
COL_HQ, COL_HFF, COL_HFB, COL_HV, COL_HG = (4 * ATT_W // LANES + i * HG_W // LANES for i in range(5))

INPROJ_PREP_TN = 512
N_LATE_WEIGHTS = 3
LATE_WEIGHT_BLOCKS = 16
INPROJ_TM = 1024
INPROJ_TN = 1024
INPROJ_ROPE_CHUNK = 256
ATT_TILE = 1024
ATT_QB = 2 * HALF
ATT_KB = ATT_QB + 2 * HALF
ATT_DENSE_MAX = 256
ATT_DENSE_UNROLL = 16
HG_PRE_ROWS = 512
OUT_TM = 512
OUT_SUBBLOCKS = 2
HG_FAST_MIN_LOGDECAY = -120.0


def _bdot(a, b):
    return jnp.dot(a.astype(BF16), b.astype(BF16), preferred_element_type=F32)


def _bdot_nt(a, b):
    return lax.dot_general(a.astype(BF16), b.astype(BF16), (((1,), (1,)), ((), ())),
                           preferred_element_type=F32)


def _bdot_tn(a, b):
    return lax.dot_general(a.astype(BF16), b.astype(BF16), (((0,), (0,)), ((), ())),
                           preferred_element_type=F32)


def _sigmoid(x):
    return 0.5 * jnp.tanh(0.5 * x) + 0.5


def _silu(x):
    return x * _sigmoid(x)


def _aligned(x, m):
    return x if isinstance(x, int) else pl.multiple_of(x, m)


def _pair_swap_matrix():
    h = ROT_DIM // 2
    src = lax.broadcasted_iota(jnp.int32, (LANES, LANES), 0)
    dst = lax.broadcasted_iota(jnp.int32, (LANES, LANES), 1)
    shift = ATT_HD - h
    sigma = jnp.where((dst >= h) & (dst < ROT_DIM), dst + shift,
                      jnp.where((dst >= ATT_HD) & (dst < ATT_HD + h), dst - shift, dst))
    return (src == sigma).astype(BF16)


def _cast_weight_tile(w_ref, o_refs, j, swap_tiles, rb=256):
    n_rows, tn = w_ref.shape

    @pl.when(j < swap_tiles)
    def _():
        perm = _pair_swap_matrix()

        def body(i, carry):
            rows = pl.ds(pl.multiple_of(i * rb, rb), rb)
            for cb in range(tn // LANES):
                cols = slice(cb * LANES, (cb + 1) * LANES)
                wb = jnp.dot(w_ref[rows, cols].astype(BF16), perm, preferred_element_type=F32).astype(BF16)
                for o_ref in o_refs:
                    o_ref[rows, cols] = wb
            return carry
        lax.fori_loop(0, n_rows // rb, body, 0, unroll=True)

    @pl.when(j >= swap_tiles)
    def _():
        def body(i, carry):
            rows = pl.ds(pl.multiple_of(i * rb, rb), rb)
            wb = w_ref[rows, :].astype(BF16)
            for o_ref in o_refs:
                o_ref[rows, :] = wb
            return carry
        lax.fori_loop(0, n_rows // rb, body, 0)


def _head_a_lanes(lane):
    h = ROT_DIM // 2
    return (lane < h) | ((lane >= ROT_DIM) & (lane < ATT_HD + h))


ROPE_FREQS = ROT_DIM // 2
ROPE_GROUPS = LANES // ROPE_FREQS


def _split3(x):
    t1 = x.astype(BF16)
    r1 = x - t1.astype(F32)
    t2 = r1.astype(BF16)
    return t1, t2, (r1 - t2.astype(F32)).astype(BF16)


def _rope_table_kernel(pos_ref, inv_ref, cos_ref, sin_ref, c_s, s_s):
    p = pl.program_id(0)

    @pl.when(p == 0)
    def _():
        ang = pos_ref[...].astype(F32) * inv_ref[...]
        c_s[...] = jnp.concatenate(_split3(jnp.cos(ang)), axis=1)
        s_s[...] = jnp.concatenate(_split3(jnp.sin(ang)), axis=1)

    src = lax.broadcasted_iota(jnp.int32, (3 * LANES, LANES), 0) & (LANES - 1)
    dst = lax.broadcasted_iota(jnp.int32, (3 * LANES, LANES), 1)
    spread = (src == p * ROPE_FREQS + (dst & (ROPE_FREQS - 1))).astype(BF16)
    c = jnp.dot(c_s[...], spread, preferred_element_type=F32)
    s = jnp.dot(s_s[...], spread, preferred_element_type=F32)
    lane = lax.broadcasted_iota(jnp.int32, c.shape, 1)
    is_x1 = lane < ROT_DIM
    is_x2 = (lane >= ATT_HD) & (lane < ATT_HD + ROT_DIM)
    cos_ref[...] = jnp.where(is_x1 | is_x2, c, 1.0)
    sin_ref[...] = jnp.where(is_x1, -s, jnp.where(is_x2, s, 0.0))


def _rope_tables(positions):
    t = positions.size
    n = t // ROPE_GROUPS
    pos_c = jnp.repeat(positions.reshape(ROPE_GROUPS, n).T, ROPE_FREQS, axis=1)
    inv = jnp.power(ROPE_THETA, -jnp.arange(0, ROT_DIM, 2, dtype=F32) / ROT_DIM)
    inv_lane = jnp.tile(inv, ROPE_GROUPS).reshape(1, LANES)
    out = jax.ShapeDtypeStruct((t, LANES), F32)
    return pl.pallas_call(
        _rope_table_kernel,
        grid=(ROPE_GROUPS,),
        in_specs=[pl.BlockSpec((n, LANES), lambda p: (0, 0)),
                  pl.BlockSpec((1, LANES), lambda p: (0, 0))],
        out_specs=[pl.BlockSpec((n, LANES), lambda p: (p, 0))] * 2,
        out_shape=[out] * 2,
        scratch_shapes=[pltpu.VMEM((n, 3 * LANES), BF16)] * 2,
        compiler_params=pltpu.CompilerParams(dimension_semantics=("arbitrary",)),
        name="rope_tables",
    )(pos_c, inv_lane)


def _inproj_kernel(*refs, tm, tn, prepare_weights):
    j = pl.program_id(1)
    rb = 128
    n_q = ATT_W // tn
    n_rope = 2 * ATT_W // tn
    if prepare_weights:
        x_ref, g_ref, w32_ref, cos_ref, sin_ref = refs[:5]
        late32 = refs[5:5 + N_LATE_WEIGHTS]
        z_ref, wout_ref = refs[5 + N_LATE_WEIGHTS:7 + N_LATE_WEIGHTS]
        late16 = refs[7 + N_LATE_WEIGHTS:7 + 2 * N_LATE_WEIGHTS]
        u_ref, w_ref = refs[7 + 2 * N_LATE_WEIGHTS:]
        _cast_weight_tile(w32_ref, (w_ref, wout_ref), j, n_rope)
        for src, dst in zip(late32, late16):
            dst[...] = src[...].astype(BF16)
    else:
        x_ref, g_ref, w_ref, cos_ref, sin_ref, _, z_ref, u_ref = refs

    def norm_rows(r0, n):
        for i in range(n // rb):
            rows = slice(r0 + i * rb, r0 + (i + 1) * rb)
            xb = x_ref[rows, :]
            ms = jnp.mean(xb * xb, axis=-1, keepdims=True)
            u_ref[rows, :] = ((xb * lax.rsqrt(ms + EPS)) * g_ref[...]).astype(BF16)

    def project_rotate(r0, n):
        rows = slice(r0, r0 + n)
        scale = jnp.where(j < n_q, ATT_HD ** -0.5 * LOG2E, 1.0).astype(F32)
        c = cos_ref[rows, :] * scale
        s = sin_ref[rows, :] * scale
        for cb in range(tn // INPROJ_ROPE_CHUNK):
            cols = slice(cb * INPROJ_ROPE_CHUNK, (cb + 1) * INPROJ_ROPE_CHUNK)
            zc = jnp.dot(u_ref[rows, :], w_ref[:, cols], preferred_element_type=F32)
            parts = [zc[:, b * LANES:(b + 1) * LANES] for b in range(INPROJ_ROPE_CHUNK // LANES)]
            z_ref[rows, cols] = jnp.concatenate(
                [blk * c + pltpu.roll(blk, LANES // 2, 1) * s for blk in parts], axis=1)

    @pl.when(j == 0)
    def _():
        for r0 in range(0, tm, tm // 2):
            norm_rows(r0, tm // 2)
            project_rotate(r0, tm // 2)

    @pl.when((j > 0) & (j < n_rope))
    def _():
        project_rotate(0, tm)

    @pl.when(j >= n_rope)
    def _():
        z_ref[...] = jnp.dot(u_ref[...], w_ref[...], preferred_element_type=F32)


def _inproj(x2, g_pre, w_in, cos_t, sin_t, late_weights):
    assert len(late_weights) == N_LATE_WEIGHTS
    t = x2.shape[0]
    tm = min(INPROJ_TM, t)
    g2 = g_pre.reshape(1, D_MODEL)
    z_shape = jax.ShapeDtypeStruct((t, IN_W), F32)
    params = pltpu.CompilerParams(dimension_semantics=("parallel", "arbitrary"), vmem_limit_bytes=VMEM_LIMIT)
    u_scratch = pltpu.VMEM((tm, D_MODEL), BF16)

    def specs(tn, first):
        row = lambda i, j: (i + first, 0)
        return ([pl.BlockSpec((tm, D_MODEL), row),
                 pl.BlockSpec((1, D_MODEL), lambda i, j: (0, 0)),
                 pl.BlockSpec((D_MODEL, tn), lambda i, j: (0, j)),
                 pl.BlockSpec((tm, LANES), row),
                 pl.BlockSpec((tm, LANES), row)],
                pl.BlockSpec((tm, tn), lambda i, j: (i + first, j)))

    tn = INPROJ_PREP_TN
    n_steps = IN_W // tn
    in_specs, z_spec = specs(tn, 0)
    assert n_steps >= LATE_WEIGHT_BLOCKS
    late_specs = [pl.BlockSpec((w.shape[0] // LATE_WEIGHT_BLOCKS, w.shape[1]),
                               lambda i, j: (jnp.minimum(j, LATE_WEIGHT_BLOCKS - 1), 0)) for w in late_weights]
    outs = pl.pallas_call(
        functools.partial(_inproj_kernel, tm=tm, tn=tn, prepare_weights=True),
        grid=(1, n_steps),
        in_specs=in_specs + late_specs,
        out_specs=[z_spec, pl.BlockSpec((D_MODEL, tn), lambda i, j: (0, j))] + late_specs,
        out_shape=[z_shape, jax.ShapeDtypeStruct((D_MODEL, IN_W), BF16)]
        + [jax.ShapeDtypeStruct(w.shape, BF16) for w in late_weights],
        scratch_shapes=[u_scratch, pltpu.VMEM((D_MODEL, tn), BF16)],
        compiler_params=params,
        name="inproj_first",
    )(x2, g2, w_in, cos_t, sin_t, *late_weights)
    z, w_bf16, late_bf16 = outs[0], outs[1], outs[2:]
    if t == tm:
        return z, late_bf16
    tn = INPROJ_TN
    in_specs, z_spec = specs(tn, 1)
    z = pl.pallas_call(
        functools.partial(_inproj_kernel, tm=tm, tn=tn, prepare_weights=False),
        grid=(t // tm - 1, IN_W // tn),
        in_specs=in_specs + [pl.BlockSpec(memory_space=pl.ANY)],
        out_specs=z_spec,
        out_shape=z_shape,
        input_output_aliases={len(in_specs): 0},
        scratch_shapes=[u_scratch],
        compiler_params=params,
        name="inproj",
    )(x2, g2, w_bf16, cos_t, sin_t, z)
    return z, late_bf16


def _att_geometry(seq):
    geo = {}
    for d in DILATIONS:
        n_cls = seq // d
        dense = n_cls <= ATT_DENSE_MAX
        geo[d] = (dense, n_cls if dense else ATT_QB, n_cls if dense else ATT_KB)
    return geo


def _att_kernel(q_ref, k_ref, v_ref, g_ref, zc_ref, y_ref, mt_s, lt_s, nt_s, lsed_s, od_s,
                bias_s, biasd_s, p_s, pd_s, cls_s, cls_sem, *, seq):
    geo = _att_geometry(seq)
    tiled = [d for d in DILATIONS if not geo[d][0]]
    dense = [d for d in DILATIONS if geo[d][0]]
    n_tiles = seq // ATT_TILE

    n_pairs = pl.num_programs(1)
    step = pl.program_id(0) * n_pairs + pl.program_id(1)
    n_steps = pl.num_programs(0) * n_pairs
    slot = step % 2

    def class_copies(at_step):
        b, h, sl = at_step // n_pairs, at_step % n_pairs, at_step % 2
        copies = []
        for d in dense:
            n_cls = seq // d
            for which, c0 in enumerate((COL_AQ, COL_AK, COL_AV)):
                for r in range(d):
                    src = zc_ref.at[pl.ds(b * n_cls, n_cls), r, pl.ds((c0 + h) * LANES, LANES)]
                    copies.append(pltpu.make_async_copy(src, cls_s.at[sl, which, r], cls_sem.at[sl, which, r]))
        return copies

    @pl.when(step == 0)
    def _():
        for cp in class_copies(step):
            cp.start()

    @pl.when(step + 1 < n_steps)
    def _():
        for cp in class_copies(step + 1):
            cp.start()
    ones_kv = {kb: jnp.ones((kb, LANES), BF16) for kb in {geo[d][2] for d in DILATIONS}}

    def lanes(rows):
        lane = lax.broadcasted_iota(jnp.int32, (rows, LANES), 1)
        return lane < ATT_HD, _head_a_lanes(lane)

    def band_bias(qb, kb, delta):
        qa = lax.broadcasted_iota(jnp.int32, (2 * qb, kb), 0) & (qb - 1)
        ka = lax.broadcasted_iota(jnp.int32, (2 * qb, kb), 1)
        return jnp.where(jnp.abs(ka - qa + delta) <= HALF, 0.0, NEG)

    for idx, delta in enumerate((-HALF, 0, -2 * HALF)):
        bias_s[idx] = band_bias(ATT_QB, ATT_KB, delta)
    for d in dense:
        biasd_s[...] = band_bias(geo[d][1], geo[d][2], 0)

    def rows_of(start, size, d):
        return pl.ds(start, size) if d == 1 else pl.ds(start, size, stride=d)

    def score_block(d, r, a0, bias, p_ref):
        _, qb, kb = geo[d]
        head_a, qk_head_a = lanes(qb)
        if geo[d][0]:
            q2 = cls_s[slot, 0, r]
            kk = cls_s[slot, 1, r].astype(BF16)
        else:
            ka0 = jnp.clip(a0 - HALF, 0, seq // d - kb)
            q2 = q_ref[rows_of(r + d * a0, qb, d), :]
            kk = k_ref[rows_of(r + d * ka0, kb, d), :].astype(BF16)
        qs = jnp.concatenate([jnp.where(qk_head_a, q2, 0.0), jnp.where(qk_head_a, 0.0, q2)], axis=0)
        s = _bdot_nt(qs, kk) + bias
        m = jnp.max(s, axis=-1, keepdims=True)
        p_ref[...] = jnp.exp2(s - m).astype(BF16)
        mb = jnp.broadcast_to(m, (2 * qb, LANES))
        return jnp.where(head_a, mb[:qb], mb[qb:])

    def value_block(d, r, a0, p_ref):
        _, qb, kb = geo[d]
        head_a, _ = lanes(qb)
        if geo[d][0]:
            vv = cls_s[slot, 2, r].astype(BF16)
        else:
            ka0 = jnp.clip(a0 - HALF, 0, seq // d - kb)
            vv = v_ref[rows_of(r + d * ka0, kb, d), :].astype(BF16)
        nl = jnp.dot(p_ref[...], jnp.concatenate([vv, ones_kv[kb]], axis=1), preferred_element_type=F32)
        return (jnp.where(head_a, nl[:qb, LANES:], nl[qb:, LANES:]),
                jnp.where(head_a, nl[:qb, :LANES], nl[qb:, :LANES]))

    for cp in class_copies(step):
        cp.wait()
    for gi, d in enumerate(dense):
        def dense_scores(r, c, gi=gi, d=d):
            lsed_s[gi * d + r] = score_block(d, r, 0, biasd_s[...], pd_s.at[r])
            return c

        def dense_values(r, c, gi=gi, d=d):
            l, n = value_block(d, r, 0, pd_s.at[r])
            od_s[gi * d + r] = (n / l).astype(BF16)
            lsed_s[gi * d + r] = lsed_s[gi * d + r] + jnp.log2(l)
            return c
        lax.fori_loop(0, d, dense_scores, 0, unroll=ATT_DENSE_UNROLL)
        lax.fori_loop(0, d, dense_values, 0, unroll=ATT_DENSE_UNROLL)

    cb = 256

    def to_natural(d):
        i = lax.broadcasted_iota(jnp.int32, (cb, cb), 0)
        j = lax.broadcasted_iota(jnp.int32, (cb, cb), 1)
        return (j == (i % d) * (cb // d) + i // d).astype(BF16)

    def tile_body(tj, carry):
        tile0 = tj * ATT_TILE
        for gi, d in enumerate(tiled):
            n_cls = seq // d
            per_cls = ATT_TILE // d // ATT_QB
            n_blk = ATT_TILE // ATT_QB
            for bi in range(n_blk):
                r = bi // per_cls
                a0 = tile0 // d + (bi % per_cls) * ATT_QB
                which = jnp.where(a0 == 0, 1, jnp.where(a0 == n_cls - ATT_QB, 2, 0))
                mt_s[gi, rows_of(r + d * a0 - tile0, ATT_QB, d), :] = score_block(
                    d, r, a0, bias_s[which], p_s.at[bi])
            for bi in range(n_blk):
                r = bi // per_cls
                a0 = tile0 // d + (bi % per_cls) * ATT_QB
                dst = rows_of(r + d * a0 - tile0, ATT_QB, d)
                lt_s[gi, dst, :], nt_s[gi, dst, :] = value_block(d, r, a0, p_s.at[bi])

        def comb_body(ci, c):
            rows = pl.ds(pl.multiple_of(ci * cb, cb), cb)
            orow = pl.ds(pl.multiple_of(tile0 + ci * cb, cb), cb)
            parts = [(mt_s[gi, rows, :], lt_s[gi, rows, :], nt_s[gi, rows, :]) for gi in range(len(tiled))]
            for gi, d in enumerate(dense):
                per = cb // d
                a0 = pl.multiple_of((tile0 + ci * cb) // d, per)
                o_cls = jnp.concatenate([od_s[gi * d + r, pl.ds(a0, per), :] for r in range(d)], axis=0)
                lse_cls = jnp.concatenate([lsed_s[gi * d + r, pl.ds(a0, per), :] for r in range(d)], axis=0)
                perm = to_natural(d)
                o_nat = jnp.dot(perm, o_cls, preferred_element_type=F32)
                hi = lse_cls.astype(BF16)
                lo = (lse_cls - hi.astype(F32)).astype(BF16)
                lse2 = jnp.dot(perm, jnp.concatenate([hi, lo], axis=1), preferred_element_type=F32)
                parts.append((lse2[:, :LANES] + lse2[:, LANES:], None, o_nat))
            mx = functools.reduce(jnp.maximum, [part[0] for part in parts])
            num = jnp.zeros((cb, LANES), F32)
            den = jnp.zeros((cb, LANES), F32)
            for m, l, n in parts:
                w = jnp.exp2(m - mx)
                num = num + w * n
                den = den + (w if l is None else w * l)
            y_ref[orow, :] = ((num / den) * _silu(g_ref[orow, :])).astype(y_ref.dtype)
            return c
        lax.fori_loop(0, ATT_TILE // cb, comb_body, 0, unroll=True)
        return carry

    lax.fori_loop(0, n_tiles, tile_body, 0)


def _attention(z, batch, seq):
    t = batch * seq
    n_pairs = ATT_W // LANES

    def col(c0):
        return pl.BlockSpec((seq, LANES), lambda b, h, c0=c0: (b, c0 + h))

    geo = _att_geometry(seq)
    dense_geo = sorted({geo[d][1:] for d in DILATIONS if geo[d][0]})
    assert len(dense_geo) <= 1, "one band-mask table serves the dense groups"
    dqb, dkb = dense_geo[0] if dense_geo else (8, LANES)
    dense_d = [d for d in DILATIONS if geo[d][0]]
    assert len(dense_d) <= 1, "one class view of the projection output serves the dense group"
    n_dense = len(dense_d)
    dd = dense_d[0] if dense_d else 8
    z_cls = z.reshape(t // dd, dd, IN_W)
    tiled = pltpu.VMEM((max(len(DILATIONS) - n_dense, 1), ATT_TILE, LANES), F32)
    n_dense_cls = max(sum(dense_d), 1)
    return pl.pallas_call(
        functools.partial(_att_kernel, seq=seq),
        grid=(batch, n_pairs),
        in_specs=[col(COL_AQ), col(COL_AK), col(COL_AV), col(COL_AG), pl.BlockSpec(memory_space=pl.ANY)],
        out_specs=pl.BlockSpec((seq, LANES), lambda b, h: (b, h)),
        out_shape=jax.ShapeDtypeStruct((t, ATT_W), BF16),
        scratch_shapes=[tiled, tiled, tiled,
                        pltpu.VMEM((n_dense_cls, dqb, LANES), F32),
                        pltpu.VMEM((n_dense_cls, dqb, LANES), BF16),
                        pltpu.VMEM((3, 2 * ATT_QB, ATT_KB), F32),
                        pltpu.VMEM((2 * dqb, dkb), F32),
                        pltpu.VMEM((ATT_TILE // ATT_QB, 2 * ATT_QB, ATT_KB), BF16),
                        pltpu.VMEM((max(DILATIONS), 2 * dqb, dkb), BF16),
                        pltpu.VMEM((2, 3, n_dense_cls, dqb, LANES), F32),
                        pltpu.SemaphoreType.DMA((2, 3, n_dense_cls))],
        compiler_params=pltpu.CompilerParams(
            dimension_semantics=("arbitrary", "arbitrary"), vmem_limit_bytes=VMEM_LIMIT),
        name="band_attention",
    )(z, z, z, z, z_cls)


def _chunk_cumsum(g, tri):
    c = HG_CHUNK
    g1 = g.astype(BF16)
    g2 = (g - g1.astype(F32)).astype(BF16)
    terms = jnp.concatenate([g1, g2], axis=1)
    k = g.shape[1]
    outs = []
    for i in range(g.shape[0] // c):
        part = jnp.dot(tri, terms[i * c:(i + 1) * c, :], preferred_element_type=F32)
        outs.append(part[:, k:] + part[:, :k])
    return jnp.concatenate(outs, axis=0)


def _hgrn_kernel(hq_ref, ff_ref, fb_ref, hv_ref, hg_ref, lbf_ref, lbb_ref, ghg_ref, y_ref,
                 q_s, kf_s, kb_s, bf_s, bb_s, o_s,
                 qtf_s, ktf_s, qtb_s, ktb_s, v16_s,
                 emf_s, emb_s, stf_s, stb_s, *, seq, layer):
    C = HG_CHUNK
    n_chunks = seq // C

    def lower_bound(ref):
        a = ref[...]
        e = jnp.exp(a - jnp.max(a, axis=0, keepdims=True))
        return jnp.sum(e[0:layer + 1, :], axis=0, keepdims=True) / jnp.sum(e, axis=0, keepdims=True)

    lbf = lower_bound(lbf_ref)
    lbb = lower_bound(lbb_ref)

    ti = lax.broadcasted_iota(jnp.int32, (C, C), 0)
    si = lax.broadcasted_iota(jnp.int32, (C, C), 1)
    tri_f = si <= ti
    tri_b = si >= ti

    pr = HG_PRE_ROWS
    n_blocks = seq // pr
    cpb = pr // C
    fwd = (ff_ref, lbf, kf_s, bf_s, qtf_s, ktf_s, emf_s, False)
    bwd = (fb_ref, lbb, kb_s, bb_s, qtb_s, ktb_s, emb_s, True)

    def per_chunk_rows(x3):
        return jnp.broadcast_to(x3, (cpb, C, HG_DK)).reshape(pr, HG_DK)

    def block_rows(i):
        return pl.ds(_aligned(i * pr, pr), pr)

    def prepare_common(i):
        rows = block_rows(i)
        q_s[rows, :] = _silu(hq_ref[rows, :])
        o_s[rows, :] = jnp.zeros((pr, HG_DK), F32)
        v16_s[rows, :] = hv_ref[rows, :].astype(BF16)

    def prepare_direction(i, d, lo):
        f_ref, lb, k_s, b_s, qt_s, kt_s, em_s, reverse = d
        rows = block_rows(i)
        q = q_s[rows, :]
        f = lb + (1.0 - lb) * _sigmoid(f_ref[rows, :])
        kk = 1.0 - f
        tri = tri_b if reverse else tri_f
        b = _chunk_cumsum(jnp.log(f), tri.astype(BF16))
        last = 0 if reverse else C - 1
        btot = b.reshape(cpb, C, HG_DK)[:, last:last + 1, :]
        mid = 0.5 * btot
        midr = per_chunk_rows(mid)
        k_s[rows, :] = kk
        b_s[rows, :] = b
        qt_s[rows, :] = (q * jnp.exp(b - midr)).astype(BF16)
        kt_s[rows, :] = (kk * jnp.exp(midr - b)).astype(BF16)
        em_s[pl.ds(_aligned(i * cpb, cpb), cpb), :] = jnp.exp(mid).reshape(cpb, HG_DK)
        return jnp.minimum(lo, btot.reshape(cpb, HG_DK))

    def chunk(ci, tri, d, st_ref, exact):
        _, _, k_s, b_s, qt_s, kt_s, em_s, reverse = d
        r0 = _aligned(ci * C, C)
        rows = pl.ds(r0, C)
        v = v16_s[rows, :]
        st = st_ref[...]
        if exact:
            q = q_s[rows, :]
            b = b_s[rows, :]
            kk = k_s[rows, :]
            btot = b_s[pl.ds(r0 + (0 if reverse else C - 1), 1), :]

            def col_body(s, att):
                w = q * k_s[pl.ds(r0 + s, 1), :] * jnp.exp(jnp.minimum(b - b_s[pl.ds(r0 + s, 1), :], 0.0))
                return att + jnp.where(si == s, jnp.sum(w, axis=-1, keepdims=True), 0.0)
            att = lax.fori_loop(0, C, col_body, jnp.zeros((C, C), F32))
            o = _bdot(jnp.where(tri, att, 0.0), v) + _bdot_nt(q * jnp.exp(b), st)
            st_ref[...] = st * jnp.exp(btot) + _bdot_tn(v, kk * jnp.exp(btot - b))
        else:
            em = em_s[pl.ds(ci, 1), :]
            qt = qt_s[rows, :]
            kt = kt_s[rows, :]
            att = jnp.where(tri, _bdot_nt(qt, kt), 0.0)
            o = _bdot(att, v) + _bdot_nt(qt, st * em)
            st_ref[...] = st * (em * em) + _bdot_tn(v, kt) * em
        o_s[rows, :] = o_s[rows, :] + o

    def finish_rows(i):
        rows = block_rows(i)
        o = o_s[rows, :]
        ms = jnp.mean(o * o, axis=-1, keepdims=True)
        y = (o * lax.rsqrt(ms + EPS)) * ghg_ref[...]
        y_ref[rows, :] = (y * _silu(hg_ref[rows, :])).astype(y_ref.dtype)

    def factorised_ok(lo):
        return jnp.min(lo) >= HG_FAST_MIN_LOGDECAY

    lo0 = jnp.zeros((cpb, HG_DK), F32)
    stf_s[...] = jnp.zeros_like(stf_s)
    stb_s[...] = jnp.zeros_like(stb_s)

    lo = lo0
    prepare_common(0)
    lo = prepare_direction(0, fwd, lo)
    for i in range(n_blocks):
        if i + 1 < n_blocks:
            prepare_common(i + 1)
            lo = prepare_direction(i + 1, fwd, lo)
        for c in range(cpb):
            chunk(i * cpb + c, tri_f, fwd, stf_s, False)
    lo = prepare_direction(n_blocks - 1, bwd, lo)
    for i in reversed(range(n_blocks)):
        if i > 0:
            lo = prepare_direction(i - 1, bwd, lo)
        for c in reversed(range(cpb)):
            chunk(i * cpb + c, tri_b, bwd, stb_s, False)
        finish_rows(i)

    @pl.when(jnp.logical_not(factorised_ok(lo)))
    def _():
        stf_s[...] = jnp.zeros_like(stf_s)
        stb_s[...] = jnp.zeros_like(stb_s)

        def clear(i, carry):
            o_s[block_rows(i), :] = jnp.zeros((pr, HG_DK), F32)
            return carry
        lax.fori_loop(0, n_blocks, clear, 0)

        def both(n, carry):
            chunk(n, tri_f, fwd, stf_s, True)
            chunk(n_chunks - 1 - n, tri_b, bwd, stb_s, True)
            return carry
        lax.fori_loop(0, n_chunks, both, 0)

        def finish(i, carry):
            finish_rows(i)
            return carry
        lax.fori_loop(0, n_blocks, finish, 0)


def _hgrn(z, lb_fwd, lb_bwd, g_hg, batch, seq, layer):
    t = batch * seq

    def col(c0):
        return pl.BlockSpec((seq, LANES), lambda b, h, c0=c0: (b, c0 + h))

    n_lb = lb_fwd.shape[0]
    lb_spec = pl.BlockSpec((n_lb, HG_DK), lambda b, h: (0, h))
    seq_f32 = pltpu.VMEM((seq, HG_DK), F32)
    seq_bf16 = pltpu.VMEM((seq, HG_DK), BF16)
    chunk_f32 = pltpu.VMEM((seq // HG_CHUNK, HG_DK), F32)
    state = pltpu.VMEM((HG_DK, HG_DK), F32)
    return pl.pallas_call(
        functools.partial(_hgrn_kernel, seq=seq, layer=layer),
        grid=(batch, HG_HEADS),
        in_specs=[col(COL_HQ), col(COL_HFF), col(COL_HFB), col(COL_HV), col(COL_HG),
                  lb_spec, lb_spec, pl.BlockSpec((1, HG_DK), lambda b, h: (0, 0))],
        out_specs=pl.BlockSpec((seq, HG_DK), lambda b, h: (b, h)),
        out_shape=jax.ShapeDtypeStruct((t, HG_W), BF16),
        scratch_shapes=[seq_f32] * 6 + [seq_bf16] * 5 + [chunk_f32, chunk_f32, state, state],
        compiler_params=pltpu.CompilerParams(
            dimension_semantics=("parallel", "parallel"), vmem_limit_bytes=VMEM_LIMIT),
        name="hgrn2",
    )(z, z, z, z, z, lb_fwd, lb_bwd, g_hg.reshape(1, HG_DK))


def _rms(y, g):
    ms = jnp.mean(y * y, axis=-1, keepdims=True)
    return (y * lax.rsqrt(ms + EPS)) * g


def _outproj_kernel(x_ref, ya_ref, yh_ref, p_ref, wo_ref, wpg_ref, wpp_ref, gpost_ref, gple_ref, o_ref):
    sub = x_ref.shape[0] // OUT_SUBBLOCKS
    for r in range(OUT_SUBBLOCKS):
        rows = slice(r * sub, (r + 1) * sub)
        y = (jnp.dot(ya_ref[rows, :], wo_ref[0:ATT_W, :], preferred_element_type=F32)
             + jnp.dot(yh_ref[rows, :], wo_ref[ATT_W:, :], preferred_element_type=F32))
        h1 = x_ref[rows, :] + _rms(y, gpost_ref[...])
        gate = jax.nn.sigmoid(jnp.dot(h1.astype(BF16), wpg_ref[...], preferred_element_type=F32))
        e = jnp.dot(p_ref[rows, :].astype(BF16), wpp_ref[...], preferred_element_type=F32)
        o_ref[rows, :] = h1 + _rms(gate * e, gple_ref[...])


def _outproj(x2, y_att, y_hg, p2, w_out, w_pg, w_pp, g_post, g_ple):
    t = x2.shape[0]
    tm = min(OUT_TM, t)
    row = lambda i: (i, 0)
    const = lambda i: (0, 0)
    resident = dict(pipeline_mode=pl.Buffered(1))
    return pl.pallas_call(
        _outproj_kernel,
        grid=(t // tm,),
        in_specs=[pl.BlockSpec((tm, D_MODEL), row),
                  pl.BlockSpec((tm, ATT_W), row),
                  pl.BlockSpec((tm, HG_W), row),
                  pl.BlockSpec((tm, PLE_DIM), row),
                  pl.BlockSpec((ATT_W + HG_W, D_MODEL), const, **resident),
                  pl.BlockSpec((D_MODEL, D_MODEL), const, **resident),
                  pl.BlockSpec((PLE_DIM, D_MODEL), const, **resident),
                  pl.BlockSpec((1, D_MODEL), const),
                  pl.BlockSpec((1, D_MODEL), const)],
        out_specs=pl.BlockSpec((tm, D_MODEL), row),
        out_shape=jax.ShapeDtypeStruct((t, D_MODEL), F32),
        compiler_params=pltpu.CompilerParams(
            dimension_semantics=("parallel",), vmem_limit_bytes=VMEM_LIMIT),
        name="outproj_ple",
    )(x2, y_att, y_hg, p2, w_out, w_pg, w_pp, g_post.reshape(1, D_MODEL), g_ple.reshape(1, D_MODEL))


def kernel(x, p, positions, w_in, w_out, g_pre, g_post, g_hg, lb_fwd, lb_bwd, w_pg, w_pp, g_ple):
    depth = w_in.shape[0]
    batch, seq, _ = x.shape
    assert seq % ATT_TILE == 0 and seq // max(DILATIONS) >= ATT_KB
    t = batch * seq
    cos_t, sin_t = _rope_tables(positions)
    h = x.reshape(t, D_MODEL)
    for i in range(depth):
        z, (w_out16, w_pg16, w_pp16) = _inproj(h, g_pre[i], w_in[i], cos_t, sin_t,
                                               (w_out[i], w_pg[i], w_pp[i]))
        y_att = _attention(z, batch, seq)
        y_hg = _hgrn(z, lb_fwd, lb_bwd, g_hg[i], batch, seq, i)
        h = _outproj(h, y_att, y_hg, p[i].reshape(t, PLE_DIM), w_out16, w_pg16, w_pp16,
                     g_post[i], g_ple[i])
    return h.reshape(batch, seq, D_MODEL)
```

```python
import functools

import jax
import jax.numpy as jnp
from jax import lax
from jax.experimental import pallas as pl
from jax.experimental.pallas import tpu as pltpu

F32 = jnp.float32
BF16 = jnp.bfloat16

D_MODEL = 2048
PLE_DIM = 256
ATT_HEADS = 16
ATT_HD = 64
ATT_W = ATT_HEADS * ATT_HD
ROT_DIM = ATT_HD // 4
ROPE_THETA = 500000.0
DILATION_CFG = ((128, 1), (512, 4), (2048, 16))
DILATIONS = tuple(d for _, d in DILATION_CFG)
HALF = (DILATION_CFG[0][0] // 2) // DILATION_CFG[0][1]
assert all((w // 2) // d == HALF for w, d in DILATION_CFG)
HG_HEADS = 8
HG_DK = 128
HG_W = HG_HEADS * HG_DK
HG_CHUNK = 128
IN_W = 4 * ATT_W + 5 * HG_W
EPS = 1e-6
NEG = -1e30
LOG2E = 1.4426950408889634

LANES = 128
VMEM_LIMIT = 56 * 1024 * 1024

COL_AQ, COL_AK, COL_AV, COL_AG = (i * ATT_W // LANES for i in range(4))
COL_HQ, COL_HFF, COL_HFB, COL_HV, COL_HG = (4 * ATT_W // LANES + i * HG_W // LANES for i in range(5))

INPROJ_PREP_TN = 512
N_LATE_WEIGHTS = 3
LATE_WEIGHT_BLOCKS = 16
INPROJ_TM = 1024
INPROJ_TN = 1024
INPROJ_ROPE_CHUNK = 256
ATT_TILE = 1024
ATT_QB = 2 * HALF
ATT_KB = ATT_QB + 2 * HALF
ATT_DENSE_MAX = 256
ATT_DENSE_UNROLL = 16
HG_PRE_ROWS = 512
OUT_TM = 512
OUT_SUBBLOCKS = 2
HG_FAST_MIN_LOGDECAY = -120.0


def _bdot(a, b):
    return jnp.dot(a.astype(BF16), b.astype(BF16), preferred_element_type=F32)


def _bdot_nt(a, b):
    return lax.dot_general(a.astype(BF16), b.astype(BF16), (((1,), (1,)), ((), ())),
                           preferred_element_type=F32)


def _bdot_tn(a, b):
    return lax.dot_general(a.astype(BF16), b.astype(BF16), (((0,), (0,)), ((), ())),
                           preferred_element_type=F32)


def _sigmoid(x):
    return 0.5 * jnp.tanh(0.5 * x) + 0.5


def _silu(x):
    return x * _sigmoid(x)


def _aligned(x, m):
    return x if isinstance(x, int) else pl.multiple_of(x, m)


def _pair_swap_matrix():
    h = ROT_DIM // 2
    src = lax.broadcasted_iota(jnp.int32, (LANES, LANES), 0)
    dst = lax.broadcasted_iota(jnp.int32, (LANES, LANES), 1)
    shift = ATT_HD - h
    sigma = jnp.where((dst >= h) & (dst < ROT_DIM), dst + shift,
                      jnp.where((dst >= ATT_HD) & (dst < ATT_HD + h), dst - shift, dst))
    return (src == sigma).astype(BF16)


def _cast_weight_tile(w_ref, o_refs, j, swap_tiles, rb=256):
    n_rows, tn = w_ref.shape

    @pl.when(j < swap_tiles)
    def _():
        perm = _pair_swap_matrix()

        def body(i, carry):
            rows = pl.ds(pl.multiple_of(i * rb, rb), rb)
            for cb in range(tn // LANES):
                cols = slice(cb * LANES, (cb + 1) * LANES)
                wb = jnp.dot(w_ref[rows, cols].astype(BF16), perm, preferred_element_type=F32).astype(BF16)
                for o_ref in o_refs:
                    o_ref[rows, cols] = wb
            return carry
        lax.fori_loop(0, n_rows // rb, body, 0, unroll=True)

    @pl.when(j >= swap_tiles)
    def _():
        def body(i, carry):
            rows = pl.ds(pl.multiple_of(i * rb, rb), rb)
            wb = w_ref[rows, :].astype(BF16)
            for o_ref in o_refs:
                o_ref[rows, :] = wb
            return carry
        lax.fori_loop(0, n_rows // rb, body, 0)


def _head_a_lanes(lane):
    h = ROT_DIM // 2
    return (lane < h) | ((lane >= ROT_DIM) & (lane < ATT_HD + h))


ROPE_FREQS = ROT_DIM // 2
ROPE_GROUPS = LANES // ROPE_FREQS


def _split3(x):
    t1 = x.astype(BF16)
    r1 = x - t1.astype(F32)
    t2 = r1.astype(BF16)
    return t1, t2, (r1 - t2.astype(F32)).astype(BF16)


def _rope_table_kernel(pos_ref, inv_ref, cos_ref, sin_ref, c_s, s_s):
    p = pl.program_id(0)

    @pl.when(p == 0)
    def _():
        ang = pos_ref[...].astype(F32) * inv_ref[...]
        c_s[...] = jnp.concatenate(_split3(jnp.cos(ang)), axis=1)
        s_s[...] = jnp.concatenate(_split3(jnp.sin(ang)), axis=1)

    src = lax.broadcasted_iota(jnp.int32, (3 * LANES, LANES), 0) & (LANES - 1)
    dst = lax.broadcasted_iota(jnp.int32, (3 * LANES, LANES), 1)
    spread = (src == p * ROPE_FREQS + (dst & (ROPE_FREQS - 1))).astype(BF16)
    c = jnp.dot(c_s[...], spread, preferred_element_type=F32)
    s = jnp.dot(s_s[...], spread, preferred_element_type=F32)
    lane = lax.broadcasted_iota(jnp.int32, c.shape, 1)
    is_x1 = lane < ROT_DIM
    is_x2 = (lane >= ATT_HD) & (lane < ATT_HD + ROT_DIM)
    cos_ref[...] = jnp.where(is_x1 | is_x2, c, 1.0)
    sin_ref[...] = jnp.where(is_x1, -s, jnp.where(is_x2, s, 0.0))


def _rope_tables(positions):
    t = positions.size
    n = t // ROPE_GROUPS
    pos_c = jnp.repeat(positions.reshape(ROPE_GROUPS, n).T, ROPE_FREQS, axis=1)
    inv = jnp.power(ROPE_THETA, -jnp.arange(0, ROT_DIM, 2, dtype=F32) / ROT_DIM)
    inv_lane = jnp.tile(inv, ROPE_GROUPS).reshape(1, LANES)
    out = jax.ShapeDtypeStruct((t, LANES), F32)
    return pl.pallas_call(
        _rope_table_kernel,
        grid=(ROPE_GROUPS,),
        in_specs=[pl.BlockSpec((n, LANES), lambda p: (0, 0)),
                  pl.BlockSpec((1, LANES), lambda p: (0, 0))],
        out_specs=[pl.BlockSpec((n, LANES), lambda p: (p, 0))] * 2,
        out_shape=[out] * 2,
        scratch_shapes=[pltpu.VMEM((n, 3 * LANES), BF16)] * 2,
        compiler_params=pltpu.CompilerParams(dimension_semantics=("arbitrary",)),
        name="rope_tables",
    )(pos_c, inv_lane)


def _inproj_kernel(*refs, tm, tn, prepare_weights):
    j = pl.program_id(1)
    rb = 128
    n_q = ATT_W // tn
    n_rope = 2 * ATT_W // tn
    if prepare_weights:
        x_ref, g_ref, w32_ref, cos_ref, sin_ref = refs[:5]
        late32 = refs[5:5 + N_LATE_WEIGHTS]
        z_ref, wout_ref = refs[5 + N_LATE_WEIGHTS:7 + N_LATE_WEIGHTS]
        late16 = refs[7 + N_LATE_WEIGHTS:7 + 2 * N_LATE_WEIGHTS]
        u_ref, w_ref = refs[7 + 2 * N_LATE_WEIGHTS:]
        _cast_weight_tile(w32_ref, (w_ref, wout_ref), j, n_rope)
        for src, dst in zip(late32, late16):
            dst[...] = src[...].astype(BF16)
    else:
        x_ref, g_ref, w_ref, cos_ref, sin_ref, _, z_ref, u_ref = refs

    def norm_rows(r0, n):
        for i in range(n // rb):
            rows = slice(r0 + i * rb, r0 + (i + 1) * rb)
            xb = x_ref[rows, :]
            ms = jnp.mean(xb * xb, axis=-1, keepdims=True)
            u_ref[rows, :] = ((xb * lax.rsqrt(ms + EPS)) * g_ref[...]).astype(BF16)

    def project_rotate(r0, n):
        rows = slice(r0, r0 + n)
        scale = jnp.where(j < n_q, ATT_HD ** -0.5 * LOG2E, 1.0).astype(F32)
        c = cos_ref[rows, :] * scale
        s = sin_ref[rows, :] * scale
        for cb in range(tn // INPROJ_ROPE_CHUNK):
            cols = slice(cb * INPROJ_ROPE_CHUNK, (cb + 1) * INPROJ_ROPE_CHUNK)
            zc = jnp.dot(u_ref[rows, :], w_ref[:, cols], preferred_element_type=F32)
            parts = [zc[:, b * LANES:(b + 1) * LANES] for b in range(INPROJ_ROPE_CHUNK // LANES)]
            z_ref[rows, cols] = jnp.concatenate(
                [blk * c + pltpu.roll(blk, LANES // 2, 1) * s for blk in parts], axis=1)

    @pl.when(j == 0)
    def _():
        for r0 in range(0, tm, tm // 2):
            norm_rows(r0, tm // 2)
            project_rotate(r0, tm // 2)

    @pl.when((j > 0) & (j < n_rope))
    def _():
        project_rotate(0, tm)

    @pl.when(j >= n_rope)
    def _():
        z_ref[...] = jnp.dot(u_ref[...], w_ref[...], preferred_element_type=F32)


def _inproj(x2, g_pre, w_in, cos_t, sin_t, late_weights):
    assert len(late_weights) == N_LATE_WEIGHTS
    t = x2.shape[0]
    tm = min(INPROJ_TM, t)
    g2 = g_pre.reshape(1, D_MODEL)
    z_shape = jax.ShapeDtypeStruct((t, IN_W), F32)
    params = pltpu.CompilerParams(dimension_semantics=("parallel", "arbitrary"), vmem_limit_bytes=VMEM_LIMIT)
    u_scratch = pltpu.VMEM((tm, D_MODEL), BF16)

    def specs(tn, first):
        row = lambda i, j: (i + first, 0)
        return ([pl.BlockSpec((tm, D_MODEL), row),
                 pl.BlockSpec((1, D_MODEL), lambda i, j: (0, 0)),
                 pl.BlockSpec((D_MODEL, tn), lambda i, j: (0, j)),
                 pl.BlockSpec((tm, LANES), row),
                 pl.BlockSpec((tm, LANES), row)],
                pl.BlockSpec((tm, tn), lambda i, j: (i + first, j)))

    tn = INPROJ_PREP_TN
    n_steps = IN_W // tn
    in_specs, z_spec = specs(tn, 0)
    assert n_steps >= LATE_WEIGHT_BLOCKS
    late_specs = [pl.BlockSpec((w.shape[0] // LATE_WEIGHT_BLOCKS, w.shape[1]),
                               lambda i, j: (jnp.minimum(j, LATE_WEIGHT_BLOCKS - 1), 0)) for w in late_weights]
    outs = pl.pallas_call(
        functools.partial(_inproj_kernel, tm=tm, tn=tn, prepare_weights=True),
        grid=(1, n_steps),
        in_specs=in_specs + late_specs,
        out_specs=[z_spec, pl.BlockSpec((D_MODEL, tn), lambda i, j: (0, j))] + late_specs,
        out_shape=[z_shape, jax.ShapeDtypeStruct((D_MODEL, IN_W), BF16)]
        + [jax.ShapeDtypeStruct(w.shape, BF16) for w in late_weights],
        scratch_shapes=[u_scratch, pltpu.VMEM((D_MODEL, tn), BF16)],
        compiler_params=params,
        name="inproj_first",
    )(x2, g2, w_in, cos_t, sin_t, *late_weights)
    z, w_bf16, late_bf16 = outs[0], outs[1], outs[2:]
    if t == tm:
        return z, late_bf16
    tn = INPROJ_TN
    in_specs, z_spec = specs(tn, 1)
    z = pl.pallas_call(
        functools.partial(_inproj_kernel, tm=tm, tn=tn, prepare_weights=False),
        grid=(t // tm - 1, IN_W // tn),
        in_specs=in_specs + [pl.BlockSpec(memory_space=pl.ANY)],
        out_specs=z_spec,
        out_shape=z_shape,
        input_output_aliases={len(in_specs): 0},
        scratch_shapes=[u_scratch],
        compiler_params=params,
        name="inproj",
    )(x2, g2, w_bf16, cos_t, sin_t, z)
    return z, late_bf16


def _att_geometry(seq):
    geo = {}
    for d in DILATIONS:
        n_cls = seq // d
        dense = n_cls <= ATT_DENSE_MAX
        geo[d] = (dense, n_cls if dense else ATT_QB, n_cls if dense else ATT_KB)
    return geo


def _att_kernel(q_ref, k_ref, v_ref, g_ref, zc_ref, y_ref, mt_s, lt_s, nt_s, lsed_s, od_s,
                bias_s, biasd_s, p_s, pd_s, cls_s, cls_sem, *, seq):
    geo = _att_geometry(seq)
    tiled = [d for d in DILATIONS if not geo[d][0]]
    dense = [d for d in DILATIONS if geo[d][0]]
    n_tiles = seq // ATT_TILE

    n_pairs = pl.num_programs(1)
    step = pl.program_id(0) * n_pairs + pl.program_id(1)
    n_steps = pl.num_programs(0) * n_pairs
    slot = step % 2

    def class_copies(at_step):
        b, h, sl = at_step // n_pairs, at_step % n_pairs, at_step % 2
        copies = []
        for d in dense:
            n_cls = seq // d
            for which, c0 in enumerate((COL_AQ, COL_AK, COL_AV)):
                for r in range(d):
                    src = zc_ref.at[pl.ds(b * n_cls, n_cls), r, pl.ds((c0 + h) * LANES, LANES)]
                    copies.append(pltpu.make_async_copy(src, cls_s.at[sl, which, r], cls_sem.at[sl, which, r]))
        return copies

    @pl.when(step == 0)
    def _():
        for cp in class_copies(step):
            cp.start()

    @pl.when(step + 1 < n_steps)
    def _():
        for cp in class_copies(step + 1):
            cp.start()
    ones_kv = {kb: jnp.ones((kb, LANES), BF16) for kb in {geo[d][2] for d in DILATIONS}}

    def lanes(rows):
        lane = lax.broadcasted_iota(jnp.int32, (rows, LANES), 1)
        return lane < ATT_HD, _head_a_lanes(lane)

    def band_bias(qb, kb, delta):
        qa = lax.broadcasted_iota(jnp.int32, (2 * qb, kb), 0) & (qb - 1)
        ka = lax.broadcasted_iota(jnp.int32, (2 * qb, kb), 1)
        return jnp.where(jnp.abs(ka - qa + delta) <= HALF, 0.0, NEG)

    @pl.when(step == 0)
    def _():
        for idx, delta in enumerate((-HALF, 0, -2 * HALF)):
            bias_s[idx] = band_bias(ATT_QB, ATT_KB, delta)
        for d in dense:
            biasd_s[...] = band_bias(geo[d][1], geo[d][2], 0)

    def rows_of(start, size, d):
        return pl.ds(start, size) if d == 1 else pl.ds(start, size, stride=d)

    def score_block(d, r, a0, bias, p_ref):
        _, qb, kb = geo[d]
        head_a, qk_head_a = lanes(qb)
        if geo[d][0]:
            q2 = cls_s[slot, 0, r]
            kk = cls_s[slot, 1, r].astype(BF16)
        else:
            ka0 = jnp.clip(a0 - HALF, 0, seq // d - kb)
            q2 = q_ref[rows_of(r + d * a0, qb, d), :]
            kk = k_ref[rows_of(r + d * ka0, kb, d), :].astype(BF16)
        qs = jnp.concatenate([jnp.where(qk_head_a, q2, 0.0), jnp.where(qk_head_a, 0.0, q2)], axis=0)
        s = _bdot_nt(qs, kk) + bias
        m = jnp.max(s, axis=-1, keepdims=True)
        p_ref[...] = jnp.exp2(s - m).astype(BF16)
        mb = jnp.broadcast_to(m, (2 * qb, LANES))
        return jnp.where(head_a, mb[:qb], mb[qb:])

    def value_block(d, r, a0, p_ref):
        _, qb, kb = geo[d]
        head_a, _ = lanes(qb)
        if geo[d][0]:
            vv = cls_s[slot, 2, r].astype(BF16)
        else:
            ka0 = jnp.clip(a0 - HALF, 0, seq // d - kb)
            vv = v_ref[rows_of(r + d * ka0, kb, d), :].astype(BF16)
        nl = jnp.dot(p_ref[...], jnp.concatenate([vv, ones_kv[kb]], axis=1), preferred_element_type=F32)
        return (jnp.where(head_a, nl[:qb, LANES:], nl[qb:, LANES:]),
                jnp.where(head_a, nl[:qb, :LANES], nl[qb:, :LANES]))

    for cp in class_copies(step):
        cp.wait()
    for gi, d in enumerate(dense):
        def dense_scores(r, c, gi=gi, d=d):
            lsed_s[gi * d + r] = score_block(d, r, 0, biasd_s[...], pd_s.at[r])
            return c

        def dense_values(r, c, gi=gi, d=d):
            l, n = value_block(d, r, 0, pd_s.at[r])
            od_s[gi * d + r] = (n / l).astype(BF16)
            lsed_s[gi * d + r] = lsed_s[gi * d + r] + jnp.log2(l)
            return c
        lax.fori_loop(0, d, dense_scores, 0, unroll=ATT_DENSE_UNROLL)
        lax.fori_loop(0, d, dense_values, 0, unroll=ATT_DENSE_UNROLL)

    cb = 256

    def to_natural(d):
        i = lax.broadcasted_iota(jnp.int32, (cb, cb), 0)
        j = lax.broadcasted_iota(jnp.int32, (cb, cb), 1)
        return (j == (i % d) * (cb // d) + i // d).astype(BF16)

    def tile_body(tj, carry):
        tile0 = tj * ATT_TILE
        for gi, d in enumerate(tiled):
            n_cls = seq // d
            per_cls = ATT_TILE // d // ATT_QB
            n_blk = ATT_TILE // ATT_QB
            for bi in range(n_blk):
                r = bi // per_cls
                a0 = tile0 // d + (bi % per_cls) * ATT_QB
                which = jnp.where(a0 == 0, 1, jnp.where(a0 == n_cls - ATT_QB, 2, 0))
                mt_s[gi, rows_of(r + d * a0 - tile0, ATT_QB, d), :] = score_block(
                    d, r, a0, bias_s[which], p_s.at[bi])
            for bi in range(n_blk):
                r = bi // per_cls
                a0 = tile0 // d + (bi % per_cls) * ATT_QB
                dst = rows_of(r + d * a0 - tile0, ATT_QB, d)
                lt_s[gi, dst, :], nt_s[gi, dst, :] = value_block(d, r, a0, p_s.at[bi])

        def comb_body(ci, c):
            rows = pl.ds(pl.multiple_of(ci * cb, cb), cb)
            orow = pl.ds(pl.multiple_of(tile0 + ci * cb, cb), cb)
            parts = [(mt_s[gi, rows, :], lt_s[gi, rows, :], nt_s[gi, rows, :]) for gi in range(len(tiled))]
            for gi, d in enumerate(dense):
                per = cb // d
                a0 = pl.multiple_of((tile0 + ci * cb) // d, per)
                o_cls = jnp.concatenate([od_s[gi * d + r, pl.ds(a0, per), :] for r in range(d)], axis=0)
                lse_cls = jnp.concatenate([lsed_s[gi * d + r, pl.ds(a0, per), :] for r in range(d)], axis=0)
                perm = to_natural(d)
                o_nat = jnp.dot(perm, o_cls, preferred_element_type=F32)
                hi = lse_cls.astype(BF16)
                lo = (lse_cls - hi.astype(F32)).astype(BF16)
                lse2 = jnp.dot(perm, jnp.concatenate([hi, lo], axis=1), preferred_element_type=F32)
                parts.append((lse2[:, :LANES] + lse2[:, LANES:], None, o_nat))
            mx = functools.reduce(jnp.maximum, [part[0] for part in parts])
            num = jnp.zeros((cb, LANES), F32)
            den = jnp.zeros((cb, LANES), F32)
            for m, l, n in parts:
                w = jnp.exp2(m - mx)
                num = num + w * n
                den = den + (w if l is None else w * l)
            y_ref[orow, :] = ((num / den) * _silu(g_ref[orow, :])).astype(y_ref.dtype)
            return c
        lax.fori_loop(0, ATT_TILE // cb, comb_body, 0, unroll=True)
        return carry

    lax.fori_loop(0, n_tiles, tile_body, 0, unroll=2)


def _attention(z, batch, seq):
    t = batch * seq
    n_pairs = ATT_W // LANES

    def col(c0):
        return pl.BlockSpec((seq, LANES), lambda b, h, c0=c0: (b, c0 + h))

    geo = _att_geometry(seq)
    dense_geo = sorted({geo[d][1:] for d in DILATIONS if geo[d][0]})
    assert len(dense_geo) <= 1, "one band-mask table serves the dense groups"
    dqb, dkb = dense_geo[0] if dense_geo else (8, LANES)
    dense_d = [d for d in DILATIONS if geo[d][0]]
    assert len(dense_d) <= 1, "one class view of the projection output serves the dense group"
    n_dense = len(dense_d)
    dd = dense_d[0] if dense_d else 8
    z_cls = z.reshape(t // dd, dd, IN_W)
    tiled = pltpu.VMEM((max(len(DILATIONS) - n_dense, 1), ATT_TILE, LANES), F32)
    n_dense_cls = max(sum(dense_d), 1)
    return pl.pallas_call(
        functools.partial(_att_kernel, seq=seq),
        grid=(batch, n_pairs),
        in_specs=[col(COL_AQ), col(COL_AK), col(COL_AV), col(COL_AG), pl.BlockSpec(memory_space=pl.ANY)],
        out_specs=pl.BlockSpec((seq, LANES), lambda b, h: (b, h)),
        out_shape=jax.ShapeDtypeStruct((t, ATT_W), BF16),
        scratch_shapes=[tiled, tiled, tiled,
                        pltpu.VMEM((n_dense_cls, dqb, LANES), F32),
                        pltpu.VMEM((n_dense_cls, dqb, LANES), BF16),
                        pltpu.VMEM((3, 2 * ATT_QB, ATT_KB), F32),
                        pltpu.VMEM((2 * dqb, dkb), F32),
                        pltpu.VMEM((ATT_TILE // ATT_QB, 2 * ATT_QB, ATT_KB), BF16),
                        pltpu.VMEM((max(DILATIONS), 2 * dqb, dkb), BF16),
                        pltpu.VMEM((2, 3, n_dense_cls, dqb, LANES), F32),
                        pltpu.SemaphoreType.DMA((2, 3, n_dense_cls))],
        compiler_params=pltpu.CompilerParams(
            dimension_semantics=("arbitrary", "arbitrary"), vmem_limit_bytes=VMEM_LIMIT),
        name="band_attention",
    )(z, z, z, z, z_cls)


def _chunk_cumsum(g, tri):
    c = HG_CHUNK
    g1 = g.astype(BF16)
    g2 = (g - g1.astype(F32)).astype(BF16)
    terms = jnp.concatenate([g1, g2], axis=1)
    k = g.shape[1]
    outs = []
    for i in range(g.shape[0] // c):
        part = jnp.dot(tri, terms[i * c:(i + 1) * c, :], preferred_element_type=F32)
        outs.append(part[:, k:] + part[:, :k])
    return jnp.concatenate(outs, axis=0)


def _hgrn_kernel(hq_ref, ff_ref, fb_ref, hv_ref, hg_ref, lbf_ref, lbb_ref, ghg_ref, y_ref,
                 q_s, kf_s, kb_s, bf_s, bb_s, o_s,
                 qtf_s, ktf_s, qtb_s, ktb_s, v16_s,
                 emf_s, emb_s, stf_s, stb_s, *, seq, layer):
    C = HG_CHUNK
    n_chunks = seq // C

    def lower_bound(ref):
        a = ref[...]
        e = jnp.exp(a - jnp.max(a, axis=0, keepdims=True))
        return jnp.sum(e[0:layer + 1, :], axis=0, keepdims=True) / jnp.sum(e, axis=0, keepdims=True)

    lbf = lower_bound(lbf_ref)
    lbb = lower_bound(lbb_ref)

    ti = lax.broadcasted_iota(jnp.int32, (C, C), 0)
    si = lax.broadcasted_iota(jnp.int32, (C, C), 1)
    tri_f = si <= ti
    tri_b = si >= ti

    pr = HG_PRE_ROWS
    n_blocks = seq // pr
    cpb = pr // C
    fwd = (ff_ref, lbf, kf_s, bf_s, qtf_s, ktf_s, emf_s, False)
    bwd = (fb_ref, lbb, kb_s, bb_s, qtb_s, ktb_s, emb_s, True)

    def per_chunk_rows(x3):
        return jnp.broadcast_to(x3, (cpb, C, HG_DK)).reshape(pr, HG_DK)

    def block_rows(i):
        return pl.ds(_aligned(i * pr, pr), pr)

    def prepare_common(i):
        rows = block_rows(i)
        q_s[rows, :] = _silu(hq_ref[rows, :])
        o_s[rows, :] = jnp.zeros((pr, HG_DK), F32)
        v16_s[rows, :] = hv_ref[rows, :].astype(BF16)

    def prepare_direction(i, d, lo):
        f_ref, lb, k_s, b_s, qt_s, kt_s, em_s, reverse = d
        rows = block_rows(i)
        q = q_s[rows, :]
        f = lb + (1.0 - lb) * _sigmoid(f_ref[rows, :])
        kk = 1.0 - f
        tri = tri_b if reverse else tri_f
        b = _chunk_cumsum(jnp.log(f), tri.astype(BF16))
        last = 0 if reverse else C - 1
        btot = b.reshape(cpb, C, HG_DK)[:, last:last + 1, :]
        mid = 0.5 * btot
        midr = per_chunk_rows(mid)
        k_s[rows, :] = kk
        b_s[rows, :] = b
        qt_s[rows, :] = (q * jnp.exp(b - midr)).astype(BF16)
        kt_s[rows, :] = (kk * jnp.exp(midr - b)).astype(BF16)
        em_s[pl.ds(_aligned(i * cpb, cpb), cpb), :] = jnp.exp(mid).reshape(cpb, HG_DK)
        return jnp.minimum(lo, btot.reshape(cpb, HG_DK))

    def chunk(ci, tri, d, st_ref, exact):
        _, _, k_s, b_s, qt_s, kt_s, em_s, reverse = d
        r0 = _aligned(ci * C, C)
        rows = pl.ds(r0, C)
        v = v16_s[rows, :]
        st = st_ref[...]
        if exact:
            q = q_s[rows, :]
            b = b_s[rows, :]
            kk = k_s[rows, :]
            btot = b_s[pl.ds(r0 + (0 if reverse else C - 1), 1), :]

            def col_body(s, att):
                w = q * k_s[pl.ds(r0 + s, 1), :] * jnp.exp(jnp.minimum(b - b_s[pl.ds(r0 + s, 1), :], 0.0))
                return att + jnp.where(si == s, jnp.sum(w, axis=-1, keepdims=True), 0.0)
            att = lax.fori_loop(0, C, col_body, jnp.zeros((C, C), F32))
            o = _bdot(jnp.where(tri, att, 0.0), v) + _bdot_nt(q * jnp.exp(b), st)
            st_ref[...] = st * jnp.exp(btot) + _bdot_tn(v, kk * jnp.exp(btot - b))
        else:
            em = em_s[pl.ds(ci, 1), :]
            qt = qt_s[rows, :]
            kt = kt_s[rows, :]
            att = jnp.where(tri, _bdot_nt(qt, kt), 0.0)
            o = _bdot(att, v) + _bdot_nt(qt, st * em)
            st_ref[...] = st * (em * em) + _bdot_tn(v, kt) * em
        o_s[rows, :] = o_s[rows, :] + o

    def finish_rows(i):
        rows = block_rows(i)
        o = o_s[rows, :]
        ms = jnp.mean(o * o, axis=-1, keepdims=True)
        y = (o * lax.rsqrt(ms + EPS)) * ghg_ref[...]
        y_ref[rows, :] = (y * _silu(hg_ref[rows, :])).astype(y_ref.dtype)

    def factorised_ok(lo):
        return jnp.min(lo) >= HG_FAST_MIN_LOGDECAY

    lo0 = jnp.zeros((cpb, HG_DK), F32)
    stf_s[...] = jnp.zeros_like(stf_s)
    stb_s[...] = jnp.zeros_like(stb_s)

    lo = lo0
    prepare_common(0)
    lo = prepare_direction(0, fwd, lo)
    for i in range(n_blocks):
        if i + 1 < n_blocks:
            prepare_common(i + 1)
            lo = prepare_direction(i + 1, fwd, lo)
        for c in range(cpb):
            chunk(i * cpb + c, tri_f, fwd, stf_s, False)
    lo = prepare_direction(n_blocks - 1, bwd, lo)
    for i in reversed(range(n_blocks)):
        if i > 0:
            lo = prepare_direction(i - 1, bwd, lo)
        for c in reversed(range(cpb)):
            chunk(i * cpb + c, tri_b, bwd, stb_s, False)
        finish_rows(i)

    @pl.when(jnp.logical_not(factorised_ok(lo)))
    def _():
        stf_s[...] = jnp.zeros_like(stf_s)
        stb_s[...] = jnp.zeros_like(stb_s)

        def clear(i, carry):
            o_s[block_rows(i), :] = jnp.zeros((pr, HG_DK), F32)
            return carry
        lax.fori_loop(0, n_blocks, clear, 0)

        def both(n, carry):
            chunk(n, tri_f, fwd, stf_s, True)
            chunk(n_chunks - 1 - n, tri_b, bwd, stb_s, True)
            return carry
        lax.fori_loop(0, n_chunks, both, 0)

        def finish(i, carry):
            finish_rows(i)
            return carry
        lax.fori_loop(0, n_blocks, finish, 0)


def _hgrn(z, lb_fwd, lb_bwd, g_hg, batch, seq, layer):
    t = batch * seq

    def col(c0):
        return pl.BlockSpec((seq, LANES), lambda b, h, c0=c0: (b, c0 + h))

    n_lb = lb_fwd.shape[0]
    lb_spec = pl.BlockSpec((n_lb, HG_DK), lambda b, h: (0, h))
    seq_f32 = pltpu.VMEM((seq, HG_DK), F32)
    seq_bf16 = pltpu.VMEM((seq, HG_DK), BF16)
    chunk_f32 = pltpu.VMEM((seq // HG_CHUNK, HG_DK), F32)
    state = pltpu.VMEM((HG_DK, HG_DK), F32)
    return pl.pallas_call(
        functools.partial(_hgrn_kernel, seq=seq, layer=layer),
        grid=(batch, HG_HEADS),
        in_specs=[col(COL_HQ), col(COL_HFF), col(COL_HFB), col(COL_HV), col(COL_HG),
                  lb_spec, lb_spec, pl.BlockSpec((1, HG_DK), lambda b, h: (0, 0))],
        out_specs=pl.BlockSpec((seq, HG_DK), lambda b, h: (b, h)),
        out_shape=jax.ShapeDtypeStruct((t, HG_W), BF16),
        scratch_shapes=[seq_f32] * 6 + [seq_bf16] * 5 + [chunk_f32, chunk_f32, state, state],
        compiler_params=pltpu.CompilerParams(
            dimension_semantics=("parallel", "parallel"), vmem_limit_bytes=VMEM_LIMIT),
        name="hgrn2",
    )(z, z, z, z, z, lb_fwd, lb_bwd, g_hg.reshape(1, HG_DK))


def _rms(y, g):
    ms = jnp.mean(y * y, axis=-1, keepdims=True)
    return (y * lax.rsqrt(ms + EPS)) * g


def _outproj_kernel(x_ref, ya_ref, yh_ref, p_ref, wo_ref, wpg_ref, wpp_ref, gpost_ref, gple_ref, o_ref):
    sub = x_ref.shape[0] // OUT_SUBBLOCKS
    for r in range(OUT_SUBBLOCKS):
        rows = slice(r * sub, (r + 1) * sub)
        y = (jnp.dot(ya_ref[rows, :], wo_ref[0:ATT_W, :], preferred_element_type=F32)
             + jnp.dot(yh_ref[rows, :], wo_ref[ATT_W:, :], preferred_element_type=F32))
        h1 = x_ref[rows, :] + _rms(y, gpost_ref[...])
        gate = jax.nn.sigmoid(jnp.dot(h1.astype(BF16), wpg_ref[...], preferred_element_type=F32))
        e = jnp.dot(p_ref[rows, :].astype(BF16), wpp_ref[...], preferred_element_type=F32)
        o_ref[rows, :] = h1 + _rms(gate * e, gple_ref[...])


def _outproj(x2, y_att, y_hg, p2, w_out, w_pg, w_pp, g_post, g_ple):
    t = x2.shape[0]
    tm = min(OUT_TM, t)
    row = lambda i: (i, 0)
    const = lambda i: (0, 0)
    resident = dict(pipeline_mode=pl.Buffered(1))
    return pl.pallas_call(
        _outproj_kernel,
        grid=(t // tm,),
        in_specs=[pl.BlockSpec((tm, D_MODEL), row),
                  pl.BlockSpec((tm, ATT_W), row),
                  pl.BlockSpec((tm, HG_W), row),
                  pl.BlockSpec((tm, PLE_DIM), row),
                  pl.BlockSpec((ATT_W + HG_W, D_MODEL), const, **resident),
                  pl.BlockSpec((D_MODEL, D_MODEL), const, **resident),
                  pl.BlockSpec((PLE_DIM, D_MODEL), const, **resident),
                  pl.BlockSpec((1, D_MODEL), const),
                  pl.BlockSpec((1, D_MODEL), const)],
        out_specs=pl.BlockSpec((tm, D_MODEL), row),
        out_shape=jax.ShapeDtypeStruct((t, D_MODEL), F32),
        compiler_params=pltpu.CompilerParams(
            dimension_semantics=("parallel",), vmem_limit_bytes=VMEM_LIMIT),
        name="outproj_ple",
    )(x2, y_att, y_hg, p2, w_out, w_pg, w_pp, g_post.reshape(1, D_MODEL), g_ple.reshape(1, D_MODEL))


def kernel(x, p, positions, w_in, w_out, g_pre, g_post, g_hg, lb_fwd, lb_bwd, w_pg, w_pp, g_ple):
    depth = w_in.shape[0]
    batch, seq, _ = x.shape
    assert seq % ATT_TILE == 0 and seq // max(DILATIONS) >= ATT_KB
    t = batch * seq
    cos_t, sin_t = _rope_tables(positions)
    h = x.reshape(t, D_MODEL)
    for i in range(depth):
        z, (w_out16, w_pg16, w_pp16) = _inproj(h, g_pre[i], w_in[i], cos_t, sin_t,
                                               (w_out[i], w_pg[i], w_pp[i]))
        y_att = _attention(z, batch, seq)
        y_hg = _hgrn(z, lb_fwd, lb_bwd, g_hg[i], batch, seq, i)
        h = _outproj(h, y_att, y_hg, p[i].reshape(t, PLE_DIM), w_out16, w_pg16, w_pp16,
                     g_post[i], g_ple[i])
    return h.reshape(batch, seq, D_MODEL)
```
